```python
import math
import jax
import jax.numpy as jnp
from jax import lax
import numpy as np

D_MODEL = 1024
BATCH = 16
SEQ = 2048
DEPTH = 2

HEAD_DIM = 64
A_HEADS = 6
B_HEADS = 5
C_HEADS = 5
MIX_WIDTH = (A_HEADS + B_HEADS + C_HEADS) * HEAD_DIM
DILATED_PATTERNS = ((128, 1), (512, 4), (2048, 16))
MLA_Q_RANK = 256
MLA_KV_RANK = 128
MLA_NOPE_DIM = 64
MLA_ROPE_DIM = 32
MLA_V_DIM = HEAD_DIM
IDX_HEADS = 8
IDX_DIM = 64
IDX_ROPE_DIM = 32
TOPK_MAX = 256
D_FF = 2816
MACARON_WEIGHT = 0.5
N_BUCKETS = 32
MAX_DISTANCE = 128
ROPE_THETA = 10000.0
QBLOCK = 128
RMS_EPS = 1e-6
N_MOD = 9
NEG_INF = -1e30

IN_SPLITS = (A_HEADS * HEAD_DIM, A_HEADS * HEAD_DIM, A_HEADS * HEAD_DIM,
             MLA_Q_RANK, MLA_KV_RANK, MLA_ROPE_DIM,
             C_HEADS * HEAD_DIM, C_HEADS * HEAD_DIM, C_HEADS * HEAD_DIM,
             IDX_HEADS * IDX_DIM, IDX_DIM, IDX_HEADS)
IN_WIDTH = sum(IN_SPLITS)

kernel_name = 'hybrid_dilated_mla_dsa_macaron_adaln'


def rmsnorm(x, g):
    xf = x.astype(jnp.float32)
    y = xf * lax.rsqrt(jnp.mean(xf * xf, axis=-1, keepdims=True) + RMS_EPS)
    return (y * g.astype(jnp.float32)).astype(x.dtype)


def modulate(x, shift, scale):
    return x * (1 + scale[:, None, :]) + shift[:, None, :]


def swiglu(x, w_gu, w_down):
    g, u = jnp.split(x @ w_gu, 2, axis=-1)
    return (jax.nn.silu(g) * u) @ w_down


def t5_bucket(dist):
    n = jnp.maximum(dist, 0)
    max_exact = N_BUCKETS // 2
    nf = jnp.maximum(n, 1).astype(jnp.float32)
    large = max_exact + (jnp.log(nf / max_exact) / math.log(MAX_DISTANCE / max_exact)
                         * (N_BUCKETS - max_exact)).astype(jnp.int32)
    large = jnp.minimum(large, N_BUCKETS - 1)
    return jnp.where(n < max_exact, n, large)


def rope(x, pos):
    half = x.shape[-1] // 2
    inv = ROPE_THETA ** (-jnp.arange(half, dtype=jnp.float32) / half)
    ang = pos.astype(jnp.float32)[..., None] * inv
    cos = jnp.cos(ang)[:, :, None, :]
    sin = jnp.sin(ang)[:, :, None, :]
    x1 = x[..., :half].astype(jnp.float32)
    x2 = x[..., half:].astype(jnp.float32)
    return jnp.concatenate([x1 * cos - x2 * sin, x2 * cos + x1 * sin], axis=-1).astype(x.dtype)


def dilated_branch(q, k, v, window, dilation, bias_tab):
    B, T, H, Dh = q.shape
    n = T // dilation
    steps = window // dilation
    n_pad = -(-n // QBLOCK) * QBLOCK
    nb = n_pad // QBLOCK

    def regroup(z):
        z = z.reshape(B, n, dilation, H, Dh).transpose(0, 2, 1, 3, 4)
        z = jnp.pad(z, ((0, 0), (0, 0), (0, n_pad - n), (0, 0), (0, 0)))
        return z.reshape(B, dilation, nb, QBLOCK, H, Dh)

    def band(z):
        prev = jnp.pad(z, ((0, 0), (0, 0), (1, 0), (0, 0), (0, 0), (0, 0)))[:, :, :-1]
        return jnp.concatenate([prev, z], axis=3)

    qb = regroup(q)
    kband = band(regroup(k))
    vband = band(regroup(v))
    logits = jnp.einsum('bdnqhe,bdnkhe->bdnhqk', qb, kband).astype(jnp.float32) * HEAD_DIM ** -0.5
    qi = jnp.arange(QBLOCK)[:, None]
    kj = jnp.arange(2 * QBLOCK)[None, :]
    step = QBLOCK + qi - kj
    bias = bias_tab[t5_bucket(step * dilation)].astype(jnp.float32).transpose(2, 0, 1)
    in_band = (step >= 0) & (step <= steps)
    key_idx = jnp.arange(nb)[:, None, None] * QBLOCK + kj[None] - QBLOCK
    valid = in_band[None] & (key_idx >= 0)
    logits = jnp.where(valid[None, None, :, None], logits + bias, NEG_INF)
    m = jnp.max(logits, axis=-1, keepdims=True)
    p = jnp.exp(logits - m)
    s = jnp.sum(p, axis=-1, keepdims=True)
    o = jnp.einsum('bdnhqk,bdnkhe->bdnqhe', (p / s).astype(v.dtype), vband)
    lse = (m + jnp.log(s))[..., 0]
    o = o.reshape(B, dilation, n_pad, H, Dh)[:, :, :n].transpose(0, 2, 1, 3, 4).reshape(B, T, H, Dh)
    lse = lse.transpose(0, 1, 2, 4, 3).reshape(B, dilation, n_pad, H)[:, :, :n]
    lse = lse.transpose(0, 2, 1, 3).reshape(B, T, H)
    return o, lse


def dilated_attention(q, k, v, bias_tab):
    results = [dilated_branch(q, k, v, w, d, bias_tab) for (w, d) in DILATED_PATTERNS]
    outs = jnp.stack([r[0] for r in results], axis=0).astype(jnp.float32)
    lses = jnp.stack([r[1] for r in results], axis=0)
    wts = jax.nn.softmax(lses, axis=0)
    return jnp.sum(wts[..., None] * outs, axis=0).astype(q.dtype)


def causal_block_attention(q, k, v, scale):
    B, T, H, _ = q.shape
    kpos = jnp.arange(T)

    def one_block(i):
        start = i * QBLOCK
        q_b = lax.dynamic_slice_in_dim(q, start, QBLOCK, axis=1)
        qpos = start + jnp.arange(QBLOCK)
        logits = jnp.einsum('bqhd,bkhd->bhqk', q_b, k).astype(jnp.float32) * scale
        logits = jnp.where(kpos[None, :] <= qpos[:, None], logits, NEG_INF)
        p = jax.nn.softmax(logits, axis=-1)
        return jnp.einsum('bhqk,bkhd->bqhd', p.astype(v.dtype), v)

    out = lax.map(one_block, jnp.arange(T // QBLOCK))
    return out.transpose(1, 0, 2, 3, 4).reshape(B, T, H, v.shape[-1])


def mla_attention(cq, ckv, k_rope_raw, pos, q_norm, w_uq, kv_norm, w_ukv):
    B, T, _ = cq.shape
    q = (rmsnorm(cq, q_norm) @ w_uq).reshape(B, T, B_HEADS, MLA_NOPE_DIM + MLA_ROPE_DIM)
    q = jnp.concatenate([q[..., :MLA_NOPE_DIM], rope(q[..., MLA_NOPE_DIM:], pos)], axis=-1)
    kv = (rmsnorm(ckv, kv_norm) @ w_ukv).reshape(B, T, B_HEADS, MLA_NOPE_DIM + MLA_V_DIM)
    k_nope = kv[..., :MLA_NOPE_DIM]
    v = kv[..., MLA_NOPE_DIM:]
    k_r = rope(k_rope_raw[:, :, None, :], pos)
    k = jnp.concatenate([k_nope, jnp.broadcast_to(k_r, (B, T, B_HEADS, MLA_ROPE_DIM))], axis=-1)
    return causal_block_attention(q, k, v, (MLA_NOPE_DIM + MLA_ROPE_DIM) ** -0.5)


def dsa_attention(q, k, v, iq, ik, iw, pos, ik_norm, bias_tab):
    B, T, H, Dh = q.shape
    n_sel = min(TOPK_MAX, T // 4)
    iq = iq.reshape(B, T, IDX_HEADS, IDX_DIM)
    iq = jnp.concatenate([rope(iq[..., :IDX_ROPE_DIM], pos), iq[..., IDX_ROPE_DIM:]], axis=-1)
    ik = rmsnorm(ik, ik_norm)[:, :, None, :]
    ik = jnp.concatenate([rope(ik[..., :IDX_ROPE_DIM], pos), ik[..., IDX_ROPE_DIM:]], axis=-1)[:, :, 0]
    iw = iw.astype(jnp.float32) * (IDX_HEADS ** -0.5 * IDX_DIM ** -0.5)
    kpos = jnp.arange(T)
    bidx = jnp.arange(B)[:, None, None]

    def one_block(i):
        start = i * QBLOCK
        qpos = start + jnp.arange(QBLOCK)
        iq_b = lax.dynamic_slice_in_dim(iq, start, QBLOCK, axis=1)
        iw_b = lax.dynamic_slice_in_dim(iw, start, QBLOCK, axis=1)
        rel = jax.nn.relu(jnp.einsum('bqhd,bkd->bqhk', iq_b, ik).astype(jnp.float32))
        score = jnp.einsum('bqhk,bqh->bqk', rel, iw_b)
        score = jnp.where(kpos[None, None, :] <= qpos[None, :, None], score, -jnp.inf)
        _, idx = lax.top_k(score, n_sel)
        dist = qpos[None, :, None] - idx
        kg = k[bidx, idx]
        vg = v[bidx, idx]
        q_b = lax.dynamic_slice_in_dim(q, start, QBLOCK, axis=1)
        logits = jnp.einsum('bqhd,bqkhd->bhqk', q_b, kg).astype(jnp.float32) * Dh ** -0.5
        logits = logits + bias_tab[t5_bucket(dist)].astype(jnp.float32).transpose(0, 3, 1, 2)
        logits = jnp.where((dist >= 0)[:, None], logits, NEG_INF)
        p = jax.nn.softmax(logits, axis=-1)
        return jnp.einsum('bhqk,bqkhd->bqhd', p.astype(v.dtype), vg)

    out = lax.map(one_block, jnp.arange(T // QBLOCK))
    return out.transpose(1, 0, 2, 3, 4).reshape(B, T, H, Dh)


def hybrid_mixer(n, pos, rel_bias, w_in, q_norm, w_uq, kv_norm, w_ukv, ik_norm, w_out):
    B, T, _ = n.shape
    offsets = np.cumsum(IN_SPLITS)[:-1].tolist()
    (qa, ka, va, cq, ckv, kr, qc, kc, vc, iq, ik, iw) = jnp.split(n @ w_in, offsets, axis=-1)
    bias_a = rel_bias[:, :A_HEADS]
    bias_c = rel_bias[:, A_HEADS:]
    oa = dilated_attention(qa.reshape(B, T, A_HEADS, HEAD_DIM), ka.reshape(B, T, A_HEADS, HEAD_DIM),
                           va.reshape(B, T, A_HEADS, HEAD_DIM), bias_a)
    ob = mla_attention(cq, ckv, kr, pos, q_norm, w_uq, kv_norm, w_ukv)
    oc = dsa_attention(qc.reshape(B, T, C_HEADS, HEAD_DIM), kc.reshape(B, T, C_HEADS, HEAD_DIM),
                       vc.reshape(B, T, C_HEADS, HEAD_DIM), iq, ik, iw, pos, ik_norm, bias_c)
    o = jnp.concatenate([oa.reshape(B, T, -1), ob.reshape(B, T, -1), oc.reshape(B, T, -1)], axis=-1)
    return o @ w_out


def setup_inputs(seed: int = 0) -> dict:
    key = jax.random.key(seed)
    ks = iter(jax.random.split(key, 32))
    f32 = jnp.float32
    L, D = DEPTH, D_MODEL

    def dense(shape, fan_in, gain=1.0):
        return gain * fan_in ** -0.5 * jax.random.normal(next(ks), shape, f32)

    def norm_gain(shape):
        return 1.0 + 0.05 * jax.random.normal(next(ks), shape, f32)

    x = jax.random.normal(next(ks), (BATCH, SEQ, D), f32)
    c = jax.random.normal(next(ks), (BATCH, D), f32)
    offset = jax.random.randint(next(ks), (BATCH, 1), 0, 4096, dtype=jnp.int32)
    positions = offset + jnp.arange(SEQ, dtype=jnp.int32)[None, :]
    rel_bias = 0.5 * jax.random.normal(next(ks), (N_BUCKETS, A_HEADS + C_HEADS), f32)
    ada_w = dense((L, D, N_MOD * D), D, 0.5)
    ada_b = 0.02 * jax.random.normal(next(ks), (L, N_MOD * D), f32)
    norm_ffn1 = norm_gain((L, D))
    ffn1_w_gu = dense((L, D, 2 * D_FF), D)
    ffn1_w_down = dense((L, D_FF, D), D_FF)
    norm_mix = norm_gain((L, D))
    w_in = dense((L, D, IN_WIDTH), D)
    mla_q_norm = norm_gain((L, MLA_Q_RANK))
    mla_w_uq = dense((L, MLA_Q_RANK, B_HEADS * (MLA_NOPE_DIM + MLA_ROPE_DIM)), MLA_Q_RANK)
    mla_kv_norm = norm_gain((L, MLA_KV_RANK))
    mla_w_ukv = dense((L, MLA_KV_RANK, B_HEADS * (MLA_NOPE_DIM + MLA_V_DIM)), MLA_KV_RANK)
    idx_k_norm = norm_gain((L, IDX_DIM))
    w_out = dense((L, MIX_WIDTH, D), MIX_WIDTH)
    norm_ffn2 = norm_gain((L, D))
    ffn2_w_gu = dense((L, D, 2 * D_FF), D)
    ffn2_w_down = dense((L, D_FF, D), D_FF)
    final_norm = norm_gain((D,))
    return {'x': x, 'c': c, 'positions': positions, 'rel_bias': rel_bias,
            'ada_w': ada_w, 'ada_b': ada_b,
            'norm_ffn1': norm_ffn1, 'ffn1_w_gu': ffn1_w_gu, 'ffn1_w_down': ffn1_w_down,
            'norm_mix': norm_mix, 'w_in': w_in,
            'mla_q_norm': mla_q_norm, 'mla_w_uq': mla_w_uq,
            'mla_kv_norm': mla_kv_norm, 'mla_w_ukv': mla_w_ukv,
            'idx_k_norm': idx_k_norm, 'w_out': w_out,
            'norm_ffn2': norm_ffn2, 'ffn2_w_gu': ffn2_w_gu, 'ffn2_w_down': ffn2_w_down,
            'final_norm': final_norm}


def reference(x, c, positions, rel_bias, ada_w, ada_b, norm_ffn1, ffn1_w_gu, ffn1_w_down,
              norm_mix, w_in, mla_q_norm, mla_w_uq, mla_kv_norm, mla_w_ukv, idx_k_norm,
              w_out, norm_ffn2, ffn2_w_gu, ffn2_w_down, final_norm):
    h = x
    cond = jax.nn.silu(c)
    for l in range(DEPTH):
        mod = cond @ ada_w[l] + ada_b[l]
        (sh1, sc1, g1, sh2, sc2, g2, sh3, sc3, g3) = jnp.split(mod, N_MOD, axis=-1)
        n1 = modulate(rmsnorm(h, norm_ffn1[l]), sh1, sc1)
        h = h + MACARON_WEIGHT * g1[:, None, :] * swiglu(n1, ffn1_w_gu[l], ffn1_w_down[l])
        n2 = modulate(rmsnorm(h, norm_mix[l]), sh2, sc2)
        h = h + g2[:, None, :] * hybrid_mixer(n2, positions, rel_bias, w_in[l], mla_q_norm[l],
                                              mla_w_uq[l], mla_kv_norm[l], mla_w_ukv[l],
                                              idx_k_norm[l], w_out[l])
        n3 = modulate(rmsnorm(h, norm_ffn2[l]), sh3, sc3)
        h = h + MACARON_WEIGHT * g3[:, None, :] * swiglu(n3, ffn2_w_gu[l], ffn2_w_down[l])
    return rmsnorm(h, final_norm)
```

```python
import functools
import math

import numpy as np
import jax
import jax.numpy as jnp
from jax import lax
from jax.experimental import pallas as pl
from jax.experimental.pallas import tpu as pltpu

F32 = jnp.float32
BF16 = jnp.bfloat16

HEAD_DIM = 64
A_HEADS, B_HEADS, C_HEADS = 6, 5, 5
MLA_Q_RANK, MLA_KV_RANK, MLA_NOPE, MLA_ROPE = 256, 128, 64, 32
IDX_HEADS, IDX_DIM, IDX_ROPE = 8, 64, 32
TOPK = 256
N_BUCKETS, MAX_DISTANCE = 32, 128
ROPE_THETA = 10000.0
RMS_EPS = 1e-6
N_MOD = 9
NEG = -1e30
DILATIONS = (1, 4, 16)

LANES = 128
BLK = 128
QB = 256
VMEM_LIMIT = 56 * 1024 * 1024

T_QA, T_KA, T_VA = 0, 3, 6
T_QC, T_KC, T_VC = 9, 12, 15
T_CQ, T_IQ, T_IQS = 18, 20, 24
T_CKV, T_KR, T_KRS, T_IK, T_IKS, T_IW = 28, 29, 30, 31, 32, 33
N_TILES = 34

KEY_FLIP = 0x7FFFFFFF
INT_MIN = -(2 ** 31)


def _cparams(*sem):
    return pltpu.CompilerParams(dimension_semantics=sem, vmem_limit_bytes=VMEM_LIMIT)


def _rms(x, g):
    return x * lax.rsqrt(jnp.mean(x * x, axis=-1, keepdims=True) + RMS_EPS) * g


def _dot(a, b):
    return jnp.dot(a, b, preferred_element_type=F32)


def _dot_t(a, b):
    return lax.dot_general(a, b, (((1,), (1,)), ((), ())), preferred_element_type=F32)


def _np_bucket(dist):
    n = np.maximum(dist, 0)
    max_exact = N_BUCKETS // 2
    nf = np.maximum(n, 1).astype(np.float32)
    large = max_exact + (np.log(nf / np.float32(max_exact)) / np.float32(math.log(MAX_DISTANCE / max_exact))
                         * np.float32(N_BUCKETS - max_exact)).astype(np.int32)
    large = np.minimum(large, N_BUCKETS - 1)
    return np.where(n < max_exact, n, large).astype(np.int32)


def _bias_from_buckets(bucket, tab_ref, col):
    acc = jnp.full(bucket.shape, NEG, F32)
    for b in range(N_BUCKETS):
        acc = jnp.where(bucket == b, tab_ref[b, col], acc)
    return acc


def _mod_kernel(c_ref, w_ref, b_ref, o_ref):
    c = c_ref[...]
    cond = c / (1.0 + jnp.exp(-c))
    w = w_ref[0]
    c_hi = cond.astype(BF16)
    c_lo = (cond - c_hi.astype(F32)).astype(BF16)
    w_hi = w.astype(BF16)
    w_lo = (w - w_hi.astype(F32)).astype(BF16)
    o_ref[0] = _dot(c_hi, w_hi) + _dot(c_hi, w_lo) + _dot(c_lo, w_hi) + b_ref[0]


def _modulation(c, ada_w, ada_b):
    L, D, N = ada_w.shape
    B = c.shape[0]
    tn = 1024
    return pl.pallas_call(
        _mod_kernel,
        grid=(L, N // tn),
        in_specs=[pl.BlockSpec((B, D), lambda l, j: (0, 0)),
                  pl.BlockSpec((1, D, tn), lambda l, j: (l, 0, j)),
                  pl.BlockSpec((1, 1, tn), lambda l, j: (l, 0, j))],
        out_specs=pl.BlockSpec((1, B, tn), lambda l, j: (l, 0, j)),
        out_shape=jax.ShapeDtypeStruct((L, B, N), F32),
        compiler_params=_cparams("arbitrary", "arbitrary"),
        name="adaln_mod",
    )(c, ada_w, ada_b.reshape(L, 1, N))


def _ffn_kernel(h_ref, mod_ref, g_ref, wg_ref, wu_ref, wd_ref, fg_ref, o_ref, n_scr, acc_scr, *, ks, final):
    j = pl.program_id(1)

    @pl.when(j == 0)
    def _():
        y = _rms(h_ref[...], g_ref[...])
        n = y * (1.0 + mod_ref[0, ks + 1:ks + 2, :]) + mod_ref[0, ks:ks + 1, :]
        n_scr[...] = n.astype(BF16)
        acc_scr[...] = jnp.zeros_like(acc_scr)

    n = n_scr[...]
    g = _dot(n, wg_ref[...])
    u = _dot(n, wu_ref[...])
    a = (g / (1.0 + jnp.exp(-g))) * u
    acc_scr[...] += _dot(a.astype(BF16), wd_ref[...])

    @pl.when(j == pl.num_programs(1) - 1)
    def _():
        out = h_ref[...] + (0.5 * mod_ref[0, ks + 2:ks + 3, :]) * acc_scr[...]
        if final:
            out = _rms(out, fg_ref[...])
        o_ref[...] = out


def _ffn(h, mod, gain, w_gu, w_down, final_gain, *, ks, final, seq):
    M, D = h.shape
    F = w_down.shape[0]
    tm = 512
    tf = 512 if F % 512 == 0 else 256
    nf = F // tf
    per_b = seq // tm
    return pl.pallas_call(
        functools.partial(_ffn_kernel, ks=ks, final=final),
        grid=(M // tm, nf),
        in_specs=[pl.BlockSpec((tm, D), lambda i, j: (i, 0)),
                  pl.BlockSpec((1, N_MOD, D), lambda i, j: (i // per_b, 0, 0)),
                  pl.BlockSpec((1, D), lambda i, j: (0, 0)),
                  pl.BlockSpec((D, tf), lambda i, j: (0, j)),
                  pl.BlockSpec((D, tf), lambda i, j: (0, j + nf)),
                  pl.BlockSpec((tf, D), lambda i, j: (j, 0)),
                  pl.BlockSpec((1, D), lambda i, j: (0, 0))],
        out_specs=pl.BlockSpec((tm, D), lambda i, j: (i, 0)),
        out_shape=jax.ShapeDtypeStruct((M, D), F32),
        scratch_shapes=[pltpu.VMEM((tm, D), BF16), pltpu.VMEM((tm, D), F32)],
        compiler_params=_cparams("arbitrary", "arbitrary"),
        name="ffn",
    )(h, mod, gain, w_gu, w_gu, w_down, final_gain)


def _win_kernel(h_ref, mod_ref, g_ref, w_ref, o_ref, n_scr):
    @pl.when(pl.program_id(1) == 0)
    def _():
        y = _rms(h_ref[...], g_ref[...])
        n = y * (1.0 + mod_ref[0, 4:5, :]) + mod_ref[0, 3:4, :]
        n_scr[...] = n.astype(BF16)

    o_ref[...] = _dot(n_scr[...], w_ref[...])


def _in_proj(h, mod, gain, w, *, seq):
    M, D = h.shape
    N = w.shape[1]
    tm, tn = 512, 256
    per_b = seq // tm
    return pl.pallas_call(
        _win_kernel,
        grid=(M // tm, N // tn),
        in_specs=[pl.BlockSpec((tm, D), lambda i, j: (i, 0)),
                  pl.BlockSpec((1, N_MOD, D), lambda i, j: (i // per_b, 0, 0)),
                  pl.BlockSpec((1, D), lambda i, j: (0, 0)),
                  pl.BlockSpec((D, tn), lambda i, j: (0, j))],
        out_specs=pl.BlockSpec((tm, tn), lambda i, j: (i, j)),
        out_shape=jax.ShapeDtypeStruct((M, N), F32),
        scratch_shapes=[pltpu.VMEM((tm, D), BF16)],
        compiler_params=_cparams("arbitrary", "arbitrary"),
        name="in_proj",
    )(h, mod, gain, w)


def _rope_kernel(pos_ref, inv_ref, ci_ref, si_ref, cm_ref, sm_ref):
    pos = pos_ref[0].astype(F32)
    ang_i = pos * inv_ref[0:1, :]
    ang_m = pos * inv_ref[1:2, :]
    ci_ref[0] = jnp.cos(ang_i)
    si_ref[0] = jnp.sin(ang_i)
    cm_ref[0] = jnp.cos(ang_m)
    sm_ref[0] = jnp.sin(ang_m)


def _rope_tables(positions):
    B, T = positions.shape
    half = IDX_ROPE // 2
    inv = (ROPE_THETA ** (-np.arange(half, dtype=np.float32) / half)).astype(np.float32)
    zeros32 = np.zeros(IDX_DIM - IDX_ROPE, np.float32)
    row_idx = np.concatenate([inv, inv, zeros32, inv, inv, zeros32])
    row_mla = np.concatenate([np.zeros(MLA_NOPE, np.float32), inv, inv,
                              np.zeros(LANES - MLA_NOPE - MLA_ROPE, np.float32)])
    inv_rows = jnp.asarray(np.stack([row_idx, row_mla]))
    tab = jax.ShapeDtypeStruct((B, T, LANES), F32)
    spec = pl.BlockSpec((1, T, LANES), lambda b: (b, 0, 0))
    return pl.pallas_call(
        _rope_kernel,
        grid=(B,),
        in_specs=[pl.BlockSpec((1, T, 1), lambda b: (b, 0, 0)),
                  pl.BlockSpec((2, LANES), lambda b: (0, 0))],
        out_specs=[spec, spec, spec, spec],
        out_shape=[tab, tab, tab, tab],
        compiler_params=_cparams("arbitrary"),
        name="rope_tables",
    )(positions.reshape(B, T, 1), inv_rows)


def _dil_kernel(tab_ref, bk_ref, q_ref, k_ref, v_ref, o_ref, bt_scr, o_scr, m_scr, s_scr, *, seq):
    b = pl.program_id(0)
    pair = pl.program_id(1)

    @pl.when(b == 0)
    def _():
        for p in range(3):
            for hh in range(2):
                bt_scr[pair, p, hh] = _bias_from_buckets(bk_ref[p], tab_ref, 2 * pair + hh)

    lane = lax.broadcasted_iota(jnp.int32, (BLK, LANES), 1)
    low = lane < HEAD_DIM

    def block(p, d, q_start, k_start, nk):
        q = q_ref[0, pl.ds(q_start, BLK, stride=d), :]
        k = k_ref[0, pl.ds(k_start, nk, stride=d), :].astype(BF16)
        v = v_ref[0, pl.ds(k_start, nk, stride=d), :].astype(BF16)
        res = []
        for hh in range(2):
            qh = jnp.where(low, q, 0.0) if hh == 0 else jnp.where(low, 0.0, q)
            logit = _dot_t(qh.astype(BF16), k) * (HEAD_DIM ** -0.5) + bt_scr[pair, p, hh][:, 2 * BLK - nk:]
            m = jnp.max(logit, axis=-1, keepdims=True)
            e = jnp.exp(logit - m)
            s = jnp.sum(e, axis=-1, keepdims=True)
            res.append((m, s, _dot(e.astype(BF16), v)))
        rows = pl.ds(q_start, BLK, stride=d)
        o_scr[p, rows, :] = jnp.where(low, res[0][2], res[1][2])
        m_scr[p, rows, :] = jnp.where(low, res[0][0], res[1][0])
        s_scr[p, rows, :] = jnp.where(low, res[0][1], res[1][1])

    for p, d in enumerate(DILATIONS):
        nb = seq // (d * BLK)
        if nb > 1:
            for r in range(d):
                block(p, d, r, r, BLK)

                def body(n, carry, p=p, d=d, r=r):
                    block(p, d, r + d * BLK * n, r + d * BLK * (n - 1), 2 * BLK)
                    return carry
                lax.fori_loop(1, nb, body, 0)
        else:
            def body1(r, carry, p=p, d=d):
                block(p, d, r, r, BLK)
                return carry
            lax.fori_loop(0, d, body1, 0)

    m_all = jnp.maximum(jnp.maximum(m_scr[0], m_scr[1]), m_scr[2])
    num = jnp.zeros((seq, LANES), F32)
    den = jnp.zeros((seq, LANES), F32)
    for p in range(3):
        w = jnp.exp(m_scr[p] - m_all)
        num = num + w * o_scr[p]
        den = den + w * s_scr[p]
    o_ref[0] = (num / den).astype(BF16)


def _dilated_attention(proj, rel_bias, *, batch, seq):
    qi = np.arange(BLK)[:, None]
    kj = np.arange(2 * BLK)[None, :]
    step = BLK + qi - kj
    in_band = (step >= 0) & (step <= BLK)
    bk = np.stack([np.where(in_band, _np_bucket(step * d), -1) for d in DILATIONS]).astype(np.int32)
    pairs = A_HEADS // 2

    def col(t0):
        return pl.BlockSpec((1, seq, LANES), lambda b, p: (b, 0, t0 + p))

    return pl.pallas_call(
        functools.partial(_dil_kernel, seq=seq),
        grid=(batch, pairs),
        in_specs=[pl.BlockSpec(memory_space=pltpu.SMEM),
                  pl.BlockSpec((3, BLK, 2 * BLK), lambda b, p: (0, 0, 0)),
                  col(T_QA), col(T_KA), col(T_VA)],
        out_specs=pl.BlockSpec((1, seq, LANES), lambda b, p: (b, 0, p)),
        out_shape=jax.ShapeDtypeStruct((batch, seq, pairs * LANES), BF16),
        scratch_shapes=[pltpu.VMEM((pairs, 3, 2, BLK, 2 * BLK), F32),
                        pltpu.VMEM((3, seq, LANES), F32),
                        pltpu.VMEM((3, seq, LANES), F32),
                        pltpu.VMEM((3, seq, LANES), F32)],
        compiler_params=_cparams("arbitrary", "arbitrary"),
        name="dilated_attn",
    )(rel_bias, jnp.asarray(bk), proj, proj, proj)


def _mla_kernel(cq_ref, ckv_ref, kr_ref, krs_ref, c_ref, s_ref, qn_ref, kvn_ref,
                wq_ref, wqs_ref, wk_ref, wv_ref, o_ref, q_scr, k_scr, v_scr, *, seq):
    rc = 512
    for c0 in range(0, seq, rc):
        rows = slice(c0, c0 + rc)
        cos = c_ref[0, rows, :]
        sin = s_ref[0, rows, :]
        nq = _rms(cq_ref[0, rows, :], qn_ref[...]).astype(BF16)
        nkv = _rms(ckv_ref[0, rows, :], kvn_ref[...]).astype(BF16)
        kr = kr_ref[0, rows, :] * cos + krs_ref[0, rows, :] * sin
        for h in range(B_HEADS):
            cols = slice(h * LANES, (h + 1) * LANES)
            q_scr[h, rows, :] = (_dot(nq, wq_ref[:, cols]) * cos + _dot(nq, wqs_ref[:, cols]) * sin).astype(BF16)
            k_scr[h, rows, :] = (_dot(nkv, wk_ref[:, cols]) + kr).astype(BF16)
            v_scr[h, rows, :] = _dot(nkv, wv_ref[:, cols]).astype(BF16)

    scale = (MLA_NOPE + MLA_ROPE) ** -0.5
    row = lax.broadcasted_iota(jnp.int32, (QB, QB), 0)
    col = lax.broadcasted_iota(jnp.int32, (QB, QB), 1)
    for h in range(B_HEADS):
        def q_block(i, carry, h=h):
            qs = pl.multiple_of(i * QB, QB)
            q = q_scr[h, pl.ds(qs, QB), :]

            def kv_step(kb, st):
                m, s, acc = st
                ks = pl.multiple_of(kb * QB, QB)
                logit = _dot_t(q, k_scr[h, pl.ds(ks, QB), :]) * scale
                logit = jnp.where(ks + col <= qs + row, logit, NEG)
                m_new = jnp.maximum(m, jnp.max(logit, axis=-1, keepdims=True))
                alpha = jnp.exp(m - m_new)
                e = jnp.exp(logit - m_new)
                s = alpha * s + jnp.sum(e, axis=-1, keepdims=True)
                acc = alpha * acc + _dot(e.astype(BF16), v_scr[h, pl.ds(ks, QB), :])
                return m_new, s, acc

            init = (jnp.full((QB, 1), NEG, F32), jnp.zeros((QB, 1), F32), jnp.zeros((QB, LANES), F32))
            m, s, acc = lax.fori_loop(0, i + 1, kv_step, init)
            o_ref[0, pl.ds(qs, QB), h * LANES:(h + 1) * LANES] = (acc / s).astype(BF16)
            return carry

        lax.fori_loop(0, seq // QB, q_block, 0)


def _mla_attention(proj, cos_m, sin_m, q_norm, kv_norm, wq, wqs, wk, wv, *, batch, seq):
    def col(t0, n):
        return pl.BlockSpec((1, seq, n * LANES), lambda b: (b, 0, t0 // n))

    def full(a):
        return pl.BlockSpec(a.shape, lambda b: (0,) * a.ndim)

    tab = pl.BlockSpec((1, seq, LANES), lambda b: (b, 0, 0))
    width = B_HEADS * LANES
    return pl.pallas_call(
        functools.partial(_mla_kernel, seq=seq),
        grid=(batch,),
        in_specs=[col(T_CQ, 2), col(T_CKV, 1), col(T_KR, 1), col(T_KRS, 1), tab, tab,
                  full(q_norm), full(kv_norm), full(wq), full(wqs), full(wk), full(wv)],
        out_specs=pl.BlockSpec((1, seq, width), lambda b: (b, 0, 0)),
        out_shape=jax.ShapeDtypeStruct((batch, seq, width), BF16),
        scratch_shapes=[pltpu.VMEM((B_HEADS, seq, LANES), BF16)] * 3,
        compiler_params=_cparams("arbitrary"),
        name="mla_attn",
    )(proj, proj, proj, proj, cos_m, sin_m, q_norm, kv_norm, wq, wqs, wk, wv)


def _sel_kernel(iq_ref, iqs_ref, iw_ref, cq_ref, sq_ref, ik_ref, iks_ref, ck_ref, sk_ref, g_ref,
                o_ref, ik_scr, iqm_scr, w_scr, key_scr, *, seq):
    i = pl.program_id(1)
    qs = i * QB
    nk = seq // QB

    @pl.when(i == 0)
    def _():
        x = ik_ref[0]
        r = lax.rsqrt(jnp.mean(x * x, axis=-1, keepdims=True) + RMS_EPS)
        y = x * r * g_ref[0:1, :]
        ys = iks_ref[0] * r * g_ref[1:2, :]
        ik_scr[...] = (y * ck_ref[0] + ys * sk_ref[0]).astype(BF16)

    lane = lax.broadcasted_iota(jnp.int32, (QB, LANES), 1)
    low = lane < IDX_DIM
    cos = cq_ref[0]
    sin = sq_ref[0]
    for t in range(IDX_HEADS // 2):
        cols = slice(t * LANES, (t + 1) * LANES)
        rot = iq_ref[0, :, cols] * cos + iqs_ref[0, :, cols] * sin
        iqm_scr[2 * t] = jnp.where(low, rot, 0.0).astype(BF16)
        iqm_scr[2 * t + 1] = jnp.where(low, 0.0, rot).astype(BF16)
    iw = iw_ref[0] * (IDX_HEADS ** -0.5 * IDX_DIM ** -0.5)
    for h in range(IDX_HEADS):
        w_scr[h] = jnp.broadcast_to(iw[:, h:h + 1], (QB, QB))

    row = lax.broadcasted_iota(jnp.int32, (QB, QB), 0)
    col = lax.broadcasted_iota(jnp.int32, (QB, QB), 1)
    key_neg_inf = int(np.array(-np.inf, np.float32).view(np.int32)) ^ KEY_FLIP
    key_scr[...] = jnp.full(key_scr.shape, key_neg_inf, jnp.int32)

    def score_step(kb, carry):
        ks = pl.multiple_of(kb * QB, QB)
        rel = _dot_t(iqm_scr[...].reshape(IDX_HEADS * QB, LANES), ik_scr[pl.ds(ks, QB), :])
        sc = jnp.zeros((QB, QB), F32)
        for h in range(IDX_HEADS):
            sc = sc + jnp.maximum(rel[h * QB:(h + 1) * QB], 0.0) * w_scr[h]
        sc = jnp.where(sc == 0.0, 0.0, sc)
        sc = jnp.where(ks + col <= qs + row, sc, -jnp.inf)
        bits = lax.bitcast_convert_type(sc, jnp.int32)
        key_scr[kb] = jnp.where(bits < 0, bits ^ KEY_FLIP, bits)
        return carry

    lax.fori_loop(0, i + 1, score_step, 0)

    target = jnp.minimum(qs + lax.broadcasted_iota(jnp.int32, (QB, 1), 0) + 1, TOPK).astype(F32)

    def count(mask):
        return jnp.sum(jnp.sum(mask.astype(F32), axis=0), axis=-1, keepdims=True)

    def bit_step(bi, prefix):
        cand_u = prefix | lax.shift_left(jnp.int32(1), 31 - bi)
        cand = cand_u ^ INT_MIN
        cnt = count(key_scr[...] >= cand[None])
        return jnp.where(cnt >= target, cand_u, prefix)

    prefix = lax.fori_loop(0, 32, bit_step, jnp.zeros((QB, 1), jnp.int32))
    thr = prefix ^ INT_MIN
    need = target - count(key_scr[...] > thr[None])

    upper = (row <= col).astype(BF16)
    carry = jnp.zeros((QB, 1), F32)
    for c in range(nk):
        kc = key_scr[c]
        eq = kc == thr
        rank = carry + _dot(eq.astype(BF16), upper)
        sel = (kc > thr) | (eq & (rank <= need))
        o_ref[0, 0, c] = jnp.where(sel, 0.0, NEG).astype(BF16)
        carry = rank[:, QB - 1:QB]


def _dsa_select(proj, cos_i, sin_i, ik_gains, *, batch, seq):
    nq = seq // QB

    def qcol(t0, n):
        return pl.BlockSpec((1, QB, n * LANES), lambda b, i: (b, i, t0 // n))

    def kcol(t0):
        return pl.BlockSpec((1, seq, LANES), lambda b, i: (b, 0, t0))

    qtab = pl.BlockSpec((1, QB, LANES), lambda b, i: (b, i, 0))
    ktab = pl.BlockSpec((1, seq, LANES), lambda b, i: (b, 0, 0))
    return pl.pallas_call(
        functools.partial(_sel_kernel, seq=seq),
        grid=(batch, nq),
        in_specs=[qcol(T_IQ, 4), qcol(T_IQS, 4), qcol(T_IW, 1), qtab, qtab,
                  kcol(T_IK), kcol(T_IKS), ktab, ktab,
                  pl.BlockSpec((2, LANES), lambda b, i: (0, 0))],
        out_specs=pl.BlockSpec((1, 1, nq, QB, QB), lambda b, i: (b, i, 0, 0, 0)),
        out_shape=jax.ShapeDtypeStruct((batch, nq, nq, QB, QB), BF16),
        scratch_shapes=[pltpu.VMEM((seq, LANES), BF16),
                        pltpu.VMEM((IDX_HEADS, QB, LANES), BF16),
                        pltpu.VMEM((IDX_HEADS, QB, QB), F32),
                        pltpu.VMEM((nq, QB, QB), jnp.int32)],
        compiler_params=_cparams("arbitrary", "arbitrary"),
        name="dsa_select",
    )(proj, proj, proj, cos_i, sin_i, proj, proj, cos_i, sin_i, ik_gains)


def _dsa_kernel(tab_ref, bk_ref, q_ref, k_ref, v_ref, sel_ref, o_ref, kb_scr, vb_scr, bt_scr):
    b = pl.program_id(0)
    i = pl.program_id(1)

    @pl.when((b == 0) & (i == 0))
    def _():
        for h in range(C_HEADS):
            for dd in range(3):
                bt_scr[3 * h + dd] = _bias_from_buckets(bk_ref[dd], tab_ref, A_HEADS + h)

    @pl.when(i == 0)
    def _():
        kb_scr[...] = k_ref[0].astype(BF16)
        vb_scr[...] = v_ref[0].astype(BF16)

    lane = lax.broadcasted_iota(jnp.int32, (QB, LANES), 1)
    low = lane < HEAD_DIM
    for pt in range((C_HEADS + 1) // 2):
        cols = slice(pt * LANES, (pt + 1) * LANES)
        heads = min(2, C_HEADS - 2 * pt)
        q = q_ref[0, :, cols]
        halves = [jnp.where(low, q, 0.0), jnp.where(low, 0.0, q)][:heads]
        q2 = jnp.concatenate(halves, axis=0).astype(BF16) if heads == 2 else halves[0].astype(BF16)

        def kv_step(kb, st, pt=pt, cols=cols, heads=heads, q2=q2):
            ks = pl.multiple_of(kb * QB, QB)
            logit2 = _dot_t(q2, kb_scr[pl.ds(ks, QB), cols]) * (HEAD_DIM ** -0.5)
            selb = sel_ref[0, 0, kb].astype(F32)
            on = selb == 0.0
            dd = jnp.minimum(i - kb, 2)
            v = vb_scr[pl.ds(ks, QB), cols]
            new = []
            for hh in range(heads):
                m, s, acc = st[hh]
                logit = logit2[hh * QB:(hh + 1) * QB] + bt_scr[3 * (2 * pt + hh) + dd] + selb
                m_new = jnp.maximum(m, jnp.max(logit, axis=-1, keepdims=True))
                alpha = jnp.exp(m - m_new)
                e = jnp.where(on, jnp.exp(logit - m_new), 0.0)
                s = alpha * s + jnp.sum(e, axis=-1, keepdims=True)
                acc = alpha * acc + _dot(e.astype(BF16), v)
                new.append((m_new, s, acc))
            return tuple(new)

        init = tuple((jnp.full((QB, 1), NEG, F32), jnp.zeros((QB, 1), F32), jnp.zeros((QB, LANES), F32))
                     for _ in range(heads))
        st = lax.fori_loop(0, i + 1, kv_step, init)
        outs = [acc / s for (_, s, acc) in st]
        o = jnp.where(low, outs[0], outs[1] if heads == 2 else 0.0)
        o_ref[0, :, cols] = o.astype(BF16)


def _dsa_attention(proj, sel, rel_bias, *, batch, seq):
    nq = seq // QB
    qi = np.arange(QB)[:, None]
    kj = np.arange(QB)[None, :]
    bk = np.stack([_np_bucket(qi - kj), _np_bucket(QB + qi - kj),
                   _np_bucket(np.full((QB, QB), 2 * QB))]).astype(np.int32)
    width = ((C_HEADS + 1) // 2) * LANES
    return pl.pallas_call(
        _dsa_kernel,
        grid=(batch, nq),
        in_specs=[pl.BlockSpec(memory_space=pltpu.SMEM),
                  pl.BlockSpec((3, QB, QB), lambda b, i: (0, 0, 0)),
                  pl.BlockSpec((1, QB, width), lambda b, i: (b, i, T_QC * LANES // width)),
                  pl.BlockSpec((1, seq, width), lambda b, i: (b, 0, T_KC * LANES // width)),
                  pl.BlockSpec((1, seq, width), lambda b, i: (b, 0, T_VC * LANES // width)),
                  pl.BlockSpec((1, 1, nq, QB, QB), lambda b, i: (b, i, 0, 0, 0))],
        out_specs=pl.BlockSpec((1, QB, width), lambda b, i: (b, i, 0)),
        out_shape=jax.ShapeDtypeStruct((batch, seq, width), BF16),
        scratch_shapes=[pltpu.VMEM((seq, width), BF16),
                        pltpu.VMEM((seq, width), BF16),
                        pltpu.VMEM((3 * C_HEADS, QB, QB), F32)],
        compiler_params=_cparams("arbitrary", "arbitrary"),
        name="dsa_attn",
    )(rel_bias, jnp.asarray(bk), proj, proj, proj, sel)


def _out_kernel(oa_ref, oc_ref, ob_ref, wa_ref, wc_ref, wb_ref, h_ref, mod_ref, o_ref):
    acc = _dot(oa_ref[...], wa_ref[...]) + _dot(oc_ref[...], wc_ref[...]) + _dot(ob_ref[...], wb_ref[...])
    o_ref[...] = h_ref[...] + mod_ref[0, 5:6, :] * acc


def _out_proj(oa, oc, ob, wa, wc, wb, h, mod, *, seq):
    M, D = h.shape
    tm = 512
    per_b = seq // tm

    def rows(a):
        return pl.BlockSpec((tm, a.shape[1]), lambda i: (i, 0))

    def full(a):
        return pl.BlockSpec(a.shape, lambda i: (0, 0))

    return pl.pallas_call(
        _out_kernel,
        grid=(M // tm,),
        in_specs=[rows(oa), rows(oc), rows(ob), full(wa), full(wc), full(wb), rows(h),
                  pl.BlockSpec((1, N_MOD, D), lambda i: (i // per_b, 0, 0))],
        out_specs=pl.BlockSpec((tm, D), lambda i: (i, 0)),
        out_shape=jax.ShapeDtypeStruct((M, D), F32),
        compiler_params=_cparams("arbitrary"),
        name="out_proj",
    )(oa, oc, ob, wa, wc, wb, h, mod)


def _swap_halves(w, rope):
    half = rope // 2
    return jnp.concatenate([-w[..., half:rope], w[..., :half], jnp.zeros_like(w[..., rope:])], axis=-1)


def _pad_cols(w, n):
    return jnp.concatenate([w, jnp.zeros(w.shape[:-1] + (n - w.shape[-1],), w.dtype)], axis=-1)


def _layout_w_in(w):
    D = w.shape[0]
    sizes = (A_HEADS * HEAD_DIM,) * 3 + (MLA_Q_RANK, MLA_KV_RANK, MLA_ROPE) + (C_HEADS * HEAD_DIM,) * 3 + (
        IDX_HEADS * IDX_DIM, IDX_DIM, IDX_HEADS)
    offs = np.cumsum((0,) + sizes)
    qa, ka, va, cq, ckv, kr, qc, kc, vc, iq, ik, iw = [w[:, offs[n]:offs[n + 1]] for n in range(len(sizes))]
    c_w = ((C_HEADS + 1) // 2) * LANES
    z64 = jnp.zeros((D, MLA_NOPE), w.dtype)
    iq_h = iq.reshape(D, IDX_HEADS, IDX_DIM)
    iq_sw = _swap_halves(iq_h, IDX_ROPE).reshape(D, IDX_HEADS * IDX_DIM)
    ik_sw = _swap_halves(ik, IDX_ROPE)
    cols = [qa, ka, va, _pad_cols(qc, c_w), _pad_cols(kc, c_w), _pad_cols(vc, c_w), cq, iq, iq_sw, ckv,
            _pad_cols(jnp.concatenate([z64, kr], axis=-1), LANES),
            _pad_cols(jnp.concatenate([z64, _swap_halves(kr, MLA_ROPE)], axis=-1), LANES),
            jnp.concatenate([ik, ik], axis=-1), jnp.concatenate([ik_sw, ik_sw], axis=-1), _pad_cols(iw, LANES)]
    out = jnp.concatenate(cols, axis=-1)
    assert out.shape[1] == N_TILES * LANES
    return out.astype(BF16)


def _layout_mla(w_uq, w_ukv):
    rq = w_uq.reshape(MLA_Q_RANK, B_HEADS, MLA_NOPE + MLA_ROPE)
    nope, rope = rq[..., :MLA_NOPE], rq[..., MLA_NOPE:]
    wq = _pad_cols(jnp.concatenate([nope, rope], axis=-1), LANES)
    wqs = _pad_cols(jnp.concatenate([jnp.zeros_like(nope), _swap_halves(rope, MLA_ROPE)], axis=-1), LANES)
    rkv = w_ukv.reshape(MLA_KV_RANK, B_HEADS, MLA_NOPE + HEAD_DIM)
    wk = _pad_cols(rkv[..., :MLA_NOPE], LANES)
    wv = _pad_cols(rkv[..., MLA_NOPE:], LANES)
    flat = lambda a, r: a.reshape(r, B_HEADS * LANES).astype(BF16)
    return flat(wq, MLA_Q_RANK), flat(wqs, MLA_Q_RANK), flat(wk, MLA_KV_RANK), flat(wv, MLA_KV_RANK)


def _layout_w_out(w):
    D = w.shape[1]
    a, b_, c_ = A_HEADS * HEAD_DIM, B_HEADS * HEAD_DIM, C_HEADS * HEAD_DIM
    wa = w[:a]
    wb = w[a:a + b_].reshape(B_HEADS, HEAD_DIM, D)
    wb = jnp.concatenate([wb, jnp.zeros_like(wb)], axis=1).reshape(B_HEADS * LANES, D)
    wc = w[a + b_:a + b_ + c_]
    wc = jnp.concatenate([wc, jnp.zeros((((C_HEADS + 1) // 2) * LANES - c_, D), w.dtype)], axis=0)
    return wa.astype(BF16), wc.astype(BF16), wb.astype(BF16)


def kernel(x, c, positions, rel_bias, ada_w, ada_b, norm_ffn1, ffn1_w_gu, ffn1_w_down, norm_mix, w_in, mla_q_norm, mla_w_uq, mla_kv_norm, mla_w_ukv, idx_k_norm, w_out, norm_ffn2, ffn2_w_gu, ffn2_w_down, final_norm):
    B, T, D = x.shape
    L = ada_w.shape[0]
    assert T % (max(DILATIONS) * BLK) == 0 and T % 512 == 0 and D % LANES == 0

    mods = _modulation(c, ada_w, ada_b).reshape(L, B, N_MOD, D)
    cos_i, sin_i, cos_m, sin_m = _rope_tables(positions)
    final_gain = final_norm.reshape(1, D)
    h = x.reshape(B * T, D)
    for l in range(L):
        mod = mods[l]
        h = _ffn(h, mod, norm_ffn1[l].reshape(1, D), ffn1_w_gu[l].astype(BF16), ffn1_w_down[l].astype(BF16),
                 final_gain, ks=0, final=False, seq=T)

        proj = _in_proj(h, mod, norm_mix[l].reshape(1, D), _layout_w_in(w_in[l]), seq=T)
        proj = proj.reshape(B, T, N_TILES * LANES)
        oa = _dilated_attention(proj, rel_bias, batch=B, seq=T)
        wq, wqs, wk, wv = _layout_mla(mla_w_uq[l], mla_w_ukv[l])
        ob = _mla_attention(proj, cos_m, sin_m, mla_q_norm[l].reshape(1, -1), mla_kv_norm[l].reshape(1, -1),
                            wq, wqs, wk, wv, batch=B, seq=T)
        g = idx_k_norm[l]
        g_sw = jnp.concatenate([g[IDX_ROPE // 2:IDX_ROPE], g[:IDX_ROPE // 2], jnp.zeros_like(g[IDX_ROPE:])])
        ik_gains = jnp.stack([jnp.concatenate([g, g]), jnp.concatenate([g_sw, g_sw])])
        sel = _dsa_select(proj, cos_i, sin_i, ik_gains, batch=B, seq=T)
        oc = _dsa_attention(proj, sel, rel_bias, batch=B, seq=T)
        wa, wc, wb = _layout_w_out(w_out[l])
        h = _out_proj(oa.reshape(B * T, -1), oc.reshape(B * T, -1), ob.reshape(B * T, -1), wa, wc, wb, h, mod, seq=T)

        h = _ffn(h, mod, norm_ffn2[l].reshape(1, D), ffn2_w_gu[l].astype(BF16), ffn2_w_down[l].astype(BF16),
                 final_gain, ks=6, final=(l == L - 1), seq=T)
    return h.reshape(B, T, D)
```

```python
import functools
import math

import numpy as np
import jax
import jax.numpy as jnp
from jax import lax
from jax.experimental import pallas as pl
from jax.experimental.pallas import tpu as pltpu

F32 = jnp.float32
BF16 = jnp.bfloat16

HEAD_DIM = 64
A_HEADS, B_HEADS, C_HEADS = 6, 5, 5
MLA_Q_RANK, MLA_KV_RANK, MLA_NOPE, MLA_ROPE = 256, 128, 64, 32
IDX_HEADS, IDX_DIM, IDX_ROPE = 8, 64, 32
TOPK = 256
N_BUCKETS, MAX_DISTANCE = 32, 128
ROPE_THETA = 10000.0
RMS_EPS = 1e-6
N_MOD = 9
NEG = -1e30
DILATIONS = (1, 4, 16)

LANES = 128
BLK = 128
QB = 256
VMEM_LIMIT = 56 * 1024 * 1024

T_QA, T_KA, T_VA = 0, 3, 6
T_QC, T_KC, T_VC = 9, 12, 15
T_CQ, T_IQ, T_IQS = 18, 20, 24
T_CKV, T_KR, T_KRS, T_IK, T_IKS, T_IW = 28, 29, 30, 31, 32, 33
N_TILES = 34

KEY_FLIP = 0x7FFFFFFF
INT_MIN = -(2 ** 31)


def _cparams(*sem):
    return pltpu.CompilerParams(dimension_semantics=sem, vmem_limit_bytes=VMEM_LIMIT)


def _rms(x, g):
    return x * lax.rsqrt(jnp.mean(x * x, axis=-1, keepdims=True) + RMS_EPS) * g


def _dot(a, b):
    return jnp.dot(a, b, preferred_element_type=F32)


def _dot_t(a, b):
    return lax.dot_general(a, b, (((1,), (1,)), ((), ())), preferred_element_type=F32)


def _np_bucket(dist):
    n = np.maximum(dist, 0)
    max_exact = N_BUCKETS // 2
    nf = np.maximum(n, 1).astype(np.float32)
    large = max_exact + (np.log(nf / np.float32(max_exact)) / np.float32(math.log(MAX_DISTANCE / max_exact))
                         * np.float32(N_BUCKETS - max_exact)).astype(np.int32)
    large = np.minimum(large, N_BUCKETS - 1)
    return np.where(n < max_exact, n, large).astype(np.int32)


def _bias_from_buckets(bucket, tab_ref, col):
    acc = jnp.full(bucket.shape, NEG, F32)
    for b in range(N_BUCKETS):
        acc = jnp.where(bucket == b, tab_ref[b, col], acc)
    return acc


def _mod_kernel(c_ref, w_ref, b_ref, o_ref):
    c = c_ref[...]
    cond = c / (1.0 + jnp.exp(-c))
    w = w_ref[0]
    c_hi = cond.astype(BF16)
    c_lo = (cond - c_hi.astype(F32)).astype(BF16)
    w_hi = w.astype(BF16)
    w_lo = (w - w_hi.astype(F32)).astype(BF16)
    o_ref[0] = _dot(c_hi, w_hi) + _dot(c_hi, w_lo) + _dot(c_lo, w_hi) + b_ref[0]


def _modulation(c, ada_w, ada_b):
    L, D, N = ada_w.shape
    B = c.shape[0]
    tn = 1024
    return pl.pallas_call(
        _mod_kernel,
        grid=(L, N // tn),
        in_specs=[pl.BlockSpec((B, D), lambda l, j: (0, 0)),
                  pl.BlockSpec((1, D, tn), lambda l, j: (l, 0, j)),
                  pl.BlockSpec((1, 1, tn), lambda l, j: (l, 0, j))],
        out_specs=pl.BlockSpec((1, B, tn), lambda l, j: (l, 0, j)),
        out_shape=jax.ShapeDtypeStruct((L, B, N), F32),
        compiler_params=_cparams("arbitrary", "arbitrary"),
        name="adaln_mod",
    )(c, ada_w, ada_b.reshape(L, 1, N))


def _ffn_kernel(h_ref, mod_ref, g_ref, wgu_ref, wd_ref, fg_ref, o_ref, *, ks, final, tf):
    F = wd_ref.shape[0]
    x = h_ref[...]
    y = _rms(x, g_ref[...])
    n = (y * (1.0 + mod_ref[0, ks + 1:ks + 2, :]) + mod_ref[0, ks:ks + 1, :]).astype(BF16)
    acc = jnp.zeros(x.shape, F32)
    for f0 in range(0, F, tf):
        g = _dot(n, wgu_ref[:, f0:f0 + tf])
        u = _dot(n, wgu_ref[:, F + f0:F + f0 + tf])
        a = (g / (1.0 + jnp.exp(-g))) * u
        acc = acc + _dot(a.astype(BF16), wd_ref[f0:f0 + tf, :])
    out = x + (0.5 * mod_ref[0, ks + 2:ks + 3, :]) * acc
    if final:
        out = _rms(out, fg_ref[...])
    o_ref[...] = out


def _resident(a):
    return pl.BlockSpec(a.shape, lambda *_: (0,) * a.ndim, pipeline_mode=pl.Buffered(1))


def _ffn(h, mod, gain, w_gu, w_down, final_gain, *, ks, final, seq):
    M, D = h.shape
    F = w_down.shape[0]
    tm = 512
    tf = 256
    assert F % tf == 0
    per_b = seq // tm
    return pl.pallas_call(
        functools.partial(_ffn_kernel, ks=ks, final=final, tf=tf),
        grid=(M // tm,),
        in_specs=[pl.BlockSpec((tm, D), lambda i: (i, 0)),
                  pl.BlockSpec((1, N_MOD, D), lambda i: (i // per_b, 0, 0)),
                  pl.BlockSpec((1, D), lambda i: (0, 0)),
                  _resident(w_gu), _resident(w_down),
                  pl.BlockSpec((1, D), lambda i: (0, 0))],
        out_specs=pl.BlockSpec((tm, D), lambda i: (i, 0)),
        out_shape=jax.ShapeDtypeStruct((M, D), F32),
        compiler_params=_cparams("arbitrary"),
        name="ffn",
    )(h, mod, gain, w_gu, w_down, final_gain)


def _win_kernel(h_ref, mod_ref, g_ref, w_ref, o_ref, *, tn):
    y = _rms(h_ref[...], g_ref[...])
    n = (y * (1.0 + mod_ref[0, 4:5, :]) + mod_ref[0, 3:4, :]).astype(BF16)
    for c0 in range(0, w_ref.shape[1], tn):
        o_ref[:, c0:c0 + tn] = _dot(n, w_ref[:, c0:c0 + tn])


def _in_proj(h, mod, gain, w, *, seq):
    M, D = h.shape
    N = w.shape[1]
    tm, tn = 512, 256
    assert N % tn == 0
    per_b = seq // tm
    return pl.pallas_call(
        functools.partial(_win_kernel, tn=tn),
        grid=(M // tm,),
        in_specs=[pl.BlockSpec((tm, D), lambda i: (i, 0)),
                  pl.BlockSpec((1, N_MOD, D), lambda i: (i // per_b, 0, 0)),
                  pl.BlockSpec((1, D), lambda i: (0, 0)),
                  _resident(w)],
        out_specs=pl.BlockSpec((tm, N), lambda i: (i, 0)),
        out_shape=jax.ShapeDtypeStruct((M, N), F32),
        compiler_params=_cparams("arbitrary"),
        name="in_proj",
    )(h, mod, gain, w)


def _rope_kernel(pos_ref, inv_ref, ci_ref, si_ref, cm_ref, sm_ref):
    pos = pos_ref[0].astype(F32)
    ang_i = pos * inv_ref[0:1, :]
    ang_m = pos * inv_ref[1:2, :]
    ci_ref[0] = jnp.cos(ang_i)
    si_ref[0] = jnp.sin(ang_i)
    cm_ref[0] = jnp.cos(ang_m)
    sm_ref[0] = jnp.sin(ang_m)


def _rope_tables(positions):
    B, T = positions.shape
    half = IDX_ROPE // 2
    inv = (ROPE_THETA ** (-np.arange(half, dtype=np.float32) / half)).astype(np.float32)
    zeros32 = np.zeros(IDX_DIM - IDX_ROPE, np.float32)
    row_idx = np.concatenate([inv, inv, zeros32, inv, inv, zeros32])
    row_mla = np.concatenate([np.zeros(MLA_NOPE, np.float32), inv, inv,
                              np.zeros(LANES - MLA_NOPE - MLA_ROPE, np.float32)])
    inv_rows = jnp.asarray(np.stack([row_idx, row_mla]))
    tab = jax.ShapeDtypeStruct((B, T, LANES), F32)
    spec = pl.BlockSpec((1, T, LANES), lambda b: (b, 0, 0))
    return pl.pallas_call(
        _rope_kernel,
        grid=(B,),
        in_specs=[pl.BlockSpec((1, T, 1), lambda b: (b, 0, 0)),
                  pl.BlockSpec((2, LANES), lambda b: (0, 0))],
        out_specs=[spec, spec, spec, spec],
        out_shape=[tab, tab, tab, tab],
        compiler_params=_cparams("arbitrary"),
        name="rope_tables",
    )(positions.reshape(B, T, 1), inv_rows)


def _dil_kernel(tab_ref, bk_ref, q_ref, k_ref, v_ref, o_ref, bt_scr, o_scr, m_scr, s_scr, *, seq):
    b = pl.program_id(0)
    pair = pl.program_id(1)

    @pl.when(b == 0)
    def _():
        for p in range(3):
            for hh in range(2):
                bt_scr[pair, p, hh] = _bias_from_buckets(bk_ref[p], tab_ref, 2 * pair + hh)

    nblk = seq // BLK
    low = lax.broadcasted_iota(jnp.int32, (nblk, BLK, LANES), 2) < HEAD_DIM
    neg_half = jnp.full((BLK, BLK), NEG, F32)

    for p, d in enumerate(DILATIONS):
        nb = seq // (d * BLK)
        starts = [r + d * BLK * n for n in range(nb) for r in range(d)]

        def rows_of(s0, d=d):
            return pl.ds(s0, BLK) if d == 1 else pl.ds(s0, BLK, stride=d)

        def regroup(ref):
            return jnp.stack([ref[0, rows_of(s0), :] for s0 in starts])

        def band(z):
            if nb == 1:
                return z
            prev = jnp.concatenate([jnp.zeros((d,) + z.shape[1:], z.dtype), z[:-d]], axis=0)
            return jnp.concatenate([prev, z], axis=1)

        q = regroup(q_ref) * (HEAD_DIM ** -0.5)
        kband = band(regroup(k_ref).astype(BF16))
        vband = band(regroup(v_ref).astype(BF16))
        res = []
        for hh in range(2):
            qh = (jnp.where(low, q, 0.0) if hh == 0 else jnp.where(low, 0.0, q)).astype(BF16)
            logit = lax.dot_general(qh, kband, (((2,), (2,)), ((0,), (0,))), preferred_element_type=F32)
            bias = bt_scr[pair, p, hh]
            if nb == 1:
                logit = logit + bias[:, BLK:][None]
            else:
                first = jnp.concatenate([neg_half, bias[:, BLK:]], axis=1)
                logit = jnp.concatenate([logit[:d] + first[None], logit[d:] + bias[None]], axis=0)
            m = jnp.max(logit, axis=-1, keepdims=True)
            e = jnp.exp(logit - m)
            s = jnp.sum(e, axis=-1, keepdims=True)
            o = lax.dot_general(e.astype(BF16), vband, (((2,), (1,)), ((0,), (0,))), preferred_element_type=F32)
            res.append((m, s, o))
        o_all = jnp.where(low, res[0][2], res[1][2])
        m_all = jnp.where(low, res[0][0], res[1][0])
        s_all = jnp.where(low, res[0][1], res[1][1])
        for bi, s0 in enumerate(starts):
            o_scr[p, rows_of(s0), :] = o_all[bi]
            m_scr[p, rows_of(s0), :] = m_all[bi]
            s_scr[p, rows_of(s0), :] = s_all[bi]

    m_all = jnp.maximum(jnp.maximum(m_scr[0], m_scr[1]), m_scr[2])
    num = jnp.zeros((seq, LANES), F32)
    den = jnp.zeros((seq, LANES), F32)
    for p in range(3):
        w = jnp.exp(m_scr[p] - m_all)
        num = num + w * o_scr[p]
        den = den + w * s_scr[p]
    o_ref[0] = (num / den).astype(BF16)


def _dilated_attention(proj, rel_bias, *, batch, seq):
    qi = np.arange(BLK)[:, None]
    kj = np.arange(2 * BLK)[None, :]
    step = BLK + qi - kj
    in_band = (step >= 0) & (step <= BLK)
    bk = np.stack([np.where(in_band, _np_bucket(step * d), -1) for d in DILATIONS]).astype(np.int32)
    pairs = A_HEADS // 2

    def col(t0):
        return pl.BlockSpec((1, seq, LANES), lambda b, p: (b, 0, t0 + p))

    return pl.pallas_call(
        functools.partial(_dil_kernel, seq=seq),
        grid=(batch, pairs),
        in_specs=[pl.BlockSpec(memory_space=pltpu.SMEM),
                  pl.BlockSpec((3, BLK, 2 * BLK), lambda b, p: (0, 0, 0)),
                  col(T_QA), col(T_KA), col(T_VA)],
        out_specs=pl.BlockSpec((1, seq, LANES), lambda b, p: (b, 0, p)),
        out_shape=jax.ShapeDtypeStruct((batch, seq, pairs * LANES), BF16),
        scratch_shapes=[pltpu.VMEM((pairs, 3, 2, BLK, 2 * BLK), F32),
                        pltpu.VMEM((3, seq, LANES), F32),
                        pltpu.VMEM((3, seq, LANES), F32),
                        pltpu.VMEM((3, seq, LANES), F32)],
        compiler_params=_cparams("arbitrary", "arbitrary"),
        name="dilated_attn",
    )(rel_bias, jnp.asarray(bk), proj, proj, proj)


def _mla_kernel(cq_ref, ckv_ref, kr_ref, krs_ref, c_ref, s_ref, qn_ref, kvn_ref,
                wq_ref, wqs_ref, wk_ref, wv_ref, o_ref, q_scr, k_scr, v_scr, *, seq):
    rc = 512
    for c0 in range(0, seq, rc):
        rows = slice(c0, c0 + rc)
        cos = c_ref[0, rows, :]
        sin = s_ref[0, rows, :]
        nq = _rms(cq_ref[0, rows, :], qn_ref[...]).astype(BF16)
        nkv = _rms(ckv_ref[0, rows, :], kvn_ref[...]).astype(BF16)
        kr = kr_ref[0, rows, :] * cos + krs_ref[0, rows, :] * sin
        for h in range(B_HEADS):
            cols = slice(h * LANES, (h + 1) * LANES)
            q_scr[h, rows, :] = (_dot(nq, wq_ref[:, cols]) * cos + _dot(nq, wqs_ref[:, cols]) * sin).astype(BF16)
            k_scr[h, rows, :] = (_dot(nkv, wk_ref[:, cols]) + kr).astype(BF16)
            v = _dot(nkv, wv_ref[:, cols])
            lane_c = lax.broadcasted_iota(jnp.int32, v.shape, 1)
            v_scr[h, rows, :] = jnp.where(lane_c < HEAD_DIM, v, 1.0).astype(BF16)

    scale = (MLA_NOPE + MLA_ROPE) ** -0.5
    diff = lax.broadcasted_iota(jnp.int32, (QB, QB), 1) - lax.broadcasted_iota(jnp.int32, (QB, QB), 0)
    low = lax.broadcasted_iota(jnp.int32, (QB, LANES), 1) < HEAD_DIM

    def q_block(i, carry):
        qs = pl.multiple_of(i * QB, QB)
        q = q_scr[:, pl.ds(qs, QB), :]

        def kv_step(kb, st):
            m, acc = st
            ks = pl.multiple_of(kb * QB, QB)
            logit = lax.dot_general(q, k_scr[:, pl.ds(ks, QB), :], (((2,), (2,)), ((0,), (0,))),
                                    preferred_element_type=F32) * scale
            logit = jnp.where((diff <= qs - ks)[None], logit, NEG)
            m_new = jnp.maximum(m, jnp.max(logit, axis=-1, keepdims=True))
            e = jnp.exp(logit - m_new).astype(BF16)
            pv = lax.dot_general(e, v_scr[:, pl.ds(ks, QB), :], (((2,), (1,)), ((0,), (0,))),
                                 preferred_element_type=F32)
            return m_new, jnp.exp(m - m_new) * acc + pv

        init = (jnp.full((B_HEADS, QB, 1), NEG, F32), jnp.zeros((B_HEADS, QB, LANES), F32))
        _, acc = lax.fori_loop(0, i + 1, kv_step, init)
        for h in range(B_HEADS):
            sums = pltpu.roll(acc[h], HEAD_DIM, 1)
            o_ref[0, pl.ds(qs, QB), h * LANES:(h + 1) * LANES] = jnp.where(low, acc[h] / sums, 0.0).astype(BF16)
        return carry

    lax.fori_loop(0, seq // QB, q_block, 0)


def _mla_attention(proj, cos_m, sin_m, q_norm, kv_norm, wq, wqs, wk, wv, *, batch, seq):
    def col(t0, n):
        return pl.BlockSpec((1, seq, n * LANES), lambda b: (b, 0, t0 // n))

    def full(a):
        return pl.BlockSpec(a.shape, lambda b: (0,) * a.ndim)

    tab = pl.BlockSpec((1, seq, LANES), lambda b: (b, 0, 0))
    width = B_HEADS * LANES
    return pl.pallas_call(
        functools.partial(_mla_kernel, seq=seq),
        grid=(batch,),
        in_specs=[col(T_CQ, 2), col(T_CKV, 1), col(T_KR, 1), col(T_KRS, 1), tab, tab,
                  full(q_norm), full(kv_norm), full(wq), full(wqs), full(wk), full(wv)],
        out_specs=pl.BlockSpec((1, seq, width), lambda b: (b, 0, 0)),
        out_shape=jax.ShapeDtypeStruct((batch, seq, width), BF16),
        scratch_shapes=[pltpu.VMEM((B_HEADS, seq, LANES), BF16)] * 3,
        compiler_params=_cparams("arbitrary"),
        name="mla_attn",
    )(proj, proj, proj, proj, cos_m, sin_m, q_norm, kv_norm, wq, wqs, wk, wv)


def _sel_kernel(iq_ref, iqs_ref, iw_ref, cq_ref, sq_ref, ik_ref, iks_ref, ck_ref, sk_ref, g_ref,
                o_ref, ik_scr, iqm_scr, w_scr, key_scr, thr_scr, need_scr, *, seq):
    i = pl.program_id(1)
    qs = i * QB
    nk = seq // QB

    @pl.when(i == 0)
    def _():
        x = ik_ref[0]
        r = lax.rsqrt(jnp.mean(x * x, axis=-1, keepdims=True) + RMS_EPS)
        y = x * r * g_ref[0:1, :]
        ys = iks_ref[0] * r * g_ref[1:2, :]
        ik_scr[...] = (y * ck_ref[0] + ys * sk_ref[0]).astype(BF16)

    lane = lax.broadcasted_iota(jnp.int32, (QB, LANES), 1)
    low = lane < IDX_DIM
    cos = cq_ref[0]
    sin = sq_ref[0]
    for t in range(IDX_HEADS // 2):
        cols = slice(t * LANES, (t + 1) * LANES)
        rot = iq_ref[0, :, cols] * cos + iqs_ref[0, :, cols] * sin
        iqm_scr[2 * t] = jnp.where(low, rot, 0.0).astype(BF16)
        iqm_scr[2 * t + 1] = jnp.where(low, 0.0, rot).astype(BF16)
    iw = iw_ref[0] * (IDX_HEADS ** -0.5 * IDX_DIM ** -0.5)
    for h in range(IDX_HEADS):
        w_scr[h] = jnp.broadcast_to(iw[:, h:h + 1], (QB, QB))

    row = lax.broadcasted_iota(jnp.int32, (QB, QB), 0)
    col = lax.broadcasted_iota(jnp.int32, (QB, QB), 1)
    key_neg_inf = int(np.array(-np.inf, np.float32).view(np.int32)) ^ KEY_FLIP
    key_scr[...] = jnp.full(key_scr.shape, key_neg_inf, jnp.int32)

    def score_step(kb, carry):
        ks = pl.multiple_of(kb * QB, QB)
        rel = _dot_t(iqm_scr[...].reshape(IDX_HEADS * QB, LANES), ik_scr[pl.ds(ks, QB), :])
        sc = jnp.zeros((QB, QB), F32)
        for h in range(IDX_HEADS):
            sc = sc + jnp.maximum(rel[h * QB:(h + 1) * QB], 0.0) * w_scr[h]
        sc = jnp.where(sc == 0.0, 0.0, sc)
        sc = jnp.where(ks + col <= qs + row, sc, -jnp.inf)
        bits = lax.bitcast_convert_type(sc, jnp.int32)
        key_scr[kb] = jnp.where(bits < 0, bits ^ KEY_FLIP, bits)
        return carry

    lax.fori_loop(0, i + 1, score_step, 0)

    target = jnp.minimum(qs + lax.broadcasted_iota(jnp.int32, (QB, 1), 0) + 1, TOPK).astype(F32)

    def search(n_tiles):
        def count(mask):
            return jnp.sum(jnp.sum(mask.astype(F32), axis=0), axis=-1, keepdims=True)

        halves = (slice(0, QB // 2), slice(QB // 2, QB))

        def bit_step(bi, prefixes):
            bit = lax.shift_left(jnp.int32(1), 31 - bi)
            out = []
            for rows, prefix in zip(halves, prefixes):
                cand_u = prefix | bit
                cnt = count(key_scr[0:n_tiles, rows, :] >= (cand_u ^ INT_MIN)[None])
                out.append(jnp.where(cnt >= target[rows], cand_u, prefix))
            return tuple(out)

        zero = jnp.zeros((QB // 2, 1), jnp.int32)
        prefix = jnp.concatenate(lax.fori_loop(0, 32, bit_step, (zero, zero)), axis=0)
        thr = prefix ^ INT_MIN
        thr_scr[...] = thr
        need_scr[...] = target - count(key_scr[0:n_tiles] > thr[None])

    for n_tiles in range(1, nk + 1):
        pl.when(i == n_tiles - 1)(functools.partial(search, n_tiles))
    thr = thr_scr[...]
    need = need_scr[...]

    upper = (row <= col).astype(BF16)
    carry = jnp.zeros((QB, 1), F32)
    for c in range(nk):
        kc = key_scr[c]
        eq = kc == thr
        rank = carry + _dot(eq.astype(BF16), upper)
        sel = (kc > thr) | (eq & (rank <= need))
        o_ref[0, 0, c] = jnp.where(sel, 0.0, NEG).astype(BF16)
        carry = rank[:, QB - 1:QB]


def _dsa_select(proj, cos_i, sin_i, ik_gains, *, batch, seq):
    nq = seq // QB

    def qcol(t0, n):
        return pl.BlockSpec((1, QB, n * LANES), lambda b, i: (b, i, t0 // n))

    def kcol(t0):
        return pl.BlockSpec((1, seq, LANES), lambda b, i: (b, 0, t0))

    qtab = pl.BlockSpec((1, QB, LANES), lambda b, i: (b, i, 0))
    ktab = pl.BlockSpec((1, seq, LANES), lambda b, i: (b, 0, 0))
    return pl.pallas_call(
        functools.partial(_sel_kernel, seq=seq),
        grid=(batch, nq),
        in_specs=[qcol(T_IQ, 4), qcol(T_IQS, 4), qcol(T_IW, 1), qtab, qtab,
                  kcol(T_IK), kcol(T_IKS), ktab, ktab,
                  pl.BlockSpec((2, LANES), lambda b, i: (0, 0))],
        out_specs=pl.BlockSpec((1, 1, nq, QB, QB), lambda b, i: (b, i, 0, 0, 0)),
        out_shape=jax.ShapeDtypeStruct((batch, nq, nq, QB, QB), BF16),
        scratch_shapes=[pltpu.VMEM((seq, LANES), BF16),
                        pltpu.VMEM((IDX_HEADS, QB, LANES), BF16),
                        pltpu.VMEM((IDX_HEADS, QB, QB), F32),
                        pltpu.VMEM((nq, QB, QB), jnp.int32),
                        pltpu.VMEM((QB, 1), jnp.int32),
                        pltpu.VMEM((QB, 1), F32)],
        compiler_params=_cparams("arbitrary", "arbitrary"),
        name="dsa_select",
    )(proj, proj, proj, cos_i, sin_i, proj, proj, cos_i, sin_i, ik_gains)


def _dsa_kernel(tab_ref, bk_ref, q_ref, k_ref, v_ref, sel_ref, o_ref, kb_scr, vb_scr, bt_scr):
    b = pl.program_id(0)
    i = pl.program_id(1)

    n_pairs = (C_HEADS + 1) // 2
    heads_of = [min(2, C_HEADS - 2 * pt) for pt in range(n_pairs)]

    @pl.when((b == 0) & (i == 0))
    def _():
        for pt in range(n_pairs):
            for hh in range(heads_of[pt]):
                for dd in range(3):
                    bt_scr[3 * pt + dd, hh * QB:(hh + 1) * QB, :] = _bias_from_buckets(
                        bk_ref[dd], tab_ref, A_HEADS + 2 * pt + hh)

    @pl.when(i == 0)
    def _():
        kb_scr[...] = k_ref[0].astype(BF16)
        vb_scr[...] = v_ref[0].astype(BF16)

    lane = lax.broadcasted_iota(jnp.int32, (QB, LANES), 1)
    low = lane < HEAD_DIM
    q2 = []
    for pt in range(n_pairs):
        q = q_ref[0, :, pt * LANES:(pt + 1) * LANES] * (HEAD_DIM ** -0.5)
        halves = [jnp.where(low, q, 0.0), jnp.where(low, 0.0, q)][:heads_of[pt]]
        q2.append(jnp.concatenate(halves, axis=0).astype(BF16))

    def kv_step(kb, st):
        ks = pl.multiple_of(kb * QB, QB)
        selb = sel_ref[0, 0, kb].astype(F32)
        dd = jnp.minimum(i - kb, 2)
        new = []
        for pt in range(n_pairs):
            rows = heads_of[pt] * QB
            cols = slice(pt * LANES, (pt + 1) * LANES)
            m, s, acc = st[pt]
            mask = jnp.concatenate([selb] * heads_of[pt], axis=0)
            logit = _dot_t(q2[pt], kb_scr[pl.ds(ks, QB), cols]) + bt_scr[3 * pt + dd, 0:rows, :] + mask
            m_new = jnp.maximum(m, jnp.max(logit, axis=-1, keepdims=True))
            alpha = jnp.exp(m - m_new)
            e = jnp.where(mask == 0.0, jnp.exp(logit - m_new), 0.0)
            s = alpha * s + jnp.sum(e, axis=-1, keepdims=True)
            acc = alpha * acc + _dot(e.astype(BF16), vb_scr[pl.ds(ks, QB), cols])
            new.append((m_new, s, acc))
        return tuple(new)

    init = tuple((jnp.full((h * QB, 1), NEG, F32), jnp.zeros((h * QB, 1), F32), jnp.zeros((h * QB, LANES), F32))
                 for h in heads_of)
    st = lax.fori_loop(0, i + 1, kv_step, init)
    for pt in range(n_pairs):
        _, s, acc = st[pt]
        out = acc / s
        upper = out[QB:] if heads_of[pt] == 2 else 0.0
        o_ref[0, :, pt * LANES:(pt + 1) * LANES] = jnp.where(low, out[:QB], upper).astype(BF16)


def _dsa_attention(proj, sel, rel_bias, *, batch, seq):
    nq = seq // QB
    qi = np.arange(QB)[:, None]
    kj = np.arange(QB)[None, :]
    bk = np.stack([_np_bucket(qi - kj), _np_bucket(QB + qi - kj),
                   _np_bucket(np.full((QB, QB), 2 * QB))]).astype(np.int32)
    width = ((C_HEADS + 1) // 2) * LANES
    return pl.pallas_call(
        _dsa_kernel,
        grid=(batch, nq),
        in_specs=[pl.BlockSpec(memory_space=pltpu.SMEM),
                  pl.BlockSpec((3, QB, QB), lambda b, i: (0, 0, 0)),
                  pl.BlockSpec((1, QB, width), lambda b, i: (b, i, T_QC * LANES // width)),
                  pl.BlockSpec((1, seq, width), lambda b, i: (b, 0, T_KC * LANES // width)),
                  pl.BlockSpec((1, seq, width), lambda b, i: (b, 0, T_VC * LANES // width)),
                  pl.BlockSpec((1, 1, nq, QB, QB), lambda b, i: (b, i, 0, 0, 0))],
        out_specs=pl.BlockSpec((1, QB, width), lambda b, i: (b, i, 0)),
        out_shape=jax.ShapeDtypeStruct((batch, seq, width), BF16),
        scratch_shapes=[pltpu.VMEM((seq, width), BF16),
                        pltpu.VMEM((seq, width), BF16),
                        pltpu.VMEM((3 * ((C_HEADS + 1) // 2), 2 * QB, QB), F32)],
        compiler_params=_cparams("arbitrary", "arbitrary"),
        name="dsa_attn",
    )(rel_bias, jnp.asarray(bk), proj, proj, proj, sel)


def _out_kernel(oa_ref, oc_ref, ob_ref, wa_ref, wc_ref, wb_ref, h_ref, mod_ref, o_ref):
    acc = _dot(oa_ref[...], wa_ref[...]) + _dot(oc_ref[...], wc_ref[...]) + _dot(ob_ref[...], wb_ref[...])
    o_ref[...] = h_ref[...] + mod_ref[0, 5:6, :] * acc


def _out_proj(oa, oc, ob, wa, wc, wb, h, mod, *, seq):
    M, D = h.shape
    tm = 512
    per_b = seq // tm

    def rows(a):
        return pl.BlockSpec((tm, a.shape[1]), lambda i: (i, 0))

    def full(a):
        return pl.BlockSpec(a.shape, lambda i: (0, 0))

    return pl.pallas_call(
        _out_kernel,
        grid=(M // tm,),
        in_specs=[rows(oa), rows(oc), rows(ob), full(wa), full(wc), full(wb), rows(h),
                  pl.BlockSpec((1, N_MOD, D), lambda i: (i // per_b, 0, 0))],
        out_specs=pl.BlockSpec((tm, D), lambda i: (i, 0)),
        out_shape=jax.ShapeDtypeStruct((M, D), F32),
        compiler_params=_cparams("arbitrary"),
        name="out_proj",
    )(oa, oc, ob, wa, wc, wb, h, mod)


def _swap_halves(w, rope):
    half = rope // 2
    return jnp.concatenate([-w[..., half:rope], w[..., :half], jnp.zeros_like(w[..., rope:])], axis=-1)


def _pad_cols(w, n):
    return jnp.concatenate([w, jnp.zeros(w.shape[:-1] + (n - w.shape[-1],), w.dtype)], axis=-1)


def _layout_w_in(w):
    D = w.shape[0]
    sizes = (A_HEADS * HEAD_DIM,) * 3 + (MLA_Q_RANK, MLA_KV_RANK, MLA_ROPE) + (C_HEADS * HEAD_DIM,) * 3 + (
        IDX_HEADS * IDX_DIM, IDX_DIM, IDX_HEADS)
    offs = np.cumsum((0,) + sizes)
    qa, ka, va, cq, ckv, kr, qc, kc, vc, iq, ik, iw = [w[:, offs[n]:offs[n + 1]] for n in range(len(sizes))]
    c_w = ((C_HEADS + 1) // 2) * LANES
    z64 = jnp.zeros((D, MLA_NOPE), w.dtype)
    iq_h = iq.reshape(D, IDX_HEADS, IDX_DIM)
    iq_sw = _swap_halves(iq_h, IDX_ROPE).reshape(D, IDX_HEADS * IDX_DIM)
    ik_sw = _swap_halves(ik, IDX_ROPE)
    cols = [qa, ka, va, _pad_cols(qc, c_w), _pad_cols(kc, c_w), _pad_cols(vc, c_w), cq, iq, iq_sw, ckv,
            _pad_cols(jnp.concatenate([z64, kr], axis=-1), LANES),
            _pad_cols(jnp.concatenate([z64, _swap_halves(kr, MLA_ROPE)], axis=-1), LANES),
            jnp.concatenate([ik, ik], axis=-1), jnp.concatenate([ik_sw, ik_sw], axis=-1), _pad_cols(iw, LANES)]
    out = jnp.concatenate(cols, axis=-1)
    assert out.shape[1] == N_TILES * LANES
    return out.astype(BF16)


def _layout_mla(w_uq, w_ukv):
    rq = w_uq.reshape(MLA_Q_RANK, B_HEADS, MLA_NOPE + MLA_ROPE)
    nope, rope = rq[..., :MLA_NOPE], rq[..., MLA_NOPE:]
    wq = _pad_cols(jnp.concatenate([nope, rope], axis=-1), LANES)
    wqs = _pad_cols(jnp.concatenate([jnp.zeros_like(nope), _swap_halves(rope, MLA_ROPE)], axis=-1), LANES)
    rkv = w_ukv.reshape(MLA_KV_RANK, B_HEADS, MLA_NOPE + HEAD_DIM)
    wk = _pad_cols(rkv[..., :MLA_NOPE], LANES)
    wv = _pad_cols(rkv[..., MLA_NOPE:], LANES)
    flat = lambda a, r: a.reshape(r, B_HEADS * LANES).astype(BF16)
    return flat(wq, MLA_Q_RANK), flat(wqs, MLA_Q_RANK), flat(wk, MLA_KV_RANK), flat(wv, MLA_KV_RANK)


def _layout_w_out(w):
    D = w.shape[1]
    a, b_, c_ = A_HEADS * HEAD_DIM, B_HEADS * HEAD_DIM, C_HEADS * HEAD_DIM
    wa = w[:a]
    wb = w[a:a + b_].reshape(B_HEADS, HEAD_DIM, D)
    wb = jnp.concatenate([wb, jnp.zeros_like(wb)], axis=1).reshape(B_HEADS * LANES, D)
    wc = w[a + b_:a + b_ + c_]
    wc = jnp.concatenate([wc, jnp.zeros((((C_HEADS + 1) // 2) * LANES - c_, D), w.dtype)], axis=0)
    return wa.astype(BF16), wc.astype(BF16), wb.astype(BF16)


def kernel(x, c, positions, rel_bias, ada_w, ada_b, norm_ffn1, ffn1_w_gu, ffn1_w_down, norm_mix, w_in, mla_q_norm, mla_w_uq, mla_kv_norm, mla_w_ukv, idx_k_norm, w_out, norm_ffn2, ffn2_w_gu, ffn2_w_down, final_norm):
    B, T, D = x.shape
    L = ada_w.shape[0]
    assert T % (max(DILATIONS) * BLK) == 0 and T % 512 == 0 and D % LANES == 0

    mods = _modulation(c, ada_w, ada_b).reshape(L, B, N_MOD, D)
    cos_i, sin_i, cos_m, sin_m = _rope_tables(positions)
    final_gain = final_norm.reshape(1, D)
    h = x.reshape(B * T, D)
    for l in range(L):
        mod = mods[l]
        h = _ffn(h, mod, norm_ffn1[l].reshape(1, D), ffn1_w_gu[l].astype(BF16), ffn1_w_down[l].astype(BF16),
                 final_gain, ks=0, final=False, seq=T)

        proj = _in_proj(h, mod, norm_mix[l].reshape(1, D), _layout_w_in(w_in[l]), seq=T)
        proj = proj.reshape(B, T, N_TILES * LANES)
        oa = _dilated_attention(proj, rel_bias, batch=B, seq=T)
        wq, wqs, wk, wv = _layout_mla(mla_w_uq[l], mla_w_ukv[l])
        ob = _mla_attention(proj, cos_m, sin_m, mla_q_norm[l].reshape(1, -1), mla_kv_norm[l].reshape(1, -1),
                            wq, wqs, wk, wv, batch=B, seq=T)
        g = idx_k_norm[l]
        g_sw = jnp.concatenate([g[IDX_ROPE // 2:IDX_ROPE], g[:IDX_ROPE // 2], jnp.zeros_like(g[IDX_ROPE:])])
        ik_gains = jnp.stack([jnp.concatenate([g, g]), jnp.concatenate([g_sw, g_sw])])
        sel = _dsa_select(proj, cos_i, sin_i, ik_gains, batch=B, seq=T)
        oc = _dsa_attention(proj, sel, rel_bias, batch=B, seq=T)
        wa, wc, wb = _layout_w_out(w_out[l])
        h = _out_proj(oa.reshape(B * T, -1), oc.reshape(B * T, -1), ob.reshape(B * T, -1), wa, wc, wb, h, mod, seq=T)

        h = _ffn(h, mod, norm_ffn2[l].reshape(1, D), ffn2_w_gu[l].astype(BF16), ffn2_w_down[l].astype(BF16),
                 final_gain, ks=6, final=(l == L - 1), seq=T)
    return h.reshape(B, T, D)
```

```python
import functools
import math

import numpy as np
import jax
import jax.numpy as jnp
from jax import lax
from jax.experimental import pallas as pl
from jax.experimental.pallas import tpu as pltpu

F32 = jnp.float32
BF16 = jnp.bfloat16

HEAD_DIM = 64
A_HEADS, B_HEADS, C_HEADS = 6, 5, 5
MLA_Q_RANK, MLA_KV_RANK, MLA_NOPE, MLA_ROPE = 256, 128, 64, 32
IDX_HEADS, IDX_DIM, IDX_ROPE = 8, 64, 32
TOPK = 256
N_BUCKETS, MAX_DISTANCE = 32, 128
ROPE_THETA = 10000.0
RMS_EPS = 1e-6
N_MOD = 9
NEG = -1e30
DILATIONS = (1, 4, 16)

LANES = 128
BLK = 128
QB = 256
VMEM_LIMIT = 56 * 1024 * 1024

T_QA, T_KA, T_VA = 0, 3, 6
T_QC, T_KC, T_VC = 9, 12, 15
T_CQ, T_IQ, T_IQS = 18, 20, 24
T_CKV, T_KR, T_KRS, T_IK, T_IKS, T_IW = 28, 29, 30, 31, 32, 33
N_TILES = 34

KEY_FLIP = 0x7FFFFFFF
INT_MIN = -(2 ** 31)


def _cparams(*sem):
    return pltpu.CompilerParams(dimension_semantics=sem, vmem_limit_bytes=VMEM_LIMIT)


def _rms(x, g):
    return x * lax.rsqrt(jnp.mean(x * x, axis=-1, keepdims=True) + RMS_EPS) * g


def _dot(a, b):
    return jnp.dot(a, b, preferred_element_type=F32)


def _dot_t(a, b):
    return lax.dot_general(a, b, (((1,), (1,)), ((), ())), preferred_element_type=F32)


def _np_bucket(dist):
    n = np.maximum(dist, 0)
    max_exact = N_BUCKETS // 2
    nf = np.maximum(n, 1).astype(np.float32)
    large = max_exact + (np.log(nf / np.float32(max_exact)) / np.float32(math.log(MAX_DISTANCE / max_exact))
                         * np.float32(N_BUCKETS - max_exact)).astype(np.int32)
    large = np.minimum(large, N_BUCKETS - 1)
    return np.where(n < max_exact, n, large).astype(np.int32)


def _bias_from_buckets(bucket, tab_ref, col):
    acc = jnp.full(bucket.shape, NEG, F32)
    for b in range(N_BUCKETS):
        acc = jnp.where(bucket == b, tab_ref[b, col], acc)
    return acc


def _mod_kernel(c_ref, w_ref, b_ref, o_ref):
    c = c_ref[...]
    cond = c / (1.0 + jnp.exp(-c))
    w = w_ref[0]
    c_hi = cond.astype(BF16)
    c_lo = (cond - c_hi.astype(F32)).astype(BF16)
    w_hi = w.astype(BF16)
    w_lo = (w - w_hi.astype(F32)).astype(BF16)
    o_ref[0] = _dot(c_hi, w_hi) + _dot(c_hi, w_lo) + _dot(c_lo, w_hi) + b_ref[0]


def _modulation(c, ada_w, ada_b):
    L, D, N = ada_w.shape
    B = c.shape[0]
    tn = 1024
    return pl.pallas_call(
        _mod_kernel,
        grid=(L, N // tn),
        in_specs=[pl.BlockSpec((B, D), lambda l, j: (0, 0)),
                  pl.BlockSpec((1, D, tn), lambda l, j: (l, 0, j)),
                  pl.BlockSpec((1, 1, tn), lambda l, j: (l, 0, j))],
        out_specs=pl.BlockSpec((1, B, tn), lambda l, j: (l, 0, j)),
        out_shape=jax.ShapeDtypeStruct((L, B, N), F32),
        compiler_params=_cparams("arbitrary", "arbitrary"),
        name="adaln_mod",
    )(c, ada_w, ada_b.reshape(L, 1, N))


def _ffn_kernel(h_ref, mod_ref, g_ref, wgu_ref, wd_ref, fg_ref, o_ref, *, ks, final, tf):
    F = wd_ref.shape[0]
    x = h_ref[...]
    y = _rms(x, g_ref[...])
    n = (y * (1.0 + mod_ref[0, ks + 1:ks + 2, :]) + mod_ref[0, ks:ks + 1, :]).astype(BF16)
    acc = jnp.zeros(x.shape, F32)
    for f0 in range(0, F, tf):
        g = _dot(n, wgu_ref[:, f0:f0 + tf])
        u = _dot(n, wgu_ref[:, F + f0:F + f0 + tf])
        a = (g / (1.0 + jnp.exp(-g))) * u
        acc = acc + _dot(a.astype(BF16), wd_ref[f0:f0 + tf, :])
    out = x + (0.5 * mod_ref[0, ks + 2:ks + 3, :]) * acc
    if final:
        out = _rms(out, fg_ref[...])
    o_ref[...] = out


def _resident(a):
    return pl.BlockSpec(a.shape, lambda *_: (0,) * a.ndim, pipeline_mode=pl.Buffered(1))


def _ffn(h, mod, gain, w_gu, w_down, final_gain, *, ks, final, seq):
    M, D = h.shape
    F = w_down.shape[0]
    tm = 512
    tf = 256
    assert F % tf == 0
    per_b = seq // tm
    return pl.pallas_call(
        functools.partial(_ffn_kernel, ks=ks, final=final, tf=tf),
        grid=(M // tm,),
        in_specs=[pl.BlockSpec((tm, D), lambda i: (i, 0)),
                  pl.BlockSpec((1, N_MOD, D), lambda i: (i // per_b, 0, 0)),
                  pl.BlockSpec((1, D), lambda i: (0, 0)),
                  _resident(w_gu), _resident(w_down),
                  pl.BlockSpec((1, D), lambda i: (0, 0))],
        out_specs=pl.BlockSpec((tm, D), lambda i: (i, 0)),
        out_shape=jax.ShapeDtypeStruct((M, D), F32),
        compiler_params=_cparams("arbitrary"),
        name="ffn",
    )(h, mod, gain, w_gu, w_down, final_gain)


def _win_kernel(h_ref, mod_ref, g_ref, w_ref, o_ref, *, tn):
    y = _rms(h_ref[...], g_ref[...])
    n = (y * (1.0 + mod_ref[0, 4:5, :]) + mod_ref[0, 3:4, :]).astype(BF16)
    for c0 in range(0, w_ref.shape[1], tn):
        o_ref[:, c0:c0 + tn] = _dot(n, w_ref[:, c0:c0 + tn])


def _in_proj(h, mod, gain, w, *, seq):
    M, D = h.shape
    N = w.shape[1]
    tm, tn = 512, 256
    assert N % tn == 0
    per_b = seq // tm
    return pl.pallas_call(
        functools.partial(_win_kernel, tn=tn),
        grid=(M // tm,),
        in_specs=[pl.BlockSpec((tm, D), lambda i: (i, 0)),
                  pl.BlockSpec((1, N_MOD, D), lambda i: (i // per_b, 0, 0)),
                  pl.BlockSpec((1, D), lambda i: (0, 0)),
                  _resident(w)],
        out_specs=pl.BlockSpec((tm, N), lambda i: (i, 0)),
        out_shape=jax.ShapeDtypeStruct((M, N), F32),
        compiler_params=_cparams("arbitrary"),
        name="in_proj",
    )(h, mod, gain, w)


def _rope_kernel(pos_ref, inv_ref, ci_ref, si_ref, cm_ref, sm_ref):
    pos = pos_ref[0].astype(F32)
    ang_i = pos * inv_ref[0:1, :]
    ang_m = pos * inv_ref[1:2, :]
    ci_ref[0] = jnp.cos(ang_i)
    si_ref[0] = jnp.sin(ang_i)
    cm_ref[0] = jnp.cos(ang_m)
    sm_ref[0] = jnp.sin(ang_m)


def _rope_tables(positions):
    B, T = positions.shape
    half = IDX_ROPE // 2
    inv = (ROPE_THETA ** (-np.arange(half, dtype=np.float32) / half)).astype(np.float32)
    zeros32 = np.zeros(IDX_DIM - IDX_ROPE, np.float32)
    row_idx = np.concatenate([inv, inv, zeros32, inv, inv, zeros32])
    row_mla = np.concatenate([np.zeros(MLA_NOPE, np.float32), inv, inv,
                              np.zeros(LANES - MLA_NOPE - MLA_ROPE, np.float32)])
    inv_rows = jnp.asarray(np.stack([row_idx, row_mla]))
    tab = jax.ShapeDtypeStruct((B, T, LANES), F32)
    spec = pl.BlockSpec((1, T, LANES), lambda b: (b, 0, 0))
    return pl.pallas_call(
        _rope_kernel,
        grid=(B,),
        in_specs=[pl.BlockSpec((1, T, 1), lambda b: (b, 0, 0)),
                  pl.BlockSpec((2, LANES), lambda b: (0, 0))],
        out_specs=[spec, spec, spec, spec],
        out_shape=[tab, tab, tab, tab],
        compiler_params=_cparams("arbitrary"),
        name="rope_tables",
    )(positions.reshape(B, T, 1), inv_rows)


def _dil_kernel(tab_ref, bk_ref, q_ref, k_ref, v_ref, o_ref, bt_scr, o_scr, m_scr, s_scr, *, seq):
    b = pl.program_id(0)
    pair = pl.program_id(1)

    @pl.when(b == 0)
    def _():
        for p in range(3):
            for hh in range(2):
                bt_scr[pair, p, hh] = _bias_from_buckets(bk_ref[p], tab_ref, 2 * pair + hh)

    nblk = seq // BLK
    low = lax.broadcasted_iota(jnp.int32, (nblk, BLK, LANES), 2) < HEAD_DIM
    neg_half = jnp.full((BLK, BLK), NEG, F32)

    for p, d in enumerate(DILATIONS):
        nb = seq // (d * BLK)
        starts = [r + d * BLK * n for n in range(nb) for r in range(d)]

        def rows_of(s0, d=d):
            return pl.ds(s0, BLK) if d == 1 else pl.ds(s0, BLK, stride=d)

        def regroup(ref):
            return jnp.stack([ref[0, rows_of(s0), :] for s0 in starts])

        def band(z):
            if nb == 1:
                return z
            prev = jnp.concatenate([jnp.zeros((d,) + z.shape[1:], z.dtype), z[:-d]], axis=0)
            return jnp.concatenate([prev, z], axis=1)

        q = regroup(q_ref) * (HEAD_DIM ** -0.5)
        kband = band(regroup(k_ref).astype(BF16))
        vband = band(regroup(v_ref).astype(BF16))
        res = []
        for hh in range(2):
            qh = (jnp.where(low, q, 0.0) if hh == 0 else jnp.where(low, 0.0, q)).astype(BF16)
            logit = lax.dot_general(qh, kband, (((2,), (2,)), ((0,), (0,))), preferred_element_type=F32)
            bias = bt_scr[pair, p, hh]
            if nb == 1:
                logit = logit + bias[:, BLK:][None]
            else:
                first = jnp.concatenate([neg_half, bias[:, BLK:]], axis=1)
                logit = jnp.concatenate([logit[:d] + first[None], logit[d:] + bias[None]], axis=0)
            m = jnp.max(logit, axis=-1, keepdims=True)
            e = jnp.exp(logit - m)
            s = jnp.sum(e, axis=-1, keepdims=True)
            o = lax.dot_general(e.astype(BF16), vband, (((2,), (1,)), ((0,), (0,))), preferred_element_type=F32)
            res.append((m, s, o))
        o_all = jnp.where(low, res[0][2], res[1][2])
        m_all = jnp.where(low, res[0][0], res[1][0])
        s_all = jnp.where(low, res[0][1], res[1][1])
        for bi, s0 in enumerate(starts):
            o_scr[p, rows_of(s0), :] = o_all[bi]
            m_scr[p, rows_of(s0), :] = m_all[bi]
            s_scr[p, rows_of(s0), :] = s_all[bi]

    m_all = jnp.maximum(jnp.maximum(m_scr[0], m_scr[1]), m_scr[2])
    num = jnp.zeros((seq, LANES), F32)
    den = jnp.zeros((seq, LANES), F32)
    for p in range(3):
        w = jnp.exp(m_scr[p] - m_all)
        num = num + w * o_scr[p]
        den = den + w * s_scr[p]
    o_ref[0] = (num / den).astype(BF16)


def _dilated_attention(proj, rel_bias, *, batch, seq):
    qi = np.arange(BLK)[:, None]
    kj = np.arange(2 * BLK)[None, :]
    step = BLK + qi - kj
    in_band = (step >= 0) & (step <= BLK)
    bk = np.stack([np.where(in_band, _np_bucket(step * d), -1) for d in DILATIONS]).astype(np.int32)
    pairs = A_HEADS // 2

    def col(t0):
        return pl.BlockSpec((1, seq, LANES), lambda b, p: (b, 0, t0 + p))

    return pl.pallas_call(
        functools.partial(_dil_kernel, seq=seq),
        grid=(batch, pairs),
        in_specs=[pl.BlockSpec(memory_space=pltpu.SMEM),
                  pl.BlockSpec((3, BLK, 2 * BLK), lambda b, p: (0, 0, 0)),
                  col(T_QA), col(T_KA), col(T_VA)],
        out_specs=pl.BlockSpec((1, seq, LANES), lambda b, p: (b, 0, p)),
        out_shape=jax.ShapeDtypeStruct((batch, seq, pairs * LANES), BF16),
        scratch_shapes=[pltpu.VMEM((pairs, 3, 2, BLK, 2 * BLK), F32),
                        pltpu.VMEM((3, seq, LANES), F32),
                        pltpu.VMEM((3, seq, LANES), F32),
                        pltpu.VMEM((3, seq, LANES), F32)],
        compiler_params=_cparams("arbitrary", "arbitrary"),
        name="dilated_attn",
    )(rel_bias, jnp.asarray(bk), proj, proj, proj)


def _mla_kernel(cq_ref, ckv_ref, kr_ref, krs_ref, c_ref, s_ref, qn_ref, kvn_ref,
                wq_ref, wqs_ref, wk_ref, wv_ref, o_ref, q_scr, k_scr, v_scr, *, seq):
    rc = 512
    for c0 in range(0, seq, rc):
        rows = slice(c0, c0 + rc)
        cos = c_ref[0, rows, :]
        sin = s_ref[0, rows, :]
        nq = _rms(cq_ref[0, rows, :], qn_ref[...]).astype(BF16)
        nkv = _rms(ckv_ref[0, rows, :], kvn_ref[...]).astype(BF16)
        kr = kr_ref[0, rows, :] * cos + krs_ref[0, rows, :] * sin
        for h in range(B_HEADS):
            cols = slice(h * LANES, (h + 1) * LANES)
            q_scr[h, rows, :] = (_dot(nq, wq_ref[:, cols]) * cos + _dot(nq, wqs_ref[:, cols]) * sin).astype(BF16)
            k_scr[h, rows, :] = (_dot(nkv, wk_ref[:, cols]) + kr).astype(BF16)
            v = _dot(nkv, wv_ref[:, cols])
            lane_c = lax.broadcasted_iota(jnp.int32, v.shape, 1)
            v_scr[h, rows, :] = jnp.where(lane_c < HEAD_DIM, v, 1.0).astype(BF16)

    exp2_scale = (MLA_NOPE + MLA_ROPE) ** -0.5 * math.log2(math.e)
    causal = lax.broadcasted_iota(jnp.int32, (QB, QB), 1) <= lax.broadcasted_iota(jnp.int32, (QB, QB), 0)
    low = lax.broadcasted_iota(jnp.int32, (QB, LANES), 1) < HEAD_DIM
    for qb in range(seq // QB):
        qs, ext = qb * QB, (qb + 1) * QB
        for h in range(B_HEADS):
            logit = _dot_t(q_scr[h, qs:qs + QB, :], k_scr[h, 0:ext, :])
            diag = jnp.where(causal, logit[:, qs:], NEG)
            logit = diag if qb == 0 else jnp.concatenate([logit[:, :qs], diag], axis=1)
            m = jnp.max(logit, axis=-1, keepdims=True)
            e = jnp.exp2((logit - m) * exp2_scale).astype(BF16)
            pv = _dot(e, v_scr[h, 0:ext, :])
            sums = pltpu.roll(pv, HEAD_DIM, 1)
            o_ref[0, qs:qs + QB, h * LANES:(h + 1) * LANES] = jnp.where(low, pv / sums, 0.0).astype(BF16)


def _mla_attention(proj, cos_m, sin_m, q_norm, kv_norm, wq, wqs, wk, wv, *, batch, seq):
    def col(t0, n):
        return pl.BlockSpec((1, seq, n * LANES), lambda b: (b, 0, t0 // n))

    def full(a):
        return pl.BlockSpec(a.shape, lambda b: (0,) * a.ndim)

    tab = pl.BlockSpec((1, seq, LANES), lambda b: (b, 0, 0))
    width = B_HEADS * LANES
    return pl.pallas_call(
        functools.partial(_mla_kernel, seq=seq),
        grid=(batch,),
        in_specs=[col(T_CQ, 2), col(T_CKV, 1), col(T_KR, 1), col(T_KRS, 1), tab, tab,
                  full(q_norm), full(kv_norm), full(wq), full(wqs), full(wk), full(wv)],
        out_specs=pl.BlockSpec((1, seq, width), lambda b: (b, 0, 0)),
        out_shape=jax.ShapeDtypeStruct((batch, seq, width), BF16),
        scratch_shapes=[pltpu.VMEM((B_HEADS, seq, LANES), BF16)] * 3,
        compiler_params=_cparams("arbitrary"),
        name="mla_attn",
    )(proj, proj, proj, proj, cos_m, sin_m, q_norm, kv_norm, wq, wqs, wk, wv)


def _sel_kernel(iq_ref, iqs_ref, iw_ref, cq_ref, sq_ref, ik_ref, iks_ref, ck_ref, sk_ref, g_ref,
                o_ref, ik_scr, iqm_scr, key_scr, thr_scr, need_scr, *, seq):
    i = pl.program_id(1)
    qs = i * QB
    nk = seq // QB

    @pl.when(i == 0)
    def _():
        x = ik_ref[0]
        r = lax.rsqrt(jnp.mean(x * x, axis=-1, keepdims=True) + RMS_EPS)
        y = x * r * g_ref[0:1, :]
        ys = iks_ref[0] * r * g_ref[1:2, :]
        ik_scr[...] = (y * ck_ref[0] + ys * sk_ref[0]).astype(BF16)

    lane = lax.broadcasted_iota(jnp.int32, (QB, LANES), 1)
    low = lane < IDX_DIM
    cos = cq_ref[0]
    sin = sq_ref[0]
    for t in range(IDX_HEADS // 2):
        cols = slice(t * LANES, (t + 1) * LANES)
        rot = iq_ref[0, :, cols] * cos + iqs_ref[0, :, cols] * sin
        iqm_scr[2 * t] = jnp.where(low, rot, 0.0).astype(BF16)
        iqm_scr[2 * t + 1] = jnp.where(low, 0.0, rot).astype(BF16)
    w_t = (iw_ref[0] * (IDX_HEADS ** -0.5 * IDX_DIM ** -0.5)).T

    krow = lax.broadcasted_iota(jnp.int32, (QB, QB), 0)
    qcol = lax.broadcasted_iota(jnp.int32, (QB, QB), 1)
    key_neg_inf = int(np.array(-np.inf, np.float32).view(np.int32)) ^ KEY_FLIP
    key_scr[...] = jnp.full(key_scr.shape, key_neg_inf, jnp.int32)

    def score_step(kb, carry):
        ks = pl.multiple_of(kb * QB, QB)
        rel = _dot_t(ik_scr[pl.ds(ks, QB), :], iqm_scr[...].reshape(IDX_HEADS * QB, LANES))
        sc = jnp.zeros((QB, QB), F32)
        for h in range(IDX_HEADS):
            sc = sc + jnp.maximum(rel[:, h * QB:(h + 1) * QB], 0.0) * w_t[h:h + 1, :]
        sc = jnp.where(sc == 0.0, 0.0, sc)
        sc = jnp.where(ks + krow <= qs + qcol, sc, -jnp.inf)
        bits = lax.bitcast_convert_type(sc, jnp.int32)
        key_scr[kb] = jnp.where(bits < 0, bits ^ KEY_FLIP, bits)
        return carry

    lax.fori_loop(0, i + 1, score_step, 0)

    target = jnp.minimum(qs + lax.broadcasted_iota(jnp.int32, (1, QB), 1) + 1, TOPK).astype(F32)

    def search(n_tiles):
        def count(pred):
            acc = jnp.zeros((8, QB), F32)
            for c in range(n_tiles):
                acc = acc + jnp.sum(pred(key_scr[c]).astype(F32).reshape(QB // 8, 8, QB), axis=0)
            return jnp.sum(acc, axis=0, keepdims=True)

        def bit_step(bi, prefix):
            cand_u = prefix | lax.shift_left(jnp.int32(1), 31 - bi)
            cand = cand_u ^ INT_MIN
            cnt = count(lambda k: k >= cand)
            return jnp.where(cnt >= target, cand_u, prefix)

        thr = lax.fori_loop(0, 32, bit_step, jnp.zeros((1, QB), jnp.int32)) ^ INT_MIN
        thr_scr[...] = thr
        need_scr[...] = target - count(lambda k: k > thr)

    for n_tiles in range(1, nk + 1):
        pl.when(i == n_tiles - 1)(functools.partial(search, n_tiles))
    thr = thr_scr[...]
    need = need_scr[...]

    lower = (qcol <= krow).astype(BF16)
    carry = jnp.zeros((1, QB), F32)
    for c in range(nk):
        kc = key_scr[c]
        eq = kc == thr
        rank = carry + _dot(lower, eq.astype(BF16))
        sel = (kc > thr) | (eq & (rank <= need))
        o_ref[0, 0, c] = jnp.where(sel, 0.0, NEG).T.astype(BF16)
        carry = rank[QB - 1:QB, :]


def _dsa_select(proj, cos_i, sin_i, ik_gains, *, batch, seq):
    nq = seq // QB

    def qcol(t0, n):
        return pl.BlockSpec((1, QB, n * LANES), lambda b, i: (b, i, t0 // n))

    def kcol(t0):
        return pl.BlockSpec((1, seq, LANES), lambda b, i: (b, 0, t0))

    qtab = pl.BlockSpec((1, QB, LANES), lambda b, i: (b, i, 0))
    ktab = pl.BlockSpec((1, seq, LANES), lambda b, i: (b, 0, 0))
    return pl.pallas_call(
        functools.partial(_sel_kernel, seq=seq),
        grid=(batch, nq),
        in_specs=[qcol(T_IQ, 4), qcol(T_IQS, 4), qcol(T_IW, 1), qtab, qtab,
                  kcol(T_IK), kcol(T_IKS), ktab, ktab,
                  pl.BlockSpec((2, LANES), lambda b, i: (0, 0))],
        out_specs=pl.BlockSpec((1, 1, nq, QB, QB), lambda b, i: (b, i, 0, 0, 0)),
        out_shape=jax.ShapeDtypeStruct((batch, nq, nq, QB, QB), BF16),
        scratch_shapes=[pltpu.VMEM((seq, LANES), BF16),
                        pltpu.VMEM((IDX_HEADS, QB, LANES), BF16),
                        pltpu.VMEM((nq, QB, QB), jnp.int32),
                        pltpu.VMEM((1, QB), jnp.int32),
                        pltpu.VMEM((1, QB), F32)],
        compiler_params=_cparams("arbitrary", "arbitrary"),
        name="dsa_select",
    )(proj, proj, proj, cos_i, sin_i, proj, proj, cos_i, sin_i, ik_gains)


def _dsa_kernel(tab_ref, bk_ref, q_ref, k_ref, v_ref, sel_ref, o_ref, kb_scr, vb_scr, bt_scr):
    b = pl.program_id(0)
    i = pl.program_id(1)

    n_pairs = (C_HEADS + 1) // 2
    heads_of = [min(2, C_HEADS - 2 * pt) for pt in range(n_pairs)]

    far = N_BUCKETS - 1

    @pl.when((b == 0) & (i == 0))
    def _():
        for pt in range(n_pairs):
            for hh in range(heads_of[pt]):
                col = A_HEADS + 2 * pt + hh
                for dd in range(2):
                    bt_scr[2 * pt + dd, hh * QB:(hh + 1) * QB, :] = (
                        _bias_from_buckets(bk_ref[dd], tab_ref, col) - tab_ref[far, col])

    @pl.when(i == 0)
    def _():
        kb_scr[...] = k_ref[0].astype(BF16)
        vb_scr[...] = v_ref[0].astype(BF16)

    lane = lax.broadcasted_iota(jnp.int32, (QB, LANES), 1)
    low = lane < HEAD_DIM

    def q_block(qb):
        ext = (qb + 1) * QB
        for pt in range(n_pairs):
            heads = heads_of[pt]
            rows = heads * QB
            cols = slice(pt * LANES, (pt + 1) * LANES)
            q = q_ref[0, :, cols] * (HEAD_DIM ** -0.5)
            halves = [jnp.where(low, q, 0.0), jnp.where(low, 0.0, q)][:heads]
            q2 = jnp.concatenate(halves, axis=0).astype(BF16)
            logit = _dot_t(q2, kb_scr[0:ext, cols])
            pieces = []
            for c in range(qb + 1):
                t = logit[:, c * QB:(c + 1) * QB] + jnp.concatenate([sel_ref[0, 0, c].astype(F32)] * heads, axis=0)
                if qb - c < 2:
                    t = t + bt_scr[2 * pt + (qb - c), 0:rows, :]
                pieces.append(t)
            logit = pieces[0] if qb == 0 else jnp.concatenate(pieces, axis=1)
            m = jnp.max(logit, axis=-1, keepdims=True)
            e = jnp.exp(logit - m)
            s = jnp.sum(e, axis=-1, keepdims=True)
            out = _dot(e.astype(BF16), vb_scr[0:ext, cols]) / s
            upper = out[QB:] if heads == 2 else 0.0
            o_ref[0, :, cols] = jnp.where(low, out[:QB], upper).astype(BF16)

    for qb in range(sel_ref.shape[2]):
        pl.when(i == qb)(functools.partial(q_block, qb))


def _dsa_attention(proj, sel, rel_bias, *, batch, seq):
    nq = seq // QB
    qi = np.arange(QB)[:, None]
    kj = np.arange(QB)[None, :]
    bk = np.stack([_np_bucket(qi - kj), _np_bucket(QB + qi - kj)]).astype(np.int32)
    assert _np_bucket(np.array(QB + 1)) == N_BUCKETS - 1
    width = ((C_HEADS + 1) // 2) * LANES
    return pl.pallas_call(
        _dsa_kernel,
        grid=(batch, nq),
        in_specs=[pl.BlockSpec(memory_space=pltpu.SMEM),
                  pl.BlockSpec((2, QB, QB), lambda b, i: (0, 0, 0)),
                  pl.BlockSpec((1, QB, width), lambda b, i: (b, i, T_QC * LANES // width)),
                  pl.BlockSpec((1, seq, width), lambda b, i: (b, 0, T_KC * LANES // width)),
                  pl.BlockSpec((1, seq, width), lambda b, i: (b, 0, T_VC * LANES // width)),
                  pl.BlockSpec((1, 1, nq, QB, QB), lambda b, i: (b, i, 0, 0, 0))],
        out_specs=pl.BlockSpec((1, QB, width), lambda b, i: (b, i, 0)),
        out_shape=jax.ShapeDtypeStruct((batch, seq, width), BF16),
        scratch_shapes=[pltpu.VMEM((seq, width), BF16),
                        pltpu.VMEM((seq, width), BF16),
                        pltpu.VMEM((2 * ((C_HEADS + 1) // 2), 2 * QB, QB), F32)],
        compiler_params=_cparams("arbitrary", "arbitrary"),
        name="dsa_attn",
    )(rel_bias, jnp.asarray(bk), proj, proj, proj, sel)


def _out_kernel(oa_ref, oc_ref, ob_ref, wa_ref, wc_ref, wb_ref, h_ref, mod_ref, o_ref):
    acc = _dot(oa_ref[...], wa_ref[...]) + _dot(oc_ref[...], wc_ref[...]) + _dot(ob_ref[...], wb_ref[...])
    o_ref[...] = h_ref[...] + mod_ref[0, 5:6, :] * acc


def _out_proj(oa, oc, ob, wa, wc, wb, h, mod, *, seq):
    M, D = h.shape
    tm = 512
    per_b = seq // tm

    def rows(a):
        return pl.BlockSpec((tm, a.shape[1]), lambda i: (i, 0))

    def full(a):
        return pl.BlockSpec(a.shape, lambda i: (0, 0))

    return pl.pallas_call(
        _out_kernel,
        grid=(M // tm,),
        in_specs=[rows(oa), rows(oc), rows(ob), full(wa), full(wc), full(wb), rows(h),
                  pl.BlockSpec((1, N_MOD, D), lambda i: (i // per_b, 0, 0))],
        out_specs=pl.BlockSpec((tm, D), lambda i: (i, 0)),
        out_shape=jax.ShapeDtypeStruct((M, D), F32),
        compiler_params=_cparams("arbitrary"),
        name="out_proj",
    )(oa, oc, ob, wa, wc, wb, h, mod)


def _swap_halves(w, rope):
    half = rope // 2
    return jnp.concatenate([-w[..., half:rope], w[..., :half], jnp.zeros_like(w[..., rope:])], axis=-1)


def _pad_cols(w, n):
    return jnp.concatenate([w, jnp.zeros(w.shape[:-1] + (n - w.shape[-1],), w.dtype)], axis=-1)


def _layout_w_in(w):
    D = w.shape[0]
    sizes = (A_HEADS * HEAD_DIM,) * 3 + (MLA_Q_RANK, MLA_KV_RANK, MLA_ROPE) + (C_HEADS * HEAD_DIM,) * 3 + (
        IDX_HEADS * IDX_DIM, IDX_DIM, IDX_HEADS)
    offs = np.cumsum((0,) + sizes)
    qa, ka, va, cq, ckv, kr, qc, kc, vc, iq, ik, iw = [w[:, offs[n]:offs[n + 1]] for n in range(len(sizes))]
    c_w = ((C_HEADS + 1) // 2) * LANES
    z64 = jnp.zeros((D, MLA_NOPE), w.dtype)
    iq_h = iq.reshape(D, IDX_HEADS, IDX_DIM)
    iq_sw = _swap_halves(iq_h, IDX_ROPE).reshape(D, IDX_HEADS * IDX_DIM)
    ik_sw = _swap_halves(ik, IDX_ROPE)
    cols = [qa, ka, va, _pad_cols(qc, c_w), _pad_cols(kc, c_w), _pad_cols(vc, c_w), cq, iq, iq_sw, ckv,
            _pad_cols(jnp.concatenate([z64, kr], axis=-1), LANES),
            _pad_cols(jnp.concatenate([z64, _swap_halves(kr, MLA_ROPE)], axis=-1), LANES),
            jnp.concatenate([ik, ik], axis=-1), jnp.concatenate([ik_sw, ik_sw], axis=-1), _pad_cols(iw, LANES)]
    out = jnp.concatenate(cols, axis=-1)
    assert out.shape[1] == N_TILES * LANES
    return out.astype(BF16)


def _layout_mla(w_uq, w_ukv):
    rq = w_uq.reshape(MLA_Q_RANK, B_HEADS, MLA_NOPE + MLA_ROPE)
    nope, rope = rq[..., :MLA_NOPE], rq[..., MLA_NOPE:]
    wq = _pad_cols(jnp.concatenate([nope, rope], axis=-1), LANES)
    wqs = _pad_cols(jnp.concatenate([jnp.zeros_like(nope), _swap_halves(rope, MLA_ROPE)], axis=-1), LANES)
    rkv = w_ukv.reshape(MLA_KV_RANK, B_HEADS, MLA_NOPE + HEAD_DIM)
    wk = _pad_cols(rkv[..., :MLA_NOPE], LANES)
    wv = _pad_cols(rkv[..., MLA_NOPE:], LANES)
    flat = lambda a, r: a.reshape(r, B_HEADS * LANES).astype(BF16)
    return flat(wq, MLA_Q_RANK), flat(wqs, MLA_Q_RANK), flat(wk, MLA_KV_RANK), flat(wv, MLA_KV_RANK)


def _layout_w_out(w):
    D = w.shape[1]
    a, b_, c_ = A_HEADS * HEAD_DIM, B_HEADS * HEAD_DIM, C_HEADS * HEAD_DIM
    wa = w[:a]
    wb = w[a:a + b_].reshape(B_HEADS, HEAD_DIM, D)
    wb = jnp.concatenate([wb, jnp.zeros_like(wb)], axis=1).reshape(B_HEADS * LANES, D)
    wc = w[a + b_:a + b_ + c_]
    wc = jnp.concatenate([wc, jnp.zeros((((C_HEADS + 1) // 2) * LANES - c_, D), w.dtype)], axis=0)
    return wa.astype(BF16), wc.astype(BF16), wb.astype(BF16)


def kernel(x, c, positions, rel_bias, ada_w, ada_b, norm_ffn1, ffn1_w_gu, ffn1_w_down, norm_mix, w_in, mla_q_norm, mla_w_uq, mla_kv_norm, mla_w_ukv, idx_k_norm, w_out, norm_ffn2, ffn2_w_gu, ffn2_w_down, final_norm):
    B, T, D = x.shape
    L = ada_w.shape[0]
    assert T % (max(DILATIONS) * BLK) == 0 and T % 512 == 0 and D % LANES == 0

    mods = _modulation(c, ada_w, ada_b).reshape(L, B, N_MOD, D)
    cos_i, sin_i, cos_m, sin_m = _rope_tables(positions)
    final_gain = final_norm.reshape(1, D)
    h = x.reshape(B * T, D)
    for l in range(L):
        mod = mods[l]
        h = _ffn(h, mod, norm_ffn1[l].reshape(1, D), ffn1_w_gu[l].astype(BF16), ffn1_w_down[l].astype(BF16),
                 final_gain, ks=0, final=False, seq=T)

        proj = _in_proj(h, mod, norm_mix[l].reshape(1, D), _layout_w_in(w_in[l]), seq=T)
        proj = proj.reshape(B, T, N_TILES * LANES)
        oa = _dilated_attention(proj, rel_bias, batch=B, seq=T)
        wq, wqs, wk, wv = _layout_mla(mla_w_uq[l], mla_w_ukv[l])
        ob = _mla_attention(proj, cos_m, sin_m, mla_q_norm[l].reshape(1, -1), mla_kv_norm[l].reshape(1, -1),
                            wq, wqs, wk, wv, batch=B, seq=T)
        g = idx_k_norm[l]
        g_sw = jnp.concatenate([g[IDX_ROPE // 2:IDX_ROPE], g[:IDX_ROPE // 2], jnp.zeros_like(g[IDX_ROPE:])])
        ik_gains = jnp.stack([jnp.concatenate([g, g]), jnp.concatenate([g_sw, g_sw])])
        sel = _dsa_select(proj, cos_i, sin_i, ik_gains, batch=B, seq=T)
        oc = _dsa_attention(proj, sel, rel_bias, batch=B, seq=T)
        wa, wc, wb = _layout_w_out(w_out[l])
        h = _out_proj(oa.reshape(B * T, -1), oc.reshape(B * T, -1), ob.reshape(B * T, -1), wa, wc, wb, h, mod, seq=T)

        h = _ffn(h, mod, norm_ffn2[l].reshape(1, D), ffn2_w_gu[l].astype(BF16), ffn2_w_down[l].astype(BF16),
                 final_gain, ks=6, final=(l == L - 1), seq=T)
    return h.reshape(B, T, D)
```

```python
import functools
import math

import numpy as np
import jax
import jax.numpy as jnp
from jax import lax
from jax.experimental import pallas as pl
from jax.experimental.pallas import tpu as pltpu

F32 = jnp.float32
BF16 = jnp.bfloat16

HEAD_DIM = 64
A_HEADS, B_HEADS, C_HEADS = 6, 5, 5
MLA_Q_RANK, MLA_KV_RANK, MLA_NOPE, MLA_ROPE = 256, 128, 64, 32
IDX_HEADS, IDX_DIM, IDX_ROPE = 8, 64, 32
TOPK = 256
N_BUCKETS, MAX_DISTANCE = 32, 128
ROPE_THETA = 10000.0
RMS_EPS = 1e-6
N_MOD = 9
NEG = -1e30
DILATIONS = (1, 4, 16)

LANES = 128
BLK = 128
QB = 256
VMEM_LIMIT = 56 * 1024 * 1024

T_QA, T_KA, T_VA = 0, 3, 6
T_QC, T_KC, T_VC = 9, 12, 15
T_CQ, T_IQ, T_IQS = 18, 20, 24
T_CKV, T_KR, T_KRS, T_IK, T_IKS, T_IW = 28, 29, 30, 31, 32, 33
N_TILES = 34

KEY_FLIP = 0x7FFFFFFF
INT_MIN = -(2 ** 31)


def _cparams(*sem):
    return pltpu.CompilerParams(dimension_semantics=sem, vmem_limit_bytes=VMEM_LIMIT)


def _rms(x, g):
    return x * lax.rsqrt(jnp.mean(x * x, axis=-1, keepdims=True) + RMS_EPS) * g


def _dot(a, b):
    return jnp.dot(a, b, preferred_element_type=F32)


def _dot_t(a, b):
    return lax.dot_general(a, b, (((1,), (1,)), ((), ())), preferred_element_type=F32)


def _np_bucket(dist):
    n = np.maximum(dist, 0)
    max_exact = N_BUCKETS // 2
    nf = np.maximum(n, 1).astype(np.float32)
    large = max_exact + (np.log(nf / np.float32(max_exact)) / np.float32(math.log(MAX_DISTANCE / max_exact))
                         * np.float32(N_BUCKETS - max_exact)).astype(np.int32)
    large = np.minimum(large, N_BUCKETS - 1)
    return np.where(n < max_exact, n, large).astype(np.int32)


def _bias_from_buckets(bucket, tab_ref, col):
    acc = jnp.full(bucket.shape, NEG, F32)
    for b in range(N_BUCKETS):
        acc = jnp.where(bucket == b, tab_ref[b, col], acc)
    return acc


def _mod_kernel(c_ref, w_ref, b_ref, o_ref):
    c = c_ref[...]
    cond = c / (1.0 + jnp.exp(-c))
    w = w_ref[0]
    c_hi = cond.astype(BF16)
    c_lo = (cond - c_hi.astype(F32)).astype(BF16)
    w_hi = w.astype(BF16)
    w_lo = (w - w_hi.astype(F32)).astype(BF16)
    o_ref[0] = _dot(c_hi, w_hi) + _dot(c_hi, w_lo) + _dot(c_lo, w_hi) + b_ref[0]


def _modulation(c, ada_w, ada_b):
    L, D, N = ada_w.shape
    B = c.shape[0]
    tn = 1024
    return pl.pallas_call(
        _mod_kernel,
        grid=(L, N // tn),
        in_specs=[pl.BlockSpec((B, D), lambda l, j: (0, 0)),
                  pl.BlockSpec((1, D, tn), lambda l, j: (l, 0, j)),
                  pl.BlockSpec((1, 1, tn), lambda l, j: (l, 0, j))],
        out_specs=pl.BlockSpec((1, B, tn), lambda l, j: (l, 0, j)),
        out_shape=jax.ShapeDtypeStruct((L, B, N), F32),
        compiler_params=_cparams("arbitrary", "arbitrary"),
        name="adaln_mod",
    )(c, ada_w, ada_b.reshape(L, 1, N))


def _ffn_kernel(h_ref, mod_ref, g_ref, wgu_ref, wd_ref, fg_ref, o_ref, *, ks, final, tf):
    F = wd_ref.shape[0]
    x = h_ref[...]
    y = _rms(x, g_ref[...])
    n = (y * (1.0 + mod_ref[0, ks + 1:ks + 2, :]) + mod_ref[0, ks:ks + 1, :]).astype(BF16)
    acc = jnp.zeros(x.shape, F32)
    for f0 in range(0, F, tf):
        g = _dot(n, wgu_ref[:, f0:f0 + tf])
        u = _dot(n, wgu_ref[:, F + f0:F + f0 + tf])
        a = (g / (1.0 + jnp.exp(-g))) * u
        acc = acc + _dot(a.astype(BF16), wd_ref[f0:f0 + tf, :])
    out = x + (0.5 * mod_ref[0, ks + 2:ks + 3, :]) * acc
    if final:
        out = _rms(out, fg_ref[...])
    o_ref[...] = out


def _resident(a):
    return pl.BlockSpec(a.shape, lambda *_: (0,) * a.ndim, pipeline_mode=pl.Buffered(1))


def _ffn(h, mod, gain, w_gu, w_down, final_gain, *, ks, final, seq):
    M, D = h.shape
    F = w_down.shape[0]
    tm = 512
    tf = 256
    assert F % tf == 0
    per_b = seq // tm
    return pl.pallas_call(
        functools.partial(_ffn_kernel, ks=ks, final=final, tf=tf),
        grid=(M // tm,),
        in_specs=[pl.BlockSpec((tm, D), lambda i: (i, 0)),
                  pl.BlockSpec((1, N_MOD, D), lambda i: (i // per_b, 0, 0)),
                  pl.BlockSpec((1, D), lambda i: (0, 0)),
                  _resident(w_gu), _resident(w_down),
                  pl.BlockSpec((1, D), lambda i: (0, 0))],
        out_specs=pl.BlockSpec((tm, D), lambda i: (i, 0)),
        out_shape=jax.ShapeDtypeStruct((M, D), F32),
        compiler_params=_cparams("arbitrary"),
        name="ffn",
    )(h, mod, gain, w_gu, w_down, final_gain)


def _win_kernel(h_ref, mod_ref, g_ref, w_ref, o_ref, *, tn):
    y = _rms(h_ref[...], g_ref[...])
    n = (y * (1.0 + mod_ref[0, 4:5, :]) + mod_ref[0, 3:4, :]).astype(BF16)
    for c0 in range(0, w_ref.shape[1], tn):
        o_ref[:, c0:c0 + tn] = _dot(n, w_ref[:, c0:c0 + tn])


def _in_proj(h, mod, gain, w, *, seq):
    M, D = h.shape
    N = w.shape[1]
    tm, tn = 512, 256
    assert N % tn == 0
    per_b = seq // tm
    return pl.pallas_call(
        functools.partial(_win_kernel, tn=tn),
        grid=(M // tm,),
        in_specs=[pl.BlockSpec((tm, D), lambda i: (i, 0)),
                  pl.BlockSpec((1, N_MOD, D), lambda i: (i // per_b, 0, 0)),
                  pl.BlockSpec((1, D), lambda i: (0, 0)),
                  _resident(w)],
        out_specs=pl.BlockSpec((tm, N), lambda i: (i, 0)),
        out_shape=jax.ShapeDtypeStruct((M, N), F32),
        compiler_params=_cparams("arbitrary"),
        name="in_proj",
    )(h, mod, gain, w)


def _rope_kernel(pos_ref, inv_ref, c_ref, s_ref):
    ang = pos_ref[0].astype(F32) * inv_ref[...]
    c_ref[0] = jnp.cos(ang)
    s_ref[0] = jnp.sin(ang)


def _rope_tables(positions):
    B, T = positions.shape
    half = IDX_ROPE // 2
    assert MLA_ROPE == IDX_ROPE and MLA_NOPE == IDX_DIM
    inv = (ROPE_THETA ** (-np.arange(half, dtype=np.float32) / half)).astype(np.float32)
    zeros32 = np.zeros(IDX_DIM - IDX_ROPE, np.float32)
    inv_row = jnp.asarray(np.concatenate([inv, inv, zeros32, inv, inv, zeros32])[None, :])
    tab = jax.ShapeDtypeStruct((B, T, LANES), F32)
    spec = pl.BlockSpec((1, T, LANES), lambda b: (b, 0, 0))
    return pl.pallas_call(
        _rope_kernel,
        grid=(B,),
        in_specs=[pl.BlockSpec((1, T, 1), lambda b: (b, 0, 0)),
                  pl.BlockSpec((1, LANES), lambda b: (0, 0))],
        out_specs=[spec, spec],
        out_shape=[tab, tab],
        compiler_params=_cparams("arbitrary"),
        name="rope_tables",
    )(positions.reshape(B, T, 1), inv_row)


def _dil_kernel(tab_ref, bk_ref, q_ref, k_ref, v_ref, o_ref, bt_scr, o_scr, m_scr, s_scr, *, seq):
    b = pl.program_id(0)
    pair = pl.program_id(1)

    @pl.when(b == 0)
    def _():
        for p in range(3):
            for hh in range(2):
                bt_scr[pair, p, hh] = _bias_from_buckets(bk_ref[p], tab_ref, 2 * pair + hh)

    nblk = seq // BLK
    low = lax.broadcasted_iota(jnp.int32, (nblk, BLK, LANES), 2) < HEAD_DIM
    neg_half = jnp.full((BLK, BLK), NEG, F32)

    for p, d in enumerate(DILATIONS):
        nb = seq // (d * BLK)
        starts = [r + d * BLK * n for n in range(nb) for r in range(d)]

        def rows_of(s0, d=d):
            return pl.ds(s0, BLK) if d == 1 else pl.ds(s0, BLK, stride=d)

        def regroup(ref):
            return jnp.stack([ref[0, rows_of(s0), :] for s0 in starts])

        def band(z):
            if nb == 1:
                return z
            prev = jnp.concatenate([jnp.zeros((d,) + z.shape[1:], z.dtype), z[:-d]], axis=0)
            return jnp.concatenate([prev, z], axis=1)

        q = regroup(q_ref) * (HEAD_DIM ** -0.5)
        kband = band(regroup(k_ref).astype(BF16))
        vband = band(regroup(v_ref).astype(BF16))
        res = []
        for hh in range(2):
            qh = (jnp.where(low, q, 0.0) if hh == 0 else jnp.where(low, 0.0, q)).astype(BF16)
            logit = lax.dot_general(qh, kband, (((2,), (2,)), ((0,), (0,))), preferred_element_type=F32)
            bias = bt_scr[pair, p, hh]
            if nb == 1:
                logit = logit + bias[:, BLK:][None]
            else:
                first = jnp.concatenate([neg_half, bias[:, BLK:]], axis=1)
                logit = jnp.concatenate([logit[:d] + first[None], logit[d:] + bias[None]], axis=0)
            m = jnp.max(logit, axis=-1, keepdims=True)
            e = jnp.exp(logit - m)
            s = jnp.sum(e, axis=-1, keepdims=True)
            o = lax.dot_general(e.astype(BF16), vband, (((2,), (1,)), ((0,), (0,))), preferred_element_type=F32)
            res.append((m, s, o))
        o_all = jnp.where(low, res[0][2], res[1][2])
        m_all = jnp.where(low, res[0][0], res[1][0])
        s_all = jnp.where(low, res[0][1], res[1][1])
        for bi, s0 in enumerate(starts):
            o_scr[p, rows_of(s0), :] = o_all[bi]
            m_scr[p, rows_of(s0), :] = m_all[bi]
            s_scr[p, rows_of(s0), :] = s_all[bi]

    m_all = jnp.maximum(jnp.maximum(m_scr[0], m_scr[1]), m_scr[2])
    num = jnp.zeros((seq, LANES), F32)
    den = jnp.zeros((seq, LANES), F32)
    for p in range(3):
        w = jnp.exp(m_scr[p] - m_all)
        num = num + w * o_scr[p]
        den = den + w * s_scr[p]
    o_ref[0] = (num / den).astype(BF16)


def _dilated_attention(proj, rel_bias, *, batch, seq):
    qi = np.arange(BLK)[:, None]
    kj = np.arange(2 * BLK)[None, :]
    step = BLK + qi - kj
    in_band = (step >= 0) & (step <= BLK)
    bk = np.stack([np.where(in_band, _np_bucket(step * d), -1) for d in DILATIONS]).astype(np.int32)
    pairs = A_HEADS // 2

    def col(t0):
        return pl.BlockSpec((1, seq, LANES), lambda b, p: (b, 0, t0 + p))

    return pl.pallas_call(
        functools.partial(_dil_kernel, seq=seq),
        grid=(batch, pairs),
        in_specs=[pl.BlockSpec(memory_space=pltpu.SMEM),
                  pl.BlockSpec((3, BLK, 2 * BLK), lambda b, p: (0, 0, 0)),
                  col(T_QA), col(T_KA), col(T_VA)],
        out_specs=pl.BlockSpec((1, seq, LANES), lambda b, p: (b, 0, p)),
        out_shape=jax.ShapeDtypeStruct((batch, seq, pairs * LANES), BF16),
        scratch_shapes=[pltpu.VMEM((pairs, 3, 2, BLK, 2 * BLK), F32),
                        pltpu.VMEM((3, seq, LANES), F32),
                        pltpu.VMEM((3, seq, LANES), F32),
                        pltpu.VMEM((3, seq, LANES), F32)],
        compiler_params=_cparams("arbitrary", "arbitrary"),
        name="dilated_attn",
    )(rel_bias, jnp.asarray(bk), proj, proj, proj)


def _mla_kernel(cq_ref, ckv_ref, kr_ref, krs_ref, c_ref, s_ref, qn_ref, kvn_ref,
                wq_ref, wqs_ref, wk_ref, wv_ref, o_ref, q_scr, k_scr, v_scr, *, seq):
    rc = 512
    for c0 in range(0, seq, rc):
        rows = slice(c0, c0 + rc)
        lane_r = lax.broadcasted_iota(jnp.int32, (rc, LANES), 1)
        rot = (lane_r >= MLA_NOPE) & (lane_r < MLA_NOPE + MLA_ROPE)
        cos = jnp.where(rot, c_ref[0, rows, :], 1.0)
        sin = jnp.where(rot, s_ref[0, rows, :], 0.0)
        nq = _rms(cq_ref[0, rows, :], qn_ref[...]).astype(BF16)
        nkv = _rms(ckv_ref[0, rows, :], kvn_ref[...]).astype(BF16)
        kr = kr_ref[0, rows, :] * cos + krs_ref[0, rows, :] * sin
        for h in range(B_HEADS):
            cols = slice(h * LANES, (h + 1) * LANES)
            q_scr[h, rows, :] = (_dot(nq, wq_ref[:, cols]) * cos + _dot(nq, wqs_ref[:, cols]) * sin).astype(BF16)
            k_scr[h, rows, :] = (_dot(nkv, wk_ref[:, cols]) + kr).astype(BF16)
            v = _dot(nkv, wv_ref[:, cols])
            lane_c = lax.broadcasted_iota(jnp.int32, v.shape, 1)
            v_scr[h, rows, :] = jnp.where(lane_c < HEAD_DIM, v, 1.0).astype(BF16)

    exp2_scale = (MLA_NOPE + MLA_ROPE) ** -0.5 * math.log2(math.e)
    causal = lax.broadcasted_iota(jnp.int32, (QB, QB), 1) <= lax.broadcasted_iota(jnp.int32, (QB, QB), 0)
    low = lax.broadcasted_iota(jnp.int32, (QB, LANES), 1) < HEAD_DIM
    for qb in range(seq // QB):
        qs, ext = qb * QB, (qb + 1) * QB
        outs = []
        for h in range(B_HEADS):
            logit = _dot_t(q_scr[h, qs:qs + QB, :], k_scr[h, 0:ext, :])
            diag = jnp.where(causal, logit[:, qs:], NEG)
            logit = diag if qb == 0 else jnp.concatenate([logit[:, :qs], diag], axis=1)
            m = jnp.max(logit, axis=-1, keepdims=True)
            e = jnp.exp2((logit - m) * exp2_scale).astype(BF16)
            pv = _dot(e, v_scr[h, 0:ext, :])
            outs.append(jnp.where(low, pv / pltpu.roll(pv, HEAD_DIM, 1), 0.0))
        for pt in range((B_HEADS + 1) // 2):
            pair = outs[2 * pt]
            if 2 * pt + 1 < B_HEADS:
                pair = pair + pltpu.roll(outs[2 * pt + 1], HEAD_DIM, 1)
            o_ref[0, qs:qs + QB, pt * LANES:(pt + 1) * LANES] = pair.astype(BF16)


def _mla_attention(proj, cos_m, sin_m, q_norm, kv_norm, wq, wqs, wk, wv, *, batch, seq):
    def col(t0, n):
        return pl.BlockSpec((1, seq, n * LANES), lambda b: (b, 0, t0 // n))

    def full(a):
        return pl.BlockSpec(a.shape, lambda b: (0,) * a.ndim)

    tab = pl.BlockSpec((1, seq, LANES), lambda b: (b, 0, 0))
    width = ((B_HEADS + 1) // 2) * LANES
    return pl.pallas_call(
        functools.partial(_mla_kernel, seq=seq),
        grid=(batch,),
        in_specs=[col(T_CQ, 2), col(T_CKV, 1), col(T_KR, 1), col(T_KRS, 1), tab, tab,
                  full(q_norm), full(kv_norm), full(wq), full(wqs), full(wk), full(wv)],
        out_specs=pl.BlockSpec((1, seq, width), lambda b: (b, 0, 0)),
        out_shape=jax.ShapeDtypeStruct((batch, seq, width), BF16),
        scratch_shapes=[pltpu.VMEM((B_HEADS, seq, LANES), BF16)] * 3,
        compiler_params=_cparams("arbitrary"),
        name="mla_attn",
    )(proj, proj, proj, proj, cos_m, sin_m, q_norm, kv_norm, wq, wqs, wk, wv)


def _sel_kernel(iq_ref, iqs_ref, iw_ref, cq_ref, sq_ref, ik_ref, iks_ref, ck_ref, sk_ref, g_ref,
                o_ref, ik_scr, iqm_scr, key_scr, hb_scr, *, seq):
    i = pl.program_id(1)
    qs = i * QB
    nk = seq // QB

    @pl.when(i == 0)
    def _():
        x = ik_ref[0]
        r = lax.rsqrt(jnp.mean(x * x, axis=-1, keepdims=True) + RMS_EPS)
        y = x * r * g_ref[0:1, :]
        ys = iks_ref[0] * r * g_ref[1:2, :]
        ik_scr[...] = (y * ck_ref[0] + ys * sk_ref[0]).astype(BF16)

    lane = lax.broadcasted_iota(jnp.int32, (QB, LANES), 1)
    low = lane < IDX_DIM
    cos = cq_ref[0]
    sin = sq_ref[0]
    for t in range(IDX_HEADS // 2):
        cols = slice(t * LANES, (t + 1) * LANES)
        rot = iq_ref[0, :, cols] * cos + iqs_ref[0, :, cols] * sin
        iqm_scr[2 * t] = jnp.where(low, rot, 0.0).astype(BF16)
        iqm_scr[2 * t + 1] = jnp.where(low, 0.0, rot).astype(BF16)
    w_t = (iw_ref[0] * (IDX_HEADS ** -0.5 * IDX_DIM ** -0.5)).T

    krow = lax.broadcasted_iota(jnp.int32, (QB, QB), 0)
    qcol = lax.broadcasted_iota(jnp.int32, (QB, QB), 1)
    target = jnp.minimum(qs + lax.broadcasted_iota(jnp.int32, (1, QB), 1) + 1, TOPK).astype(F32)
    min_normal = float(np.finfo(np.float32).tiny)
    min_normal_bits = int(np.array(min_normal, np.float32).view(np.int32))
    hi_mask = -(1 << 16)
    one16, zero16 = jnp.ones((), BF16), jnp.zeros((), BF16)
    sub16 = QB // 16

    def block(n_tiles):
        iqm = iqm_scr[...].reshape(IDX_HEADS * QB, LANES)
        for kb in range(n_tiles):
            rel = _dot_t(ik_scr[kb * QB:(kb + 1) * QB, :], iqm)
            sc = jnp.zeros((QB, QB), F32)
            for h in range(IDX_HEADS):
                sc = sc + jnp.maximum(rel[:, h * QB:(h + 1) * QB], 0.0) * w_t[h:h + 1, :]
            sc = jnp.where(jnp.abs(sc) < min_normal, 0.0, sc)
            if kb == n_tiles - 1:
                sc = jnp.where(krow <= qcol, sc, -jnp.inf)
            bits = lax.bitcast_convert_type(sc, jnp.int32)
            key_scr[kb] = jnp.where(bits < 0, bits ^ KEY_FLIP, bits)
            hb_scr[kb] = lax.bitcast_convert_type(bits & hi_mask, F32).astype(BF16)

        def count_hi(cand16):
            acc = jnp.zeros((16, QB), BF16)
            for c in range(n_tiles):
                x = hb_scr[c].reshape(sub16, 16, QB)
                for r in range(sub16):
                    acc = acc + jnp.where(x[r] >= cand16, one16, zero16)
            return jnp.sum(acc.astype(F32), axis=0, keepdims=True)

        def count(pred):
            acc = jnp.zeros((8, QB), F32)
            for c in range(n_tiles):
                acc = acc + jnp.sum(pred(key_scr[c]).astype(F32).reshape(QB // 8, 8, QB), axis=0)
            return jnp.sum(acc, axis=0, keepdims=True)

        def hi_step(bi, prefix):
            cand_u = prefix | lax.shift_left(jnp.int32(1), 31 - bi)
            k = cand_u ^ INT_MIN
            cbits = jnp.where(k < 0, k ^ KEY_FLIP, k) & hi_mask
            cbits = jnp.where((cbits > 0) & (cbits < min_normal_bits), min_normal_bits, cbits)
            cand = lax.bitcast_convert_type(cbits, F32)
            cnt = count_hi(jnp.broadcast_to(cand, (16, QB)).astype(BF16))
            return jnp.where(cnt >= target, cand_u, prefix)

        def lo_step(bi, prefix):
            cand_u = prefix | lax.shift_left(jnp.int32(1), 15 - bi)
            cand = cand_u ^ INT_MIN
            cnt = count(lambda k: k >= cand)
            return jnp.where(cnt >= target, cand_u, prefix)

        prefix = lax.fori_loop(0, 16, hi_step, jnp.zeros((1, QB), jnp.int32))
        thr = lax.fori_loop(0, 16, lo_step, prefix) ^ INT_MIN
        need = target - count(lambda k: k > thr)

        lower = (qcol <= krow).astype(BF16)
        carry = jnp.zeros((1, QB), F32)
        for c in range(n_tiles):
            kc = key_scr[c]
            eq = kc == thr
            rank = carry + _dot(lower, eq.astype(BF16))
            sel = (kc > thr) | (eq & (rank <= need))
            o_ref[0, 0, c] = jnp.where(sel, 0.0, NEG).T.astype(BF16)
            carry = rank[QB - 1:QB, :]
        for c in range(n_tiles, nk):
            o_ref[0, 0, c] = jnp.full((QB, QB), NEG, BF16)

    for n_tiles in range(1, nk + 1):
        pl.when(i == n_tiles - 1)(functools.partial(block, n_tiles))


def _dsa_select(proj, cos_i, sin_i, ik_gains, *, batch, seq):
    nq = seq // QB

    def qcol(t0, n):
        return pl.BlockSpec((1, QB, n * LANES), lambda b, i: (b, i, t0 // n))

    def kcol(t0):
        return pl.BlockSpec((1, seq, LANES), lambda b, i: (b, 0, t0))

    qtab = pl.BlockSpec((1, QB, LANES), lambda b, i: (b, i, 0))
    ktab = pl.BlockSpec((1, seq, LANES), lambda b, i: (b, 0, 0))
    return pl.pallas_call(
        functools.partial(_sel_kernel, seq=seq),
        grid=(batch, nq),
        in_specs=[qcol(T_IQ, 4), qcol(T_IQS, 4), qcol(T_IW, 1), qtab, qtab,
                  kcol(T_IK), kcol(T_IKS), ktab, ktab,
                  pl.BlockSpec((2, LANES), lambda b, i: (0, 0))],
        out_specs=pl.BlockSpec((1, 1, nq, QB, QB), lambda b, i: (b, i, 0, 0, 0)),
        out_shape=jax.ShapeDtypeStruct((batch, nq, nq, QB, QB), BF16),
        scratch_shapes=[pltpu.VMEM((seq, LANES), BF16),
                        pltpu.VMEM((IDX_HEADS, QB, LANES), BF16),
                        pltpu.VMEM((nq, QB, QB), jnp.int32),
                        pltpu.VMEM((nq, QB, QB), BF16)],
        compiler_params=_cparams("arbitrary", "arbitrary"),
        name="dsa_select",
    )(proj, proj, proj, cos_i, sin_i, proj, proj, cos_i, sin_i, ik_gains)


def _dsa_kernel(tab_ref, bk_ref, q_ref, k_ref, v_ref, sel_ref, o_ref, kb_scr, vb_scr, bt_scr):
    b = pl.program_id(0)
    i = pl.program_id(1)

    n_pairs = (C_HEADS + 1) // 2
    heads_of = [min(2, C_HEADS - 2 * pt) for pt in range(n_pairs)]

    far = N_BUCKETS - 1

    @pl.when((b == 0) & (i == 0))
    def _():
        for pt in range(n_pairs):
            for hh in range(heads_of[pt]):
                col = A_HEADS + 2 * pt + hh
                for dd in range(2):
                    bt_scr[2 * pt + dd, hh * QB:(hh + 1) * QB, :] = (
                        _bias_from_buckets(bk_ref[dd], tab_ref, col) - tab_ref[far, col])

    @pl.when(i == 0)
    def _():
        kb_scr[...] = k_ref[0].astype(BF16)
        vb_scr[...] = v_ref[0].astype(BF16)

    lane = lax.broadcasted_iota(jnp.int32, (QB, LANES), 1)
    low = lane < HEAD_DIM

    def q_block(qb):
        ext = (qb + 1) * QB
        for pt in range(n_pairs):
            heads = heads_of[pt]
            rows = heads * QB
            cols = slice(pt * LANES, (pt + 1) * LANES)
            q = q_ref[0, :, cols] * (HEAD_DIM ** -0.5)
            halves = [jnp.where(low, q, 0.0), jnp.where(low, 0.0, q)][:heads]
            q2 = jnp.concatenate(halves, axis=0).astype(BF16)
            logit = _dot_t(q2, kb_scr[0:ext, cols])
            pieces = []
            for c in range(qb + 1):
                t = logit[:, c * QB:(c + 1) * QB] + jnp.concatenate([sel_ref[0, 0, c].astype(F32)] * heads, axis=0)
                if qb - c < 2:
                    t = t + bt_scr[2 * pt + (qb - c), 0:rows, :]
                pieces.append(t)
            logit = pieces[0] if qb == 0 else jnp.concatenate(pieces, axis=1)
            m = jnp.max(logit, axis=-1, keepdims=True)
            e = jnp.exp(logit - m)
            s = jnp.sum(e, axis=-1, keepdims=True)
            out = _dot(e.astype(BF16), vb_scr[0:ext, cols]) / s
            upper = out[QB:] if heads == 2 else 0.0
            o_ref[0, :, cols] = jnp.where(low, out[:QB], upper).astype(BF16)

    for qb in range(sel_ref.shape[2]):
        pl.when(i == qb)(functools.partial(q_block, qb))


def _dsa_attention(proj, sel, rel_bias, *, batch, seq):
    nq = seq // QB
    qi = np.arange(QB)[:, None]
    kj = np.arange(QB)[None, :]
    bk = np.stack([_np_bucket(qi - kj), _np_bucket(QB + qi - kj)]).astype(np.int32)
    assert _np_bucket(np.array(QB + 1)) == N_BUCKETS - 1
    width = ((C_HEADS + 1) // 2) * LANES
    return pl.pallas_call(
        _dsa_kernel,
        grid=(batch, nq),
        in_specs=[pl.BlockSpec(memory_space=pltpu.SMEM),
                  pl.BlockSpec((2, QB, QB), lambda b, i: (0, 0, 0)),
                  pl.BlockSpec((1, QB, width), lambda b, i: (b, i, T_QC * LANES // width)),
                  pl.BlockSpec((1, seq, width), lambda b, i: (b, 0, T_KC * LANES // width)),
                  pl.BlockSpec((1, seq, width), lambda b, i: (b, 0, T_VC * LANES // width)),
                  pl.BlockSpec((1, 1, nq, QB, QB), lambda b, i: (b, i, 0, 0, 0))],
        out_specs=pl.BlockSpec((1, QB, width), lambda b, i: (b, i, 0)),
        out_shape=jax.ShapeDtypeStruct((batch, seq, width), BF16),
        scratch_shapes=[pltpu.VMEM((seq, width), BF16),
                        pltpu.VMEM((seq, width), BF16),
                        pltpu.VMEM((2 * ((C_HEADS + 1) // 2), 2 * QB, QB), F32)],
        compiler_params=_cparams("arbitrary", "arbitrary"),
        name="dsa_attn",
    )(rel_bias, jnp.asarray(bk), proj, proj, proj, sel)


def _out_kernel(oa_ref, oc_ref, ob_ref, wa_ref, wc_ref, wb_ref, h_ref, mod_ref, o_ref):
    acc = _dot(oa_ref[...], wa_ref[...]) + _dot(oc_ref[...], wc_ref[...]) + _dot(ob_ref[...], wb_ref[...])
    o_ref[...] = h_ref[...] + mod_ref[0, 5:6, :] * acc


def _out_proj(oa, oc, ob, wa, wc, wb, h, mod, *, seq):
    M, D = h.shape
    tm = 512
    per_b = seq // tm

    def rows(a):
        return pl.BlockSpec((tm, a.shape[1]), lambda i: (i, 0))

    def full(a):
        return pl.BlockSpec(a.shape, lambda i: (0, 0))

    return pl.pallas_call(
        _out_kernel,
        grid=(M // tm,),
        in_specs=[rows(oa), rows(oc), rows(ob), full(wa), full(wc), full(wb), rows(h),
                  pl.BlockSpec((1, N_MOD, D), lambda i: (i // per_b, 0, 0))],
        out_specs=pl.BlockSpec((tm, D), lambda i: (i, 0)),
        out_shape=jax.ShapeDtypeStruct((M, D), F32),
        compiler_params=_cparams("arbitrary"),
        name="out_proj",
    )(oa, oc, ob, wa, wc, wb, h, mod)


def _swap_halves(w, rope):
    half = rope // 2
    return jnp.concatenate([-w[..., half:rope], w[..., :half], jnp.zeros_like(w[..., rope:])], axis=-1)


def _pad_cols(w, n):
    return jnp.concatenate([w, jnp.zeros(w.shape[:-1] + (n - w.shape[-1],), w.dtype)], axis=-1)


def _layout_w_in(w):
    D = w.shape[0]
    sizes = (A_HEADS * HEAD_DIM,) * 3 + (MLA_Q_RANK, MLA_KV_RANK, MLA_ROPE) + (C_HEADS * HEAD_DIM,) * 3 + (
        IDX_HEADS * IDX_DIM, IDX_DIM, IDX_HEADS)
    offs = np.cumsum((0,) + sizes)
    qa, ka, va, cq, ckv, kr, qc, kc, vc, iq, ik, iw = [w[:, offs[n]:offs[n + 1]] for n in range(len(sizes))]
    c_w = ((C_HEADS + 1) // 2) * LANES
    z64 = jnp.zeros((D, MLA_NOPE), w.dtype)
    iq_h = iq.reshape(D, IDX_HEADS, IDX_DIM)
    iq_sw = _swap_halves(iq_h, IDX_ROPE).reshape(D, IDX_HEADS * IDX_DIM)
    ik_sw = _swap_halves(ik, IDX_ROPE)
    cols = [qa, ka, va, _pad_cols(qc, c_w), _pad_cols(kc, c_w), _pad_cols(vc, c_w), cq, iq, iq_sw, ckv,
            _pad_cols(jnp.concatenate([z64, kr], axis=-1), LANES),
            _pad_cols(jnp.concatenate([z64, _swap_halves(kr, MLA_ROPE)], axis=-1), LANES),
            jnp.concatenate([ik, ik], axis=-1), jnp.concatenate([ik_sw, ik_sw], axis=-1), _pad_cols(iw, LANES)]
    out = jnp.concatenate(cols, axis=-1)
    assert out.shape[1] == N_TILES * LANES
    return out.astype(BF16)


def _layout_mla(w_uq, w_ukv):
    rq = w_uq.reshape(MLA_Q_RANK, B_HEADS, MLA_NOPE + MLA_ROPE)
    nope, rope = rq[..., :MLA_NOPE], rq[..., MLA_NOPE:]
    wq = _pad_cols(jnp.concatenate([nope, rope], axis=-1), LANES)
    wqs = _pad_cols(jnp.concatenate([jnp.zeros_like(nope), _swap_halves(rope, MLA_ROPE)], axis=-1), LANES)
    rkv = w_ukv.reshape(MLA_KV_RANK, B_HEADS, MLA_NOPE + HEAD_DIM)
    wk = _pad_cols(rkv[..., :MLA_NOPE], LANES)
    wv = _pad_cols(rkv[..., MLA_NOPE:], LANES)
    flat = lambda a, r: a.reshape(r, B_HEADS * LANES).astype(BF16)
    return flat(wq, MLA_Q_RANK), flat(wqs, MLA_Q_RANK), flat(wk, MLA_KV_RANK), flat(wv, MLA_KV_RANK)


def _layout_w_out(w):
    D = w.shape[1]
    a, b_, c_ = A_HEADS * HEAD_DIM, B_HEADS * HEAD_DIM, C_HEADS * HEAD_DIM
    def pad_rows(x, heads):
        return jnp.concatenate([x, jnp.zeros((((heads + 1) // 2) * LANES - x.shape[0], D), w.dtype)], axis=0)

    wa = w[:a]
    wb = pad_rows(w[a:a + b_], B_HEADS)
    wc = pad_rows(w[a + b_:a + b_ + c_], C_HEADS)
    return wa.astype(BF16), wc.astype(BF16), wb.astype(BF16)


def kernel(x, c, positions, rel_bias, ada_w, ada_b, norm_ffn1, ffn1_w_gu, ffn1_w_down, norm_mix, w_in, mla_q_norm, mla_w_uq, mla_kv_norm, mla_w_ukv, idx_k_norm, w_out, norm_ffn2, ffn2_w_gu, ffn2_w_down, final_norm):
    B, T, D = x.shape
    L = ada_w.shape[0]
    assert T % (max(DILATIONS) * BLK) == 0 and T % 512 == 0 and D % LANES == 0

    mods = _modulation(c, ada_w, ada_b).reshape(L, B, N_MOD, D)
    cos_i, sin_i = _rope_tables(positions)
    cos_m, sin_m = cos_i, sin_i
    final_gain = final_norm.reshape(1, D)
    h = x.reshape(B * T, D)
    for l in range(L):
        mod = mods[l]
        h = _ffn(h, mod, norm_ffn1[l].reshape(1, D), ffn1_w_gu[l].astype(BF16), ffn1_w_down[l].astype(BF16),
                 final_gain, ks=0, final=False, seq=T)

        proj = _in_proj(h, mod, norm_mix[l].reshape(1, D), _layout_w_in(w_in[l]), seq=T)
        proj = proj.reshape(B, T, N_TILES * LANES)
        oa = _dilated_attention(proj, rel_bias, batch=B, seq=T)
        wq, wqs, wk, wv = _layout_mla(mla_w_uq[l], mla_w_ukv[l])
        ob = _mla_attention(proj, cos_m, sin_m, mla_q_norm[l].reshape(1, -1), mla_kv_norm[l].reshape(1, -1),
                            wq, wqs, wk, wv, batch=B, seq=T)
        g = idx_k_norm[l]
        g_sw = jnp.concatenate([g[IDX_ROPE // 2:IDX_ROPE], g[:IDX_ROPE // 2], jnp.zeros_like(g[IDX_ROPE:])])
        ik_gains = jnp.stack([jnp.concatenate([g, g]), jnp.concatenate([g_sw, g_sw])])
        sel = _dsa_select(proj, cos_i, sin_i, ik_gains, batch=B, seq=T)
        oc = _dsa_attention(proj, sel, rel_bias, batch=B, seq=T)
        wa, wc, wb = _layout_w_out(w_out[l])
        h = _out_proj(oa.reshape(B * T, -1), oc.reshape(B * T, -1), ob.reshape(B * T, -1), wa, wc, wb, h, mod, seq=T)

        h = _ffn(h, mod, norm_ffn2[l].reshape(1, D), ffn2_w_gu[l].astype(BF16), ffn2_w_down[l].astype(BF16),
                 final_gain, ks=6, final=(l == L - 1), seq=T)
    return h.reshape(B, T, D)
```

```python
import functools
import math

import numpy as np
import jax
import jax.numpy as jnp
from jax import lax
from jax.experimental import pallas as pl
from jax.experimental.pallas import tpu as pltpu

F32 = jnp.float32
BF16 = jnp.bfloat16

HEAD_DIM = 64
A_HEADS, B_HEADS, C_HEADS = 6, 5, 5
MLA_Q_RANK, MLA_KV_RANK, MLA_NOPE, MLA_ROPE = 256, 128, 64, 32
IDX_HEADS, IDX_DIM, IDX_ROPE = 8, 64, 32
TOPK = 256
N_BUCKETS, MAX_DISTANCE = 32, 128
ROPE_THETA = 10000.0
RMS_EPS = 1e-6
N_MOD = 9
NEG = -1e30
DILATIONS = (1, 4, 16)

LANES = 128
BLK = 128
QB = 256
VMEM_LIMIT = 56 * 1024 * 1024

T_QA, T_KA, T_VA = 0, 3, 6
T_QC, T_KC, T_VC = 9, 12, 15
T_CQ, T_IQ, T_IQS = 18, 20, 24
T_CKV, T_KR, T_KRS, T_IK, T_IKS, T_IW = 28, 29, 30, 31, 32, 33
N_TILES = 34

KEY_FLIP = 0x7FFFFFFF
INT_MIN = -(2 ** 31)


def _cparams(*sem):
    return pltpu.CompilerParams(dimension_semantics=sem, vmem_limit_bytes=VMEM_LIMIT)


def _rms(x, g):
    return x * lax.rsqrt(jnp.mean(x * x, axis=-1, keepdims=True) + RMS_EPS) * g


def _dot(a, b):
    return jnp.dot(a, b, preferred_element_type=F32)


def _dot_t(a, b):
    return lax.dot_general(a, b, (((1,), (1,)), ((), ())), preferred_element_type=F32)


def _np_bucket(dist):
    n = np.maximum(dist, 0)
    max_exact = N_BUCKETS // 2
    nf = np.maximum(n, 1).astype(np.float32)
    large = max_exact + (np.log(nf / np.float32(max_exact)) / np.float32(math.log(MAX_DISTANCE / max_exact))
                         * np.float32(N_BUCKETS - max_exact)).astype(np.int32)
    large = np.minimum(large, N_BUCKETS - 1)
    return np.where(n < max_exact, n, large).astype(np.int32)


def _bias_from_buckets(bucket, tab_ref, col):
    acc = jnp.full(bucket.shape, NEG, F32)
    for b in range(N_BUCKETS):
        acc = jnp.where(bucket == b, tab_ref[b, col], acc)
    return acc


def _mod_kernel(c_ref, w_ref, b_ref, o_ref):
    c = c_ref[...]
    cond = c / (1.0 + jnp.exp(-c))
    w = w_ref[0]
    c_hi = cond.astype(BF16)
    c_lo = (cond - c_hi.astype(F32)).astype(BF16)
    w_hi = w.astype(BF16)
    w_lo = (w - w_hi.astype(F32)).astype(BF16)
    o_ref[0] = _dot(c_hi, w_hi) + _dot(c_hi, w_lo) + _dot(c_lo, w_hi) + b_ref[0]


def _modulation(c, ada_w, ada_b):
    L, D, N = ada_w.shape
    B = c.shape[0]
    tn = 1024
    return pl.pallas_call(
        _mod_kernel,
        grid=(L, N // tn),
        in_specs=[pl.BlockSpec((B, D), lambda l, j: (0, 0)),
                  pl.BlockSpec((1, D, tn), lambda l, j: (l, 0, j)),
                  pl.BlockSpec((1, 1, tn), lambda l, j: (l, 0, j))],
        out_specs=pl.BlockSpec((1, B, tn), lambda l, j: (l, 0, j)),
        out_shape=jax.ShapeDtypeStruct((L, B, N), F32),
        compiler_params=_cparams("arbitrary", "arbitrary"),
        name="adaln_mod",
    )(c, ada_w, ada_b.reshape(L, 1, N))


def _ffn_kernel(h_ref, mod_ref, g_ref, wgu_ref, wd_ref, fg_ref, o_ref, *, ks, final, tf):
    F = wd_ref.shape[0]
    x = h_ref[...]
    y = _rms(x, g_ref[...])
    n = (y * (1.0 + mod_ref[0, ks + 1:ks + 2, :]) + mod_ref[0, ks:ks + 1, :]).astype(BF16)
    acc = jnp.zeros(x.shape, F32)
    for f0 in range(0, F, tf):
        g = _dot(n, wgu_ref[:, f0:f0 + tf])
        u = _dot(n, wgu_ref[:, F + f0:F + f0 + tf])
        a = (g / (1.0 + jnp.exp(-g))) * u
        acc = acc + _dot(a.astype(BF16), wd_ref[f0:f0 + tf, :])
    out = x + (0.5 * mod_ref[0, ks + 2:ks + 3, :]) * acc
    if final:
        out = _rms(out, fg_ref[...])
    o_ref[...] = out


def _resident(a):
    return pl.BlockSpec(a.shape, lambda *_: (0,) * a.ndim, pipeline_mode=pl.Buffered(1))


def _ffn(h, mod, gain, w_gu, w_down, final_gain, *, ks, final, seq):
    M, D = h.shape
    F = w_down.shape[0]
    tm = 512
    tf = 256
    assert F % tf == 0
    per_b = seq // tm
    return pl.pallas_call(
        functools.partial(_ffn_kernel, ks=ks, final=final, tf=tf),
        grid=(M // tm,),
        in_specs=[pl.BlockSpec((tm, D), lambda i: (i, 0)),
                  pl.BlockSpec((1, N_MOD, D), lambda i: (i // per_b, 0, 0)),
                  pl.BlockSpec((1, D), lambda i: (0, 0)),
                  _resident(w_gu), _resident(w_down),
                  pl.BlockSpec((1, D), lambda i: (0, 0))],
        out_specs=pl.BlockSpec((tm, D), lambda i: (i, 0)),
        out_shape=jax.ShapeDtypeStruct((M, D), F32),
        compiler_params=_cparams("arbitrary"),
        name="ffn",
    )(h, mod, gain, w_gu, w_down, final_gain)


def _win_kernel(h_ref, mod_ref, g_ref, w_ref, o_ref, *, tn):
    y = _rms(h_ref[...], g_ref[...])
    n = (y * (1.0 + mod_ref[0, 4:5, :]) + mod_ref[0, 3:4, :]).astype(BF16)
    for c0 in range(0, w_ref.shape[1], tn):
        o_ref[:, c0:c0 + tn] = _dot(n, w_ref[:, c0:c0 + tn])


def _in_proj(h, mod, gain, w, *, seq):
    M, D = h.shape
    N = w.shape[1]
    tm, tn = 512, 256
    assert N % tn == 0
    per_b = seq // tm
    return pl.pallas_call(
        functools.partial(_win_kernel, tn=tn),
        grid=(M // tm,),
        in_specs=[pl.BlockSpec((tm, D), lambda i: (i, 0)),
                  pl.BlockSpec((1, N_MOD, D), lambda i: (i // per_b, 0, 0)),
                  pl.BlockSpec((1, D), lambda i: (0, 0)),
                  _resident(w)],
        out_specs=pl.BlockSpec((tm, N), lambda i: (i, 0)),
        out_shape=jax.ShapeDtypeStruct((M, N), F32),
        compiler_params=_cparams("arbitrary"),
        name="in_proj",
    )(h, mod, gain, w)


def _rope_kernel(pos_ref, inv_ref, c_ref, s_ref):
    ang = pos_ref[0].astype(F32) * inv_ref[...]
    c_ref[0] = jnp.cos(ang)
    s_ref[0] = jnp.sin(ang)


def _rope_tables(positions):
    B, T = positions.shape
    half = IDX_ROPE // 2
    assert MLA_ROPE == IDX_ROPE and MLA_NOPE == IDX_DIM
    inv = (ROPE_THETA ** (-np.arange(half, dtype=np.float32) / half)).astype(np.float32)
    zeros32 = np.zeros(IDX_DIM - IDX_ROPE, np.float32)
    inv_row = jnp.asarray(np.concatenate([inv, inv, zeros32, inv, inv, zeros32])[None, :])
    tab = jax.ShapeDtypeStruct((B, T, LANES), F32)
    spec = pl.BlockSpec((1, T, LANES), lambda b: (b, 0, 0))
    return pl.pallas_call(
        _rope_kernel,
        grid=(B,),
        in_specs=[pl.BlockSpec((1, T, 1), lambda b: (b, 0, 0)),
                  pl.BlockSpec((1, LANES), lambda b: (0, 0))],
        out_specs=[spec, spec],
        out_shape=[tab, tab],
        compiler_params=_cparams("arbitrary"),
        name="rope_tables",
    )(positions.reshape(B, T, 1), inv_row)


def _dil_kernel(tab_ref, bk_ref, q_ref, k_ref, v_ref, o_ref, bt_scr, o_scr, m_scr, s_scr, *, seq):
    b = pl.program_id(0)
    pair = pl.program_id(1)

    @pl.when(b == 0)
    def _():
        for p in range(3):
            for hh in range(2):
                bt_scr[pair, p, hh] = _bias_from_buckets(bk_ref[p], tab_ref, 2 * pair + hh)

    nblk = seq // BLK
    low = lax.broadcasted_iota(jnp.int32, (nblk, BLK, LANES), 2) < HEAD_DIM
    neg_half = jnp.full((BLK, BLK), NEG, F32)

    for p, d in enumerate(DILATIONS):
        nb = seq // (d * BLK)
        starts = [r + d * BLK * n for n in range(nb) for r in range(d)]

        def rows_of(s0, d=d):
            return pl.ds(s0, BLK) if d == 1 else pl.ds(s0, BLK, stride=d)

        def regroup(ref):
            return jnp.stack([ref[0, rows_of(s0), :] for s0 in starts])

        def band(z):
            if nb == 1:
                return z
            prev = jnp.concatenate([jnp.zeros((d,) + z.shape[1:], z.dtype), z[:-d]], axis=0)
            return jnp.concatenate([prev, z], axis=1)

        q = regroup(q_ref) * (HEAD_DIM ** -0.5)
        kband = band(regroup(k_ref).astype(BF16))
        vband = band(regroup(v_ref).astype(BF16))
        res = []
        for hh in range(2):
            qh = (jnp.where(low, q, 0.0) if hh == 0 else jnp.where(low, 0.0, q)).astype(BF16)
            logit = lax.dot_general(qh, kband, (((2,), (2,)), ((0,), (0,))), preferred_element_type=F32)
            bias = bt_scr[pair, p, hh]
            if nb == 1:
                logit = logit + bias[:, BLK:][None]
            else:
                first = jnp.concatenate([neg_half, bias[:, BLK:]], axis=1)
                logit = jnp.concatenate([logit[:d] + first[None], logit[d:] + bias[None]], axis=0)
            m = jnp.max(logit, axis=-1, keepdims=True)
            e = jnp.exp(logit - m)
            s = jnp.sum(e, axis=-1, keepdims=True)
            o = lax.dot_general(e.astype(BF16), vband, (((2,), (1,)), ((0,), (0,))), preferred_element_type=F32)
            res.append((m, s, o))
        o_all = jnp.where(low, res[0][2], res[1][2])
        m_all = jnp.where(low, res[0][0], res[1][0])
        s_all = jnp.where(low, res[0][1], res[1][1])
        for bi, s0 in enumerate(starts):
            o_scr[p, rows_of(s0), :] = o_all[bi]
            m_scr[p, rows_of(s0), :] = m_all[bi]
            s_scr[p, rows_of(s0), :] = s_all[bi]

    m_all = jnp.maximum(jnp.maximum(m_scr[0], m_scr[1]), m_scr[2])
    num = jnp.zeros((seq, LANES), F32)
    den = jnp.zeros((seq, LANES), F32)
    for p in range(3):
        w = jnp.exp(m_scr[p] - m_all)
        num = num + w * o_scr[p]
        den = den + w * s_scr[p]
    o_ref[0] = (num / den).astype(BF16)


def _dilated_attention(proj, rel_bias, *, batch, seq):
    qi = np.arange(BLK)[:, None]
    kj = np.arange(2 * BLK)[None, :]
    step = BLK + qi - kj
    in_band = (step >= 0) & (step <= BLK)
    bk = np.stack([np.where(in_band, _np_bucket(step * d), -1) for d in DILATIONS]).astype(np.int32)
    pairs = A_HEADS // 2

    def col(t0):
        return pl.BlockSpec((1, seq, LANES), lambda b, p: (b, 0, t0 + p))

    return pl.pallas_call(
        functools.partial(_dil_kernel, seq=seq),
        grid=(batch, pairs),
        in_specs=[pl.BlockSpec(memory_space=pltpu.SMEM),
                  pl.BlockSpec((3, BLK, 2 * BLK), lambda b, p: (0, 0, 0)),
                  col(T_QA), col(T_KA), col(T_VA)],
        out_specs=pl.BlockSpec((1, seq, LANES), lambda b, p: (b, 0, p)),
        out_shape=jax.ShapeDtypeStruct((batch, seq, pairs * LANES), BF16),
        scratch_shapes=[pltpu.VMEM((pairs, 3, 2, BLK, 2 * BLK), F32),
                        pltpu.VMEM((3, seq, LANES), F32),
                        pltpu.VMEM((3, seq, LANES), F32),
                        pltpu.VMEM((3, seq, LANES), F32)],
        compiler_params=_cparams("arbitrary", "arbitrary"),
        name="dilated_attn",
    )(rel_bias, jnp.asarray(bk), proj, proj, proj)


def _mla_kernel(cq_ref, ckv_ref, kr_ref, krs_ref, c_ref, s_ref, qn_ref, kvn_ref,
                wq_ref, wqs_ref, wk_ref, wv_ref, o_ref, q_scr, k_scr, v_scr, *, seq):
    rc = 512
    for c0 in range(0, seq, rc):
        rows = slice(c0, c0 + rc)
        lane_r = lax.broadcasted_iota(jnp.int32, (rc, LANES), 1)
        rot = (lane_r >= MLA_NOPE) & (lane_r < MLA_NOPE + MLA_ROPE)
        cos = jnp.where(rot, c_ref[0, rows, :], 1.0)
        sin = jnp.where(rot, s_ref[0, rows, :], 0.0)
        nq = _rms(cq_ref[0, rows, :], qn_ref[...]).astype(BF16)
        nkv = _rms(ckv_ref[0, rows, :], kvn_ref[...]).astype(BF16)
        kr = kr_ref[0, rows, :] * cos + krs_ref[0, rows, :] * sin
        for h in range(B_HEADS):
            cols = slice(h * LANES, (h + 1) * LANES)
            q_scr[h, rows, :] = (_dot(nq, wq_ref[:, cols]) * cos + _dot(nq, wqs_ref[:, cols]) * sin).astype(BF16)
            k_scr[h, rows, :] = (_dot(nkv, wk_ref[:, cols]) + kr).astype(BF16)
            v = _dot(nkv, wv_ref[:, cols])
            lane_c = lax.broadcasted_iota(jnp.int32, v.shape, 1)
            v_scr[h, rows, :] = jnp.where(lane_c < HEAD_DIM, v, 1.0).astype(BF16)

    exp2_scale = (MLA_NOPE + MLA_ROPE) ** -0.5 * math.log2(math.e)
    causal = lax.broadcasted_iota(jnp.int32, (QB, QB), 1) <= lax.broadcasted_iota(jnp.int32, (QB, QB), 0)
    low = lax.broadcasted_iota(jnp.int32, (QB, LANES), 1) < HEAD_DIM
    blocks = [(qb, h) for qb in range(seq // QB) for h in range(B_HEADS)]

    def qk(qb, h):
        return _dot_t(q_scr[h, qb * QB:(qb + 1) * QB, :], k_scr[h, 0:(qb + 1) * QB, :])

    nxt = qk(*blocks[0])
    outs = []
    for idx, (qb, h) in enumerate(blocks):
        qs, ext = qb * QB, (qb + 1) * QB
        logit = nxt
        if idx + 1 < len(blocks):
            nxt = qk(*blocks[idx + 1])
        diag = jnp.where(causal, logit[:, qs:], NEG)
        logit = diag if qb == 0 else jnp.concatenate([logit[:, :qs], diag], axis=1)
        m = jnp.max(logit, axis=-1, keepdims=True)
        e = jnp.exp2((logit - m) * exp2_scale).astype(BF16)
        pv = _dot(e, v_scr[h, 0:ext, :])
        outs.append(jnp.where(low, pv / pltpu.roll(pv, HEAD_DIM, 1), 0.0))
        if h == B_HEADS - 1:
            for pt in range((B_HEADS + 1) // 2):
                pair = outs[2 * pt]
                if 2 * pt + 1 < B_HEADS:
                    pair = pair + pltpu.roll(outs[2 * pt + 1], HEAD_DIM, 1)
                o_ref[0, qs:qs + QB, pt * LANES:(pt + 1) * LANES] = pair.astype(BF16)
            outs = []


def _mla_attention(proj, cos_m, sin_m, q_norm, kv_norm, wq, wqs, wk, wv, *, batch, seq):
    def col(t0, n):
        return pl.BlockSpec((1, seq, n * LANES), lambda b: (b, 0, t0 // n))

    def full(a):
        return pl.BlockSpec(a.shape, lambda b: (0,) * a.ndim)

    tab = pl.BlockSpec((1, seq, LANES), lambda b: (b, 0, 0))
    width = ((B_HEADS + 1) // 2) * LANES
    return pl.pallas_call(
        functools.partial(_mla_kernel, seq=seq),
        grid=(batch,),
        in_specs=[col(T_CQ, 2), col(T_CKV, 1), col(T_KR, 1), col(T_KRS, 1), tab, tab,
                  full(q_norm), full(kv_norm), full(wq), full(wqs), full(wk), full(wv)],
        out_specs=pl.BlockSpec((1, seq, width), lambda b: (b, 0, 0)),
        out_shape=jax.ShapeDtypeStruct((batch, seq, width), BF16),
        scratch_shapes=[pltpu.VMEM((B_HEADS, seq, LANES), BF16)] * 3,
        compiler_params=_cparams("arbitrary"),
        name="mla_attn",
    )(proj, proj, proj, proj, cos_m, sin_m, q_norm, kv_norm, wq, wqs, wk, wv)


def _sel_kernel(iq_ref, iqs_ref, iw_ref, cq_ref, sq_ref, ik_ref, iks_ref, ck_ref, sk_ref, g_ref,
                o_ref, ik_scr, iqm_scr, key_scr, hb_scr, *, seq):
    i = pl.program_id(1)
    qs = i * QB
    nk = seq // QB

    @pl.when(i == 0)
    def _():
        x = ik_ref[0]
        r = lax.rsqrt(jnp.mean(x * x, axis=-1, keepdims=True) + RMS_EPS)
        y = x * r * g_ref[0:1, :]
        ys = iks_ref[0] * r * g_ref[1:2, :]
        ik_scr[...] = (y * ck_ref[0] + ys * sk_ref[0]).astype(BF16)

    lane = lax.broadcasted_iota(jnp.int32, (QB, LANES), 1)
    low = lane < IDX_DIM
    cos = cq_ref[0]
    sin = sq_ref[0]
    for t in range(IDX_HEADS // 2):
        cols = slice(t * LANES, (t + 1) * LANES)
        rot = iq_ref[0, :, cols] * cos + iqs_ref[0, :, cols] * sin
        iqm_scr[2 * t] = jnp.where(low, rot, 0.0).astype(BF16)
        iqm_scr[2 * t + 1] = jnp.where(low, 0.0, rot).astype(BF16)
    w_t = (iw_ref[0] * (IDX_HEADS ** -0.5 * IDX_DIM ** -0.5)).T

    krow = lax.broadcasted_iota(jnp.int32, (QB, QB), 0)
    qcol = lax.broadcasted_iota(jnp.int32, (QB, QB), 1)
    target = jnp.minimum(qs + lax.broadcasted_iota(jnp.int32, (1, QB), 1) + 1, TOPK).astype(F32)
    min_normal = float(np.finfo(np.float32).tiny)
    min_normal_bits = int(np.array(min_normal, np.float32).view(np.int32))
    hi_mask = -(1 << 16)
    one16, zero16 = jnp.ones((), BF16), jnp.zeros((), BF16)
    sub16 = QB // 16

    def block(n_tiles):
        iqm = iqm_scr[...].reshape(IDX_HEADS * QB, LANES)
        for kb in range(n_tiles):
            rel = _dot_t(ik_scr[kb * QB:(kb + 1) * QB, :], iqm)
            sc = jnp.zeros((QB, QB), F32)
            for h in range(IDX_HEADS):
                sc = sc + jnp.maximum(rel[:, h * QB:(h + 1) * QB], 0.0) * w_t[h:h + 1, :]
            sc = jnp.where(jnp.abs(sc) < min_normal, 0.0, sc)
            if kb == n_tiles - 1:
                sc = jnp.where(krow <= qcol, sc, -jnp.inf)
            bits = lax.bitcast_convert_type(sc, jnp.int32)
            key_scr[kb] = jnp.where(bits < 0, bits ^ KEY_FLIP, bits)
            hb_scr[kb] = lax.bitcast_convert_type(bits & hi_mask, F32).astype(BF16)

        def count_hi(cand16):
            acc = jnp.zeros((16, QB), BF16)
            for c in range(n_tiles):
                x = hb_scr[c].reshape(sub16, 16, QB)
                for r in range(sub16):
                    acc = acc + jnp.where(x[r] >= cand16, one16, zero16)
            return jnp.sum(acc.astype(F32), axis=0, keepdims=True)

        def count(pred):
            acc = jnp.zeros((8, QB), F32)
            for c in range(n_tiles):
                acc = acc + jnp.sum(pred(key_scr[c]).astype(F32).reshape(QB // 8, 8, QB), axis=0)
            return jnp.sum(acc, axis=0, keepdims=True)

        def hi_step(bi, prefix):
            cand_u = prefix | lax.shift_left(jnp.int32(1), 31 - bi)
            k = cand_u ^ INT_MIN
            cbits = jnp.where(k < 0, k ^ KEY_FLIP, k) & hi_mask
            cbits = jnp.where((cbits > 0) & (cbits < min_normal_bits), min_normal_bits, cbits)
            cand = lax.bitcast_convert_type(cbits, F32)
            cnt = count_hi(jnp.broadcast_to(cand, (16, QB)).astype(BF16))
            return jnp.where(cnt >= target, cand_u, prefix)

        def lo_step(bi, prefix):
            cand_u = prefix | lax.shift_left(jnp.int32(1), 15 - bi)
            cand = cand_u ^ INT_MIN
            cnt = count(lambda k: k >= cand)
            return jnp.where(cnt >= target, cand_u, prefix)

        prefix = lax.fori_loop(0, 16, hi_step, jnp.zeros((1, QB), jnp.int32))
        thr = lax.fori_loop(0, 16, lo_step, prefix) ^ INT_MIN
        need = target - count(lambda k: k > thr)

        lower = (qcol <= krow).astype(BF16)
        carry = jnp.zeros((1, QB), F32)
        for c in range(n_tiles):
            kc = key_scr[c]
            eq = kc == thr
            rank = carry + _dot(lower, eq.astype(BF16))
            sel = (kc > thr) | (eq & (rank <= need))
            o_ref[0, 0, c] = jnp.where(sel, 0.0, NEG).T
            carry = rank[QB - 1:QB, :]
        for c in range(n_tiles, nk):
            o_ref[0, 0, c] = jnp.full((QB, QB), NEG, F32)

    for n_tiles in range(1, nk + 1):
        pl.when(i == n_tiles - 1)(functools.partial(block, n_tiles))


def _dsa_select(proj, cos_i, sin_i, ik_gains, *, batch, seq):
    nq = seq // QB

    def qcol(t0, n):
        return pl.BlockSpec((1, QB, n * LANES), lambda b, i: (b, i, t0 // n))

    def kcol(t0):
        return pl.BlockSpec((1, seq, LANES), lambda b, i: (b, 0, t0))

    qtab = pl.BlockSpec((1, QB, LANES), lambda b, i: (b, i, 0))
    ktab = pl.BlockSpec((1, seq, LANES), lambda b, i: (b, 0, 0))
    return pl.pallas_call(
        functools.partial(_sel_kernel, seq=seq),
        grid=(batch, nq),
        in_specs=[qcol(T_IQ, 4), qcol(T_IQS, 4), qcol(T_IW, 1), qtab, qtab,
                  kcol(T_IK), kcol(T_IKS), ktab, ktab,
                  pl.BlockSpec((2, LANES), lambda b, i: (0, 0))],
        out_specs=pl.BlockSpec((1, 1, nq, QB, QB), lambda b, i: (b, i, 0, 0, 0)),
        out_shape=jax.ShapeDtypeStruct((batch, nq, nq, QB, QB), F32),
        scratch_shapes=[pltpu.VMEM((seq, LANES), BF16),
                        pltpu.VMEM((IDX_HEADS, QB, LANES), BF16),
                        pltpu.VMEM((nq, QB, QB), jnp.int32),
                        pltpu.VMEM((nq, QB, QB), BF16)],
        compiler_params=_cparams("arbitrary", "arbitrary"),
        name="dsa_select",
    )(proj, proj, proj, cos_i, sin_i, proj, proj, cos_i, sin_i, ik_gains)


def _dsa_kernel(tab_ref, bk_ref, q_ref, k_ref, v_ref, sel_ref, o_ref, kb_scr, vb_scr, bt_scr):
    b = pl.program_id(0)
    i = pl.program_id(1)

    n_pairs = (C_HEADS + 1) // 2
    heads_of = [min(2, C_HEADS - 2 * pt) for pt in range(n_pairs)]

    far = N_BUCKETS - 1

    @pl.when((b == 0) & (i == 0))
    def _():
        for pt in range(n_pairs):
            for hh in range(heads_of[pt]):
                col = A_HEADS + 2 * pt + hh
                for dd in range(2):
                    bt_scr[2 * pt + dd, hh * QB:(hh + 1) * QB, :] = (
                        _bias_from_buckets(bk_ref[dd], tab_ref, col) - tab_ref[far, col])

    @pl.when(i == 0)
    def _():
        kb_scr[...] = k_ref[0].astype(BF16)
        vb_scr[...] = v_ref[0].astype(BF16)

    lane = lax.broadcasted_iota(jnp.int32, (QB, LANES), 1)
    low = lane < HEAD_DIM

    def q_block(qb):
        ext = (qb + 1) * QB
        for pt in range(n_pairs):
            heads = heads_of[pt]
            rows = heads * QB
            cols = slice(pt * LANES, (pt + 1) * LANES)
            q = q_ref[0, :, cols] * (HEAD_DIM ** -0.5)
            halves = [jnp.where(low, q, 0.0), jnp.where(low, 0.0, q)][:heads]
            q2 = jnp.concatenate(halves, axis=0).astype(BF16)
            logit = _dot_t(q2, kb_scr[0:ext, cols])
            pieces = []
            for c in range(qb + 1):
                t = logit[:, c * QB:(c + 1) * QB] + jnp.concatenate([sel_ref[0, 0, c]] * heads, axis=0)
                if qb - c < 2:
                    t = t + bt_scr[2 * pt + (qb - c), 0:rows, :]
                pieces.append(t)
            logit = pieces[0] if qb == 0 else jnp.concatenate(pieces, axis=1)
            m = jnp.max(logit, axis=-1, keepdims=True)
            e = jnp.exp(logit - m)
            s = jnp.sum(e, axis=-1, keepdims=True)
            out = _dot(e.astype(BF16), vb_scr[0:ext, cols]) / s
            upper = out[QB:] if heads == 2 else 0.0
            o_ref[0, :, cols] = jnp.where(low, out[:QB], upper).astype(BF16)

    for qb in range(sel_ref.shape[2]):
        pl.when(i == qb)(functools.partial(q_block, qb))


def _dsa_attention(proj, sel, rel_bias, *, batch, seq):
    nq = seq // QB
    qi = np.arange(QB)[:, None]
    kj = np.arange(QB)[None, :]
    bk = np.stack([_np_bucket(qi - kj), _np_bucket(QB + qi - kj)]).astype(np.int32)
    assert _np_bucket(np.array(QB + 1)) == N_BUCKETS - 1
    width = ((C_HEADS + 1) // 2) * LANES
    return pl.pallas_call(
        _dsa_kernel,
        grid=(batch, nq),
        in_specs=[pl.BlockSpec(memory_space=pltpu.SMEM),
                  pl.BlockSpec((2, QB, QB), lambda b, i: (0, 0, 0)),
                  pl.BlockSpec((1, QB, width), lambda b, i: (b, i, T_QC * LANES // width)),
                  pl.BlockSpec((1, seq, width), lambda b, i: (b, 0, T_KC * LANES // width)),
                  pl.BlockSpec((1, seq, width), lambda b, i: (b, 0, T_VC * LANES // width)),
                  pl.BlockSpec((1, 1, nq, QB, QB), lambda b, i: (b, i, 0, 0, 0))],
        out_specs=pl.BlockSpec((1, QB, width), lambda b, i: (b, i, 0)),
        out_shape=jax.ShapeDtypeStruct((batch, seq, width), BF16),
        scratch_shapes=[pltpu.VMEM((seq, width), BF16),
                        pltpu.VMEM((seq, width), BF16),
                        pltpu.VMEM((2 * ((C_HEADS + 1) // 2), 2 * QB, QB), F32)],
        compiler_params=_cparams("arbitrary", "arbitrary"),
        name="dsa_attn",
    )(rel_bias, jnp.asarray(bk), proj, proj, proj, sel)


def _out_kernel(oa_ref, oc_ref, ob_ref, wa_ref, wc_ref, wb_ref, h_ref, mod_ref, o_ref):
    acc = _dot(oa_ref[...], wa_ref[...]) + _dot(oc_ref[...], wc_ref[...]) + _dot(ob_ref[...], wb_ref[...])
    o_ref[...] = h_ref[...] + mod_ref[0, 5:6, :] * acc


def _out_proj(oa, oc, ob, wa, wc, wb, h, mod, *, seq):
    M, D = h.shape
    tm = 512
    per_b = seq // tm

    def rows(a):
        return pl.BlockSpec((tm, a.shape[1]), lambda i: (i, 0))

    def full(a):
        return pl.BlockSpec(a.shape, lambda i: (0, 0))

    return pl.pallas_call(
        _out_kernel,
        grid=(M // tm,),
        in_specs=[rows(oa), rows(oc), rows(ob), full(wa), full(wc), full(wb), rows(h),
                  pl.BlockSpec((1, N_MOD, D), lambda i: (i // per_b, 0, 0))],
        out_specs=pl.BlockSpec((tm, D), lambda i: (i, 0)),
        out_shape=jax.ShapeDtypeStruct((M, D), F32),
        compiler_params=_cparams("arbitrary"),
        name="out_proj",
    )(oa, oc, ob, wa, wc, wb, h, mod)


def _swap_halves(w, rope):
    half = rope // 2
    return jnp.concatenate([-w[..., half:rope], w[..., :half], jnp.zeros_like(w[..., rope:])], axis=-1)


def _pad_cols(w, n):
    return jnp.concatenate([w, jnp.zeros(w.shape[:-1] + (n - w.shape[-1],), w.dtype)], axis=-1)


def _layout_w_in(w):
    D = w.shape[0]
    sizes = (A_HEADS * HEAD_DIM,) * 3 + (MLA_Q_RANK, MLA_KV_RANK, MLA_ROPE) + (C_HEADS * HEAD_DIM,) * 3 + (
        IDX_HEADS * IDX_DIM, IDX_DIM, IDX_HEADS)
    offs = np.cumsum((0,) + sizes)
    qa, ka, va, cq, ckv, kr, qc, kc, vc, iq, ik, iw = [w[:, offs[n]:offs[n + 1]] for n in range(len(sizes))]
    c_w = ((C_HEADS + 1) // 2) * LANES
    z64 = jnp.zeros((D, MLA_NOPE), w.dtype)
    iq_h = iq.reshape(D, IDX_HEADS, IDX_DIM)
    iq_sw = _swap_halves(iq_h, IDX_ROPE).reshape(D, IDX_HEADS * IDX_DIM)
    ik_sw = _swap_halves(ik, IDX_ROPE)
    cols = [qa, ka, va, _pad_cols(qc, c_w), _pad_cols(kc, c_w), _pad_cols(vc, c_w), cq, iq, iq_sw, ckv,
            _pad_cols(jnp.concatenate([z64, kr], axis=-1), LANES),
            _pad_cols(jnp.concatenate([z64, _swap_halves(kr, MLA_ROPE)], axis=-1), LANES),
            jnp.concatenate([ik, ik], axis=-1), jnp.concatenate([ik_sw, ik_sw], axis=-1), _pad_cols(iw, LANES)]
    out = jnp.concatenate(cols, axis=-1)
    assert out.shape[1] == N_TILES * LANES
    return out.astype(BF16)


def _layout_mla(w_uq, w_ukv):
    rq = w_uq.reshape(MLA_Q_RANK, B_HEADS, MLA_NOPE + MLA_ROPE)
    nope, rope = rq[..., :MLA_NOPE], rq[..., MLA_NOPE:]
    wq = _pad_cols(jnp.concatenate([nope, rope], axis=-1), LANES)
    wqs = _pad_cols(jnp.concatenate([jnp.zeros_like(nope), _swap_halves(rope, MLA_ROPE)], axis=-1), LANES)
    rkv = w_ukv.reshape(MLA_KV_RANK, B_HEADS, MLA_NOPE + HEAD_DIM)
    wk = _pad_cols(rkv[..., :MLA_NOPE], LANES)
    wv = _pad_cols(rkv[..., MLA_NOPE:], LANES)
    flat = lambda a, r: a.reshape(r, B_HEADS * LANES).astype(BF16)
    return flat(wq, MLA_Q_RANK), flat(wqs, MLA_Q_RANK), flat(wk, MLA_KV_RANK), flat(wv, MLA_KV_RANK)


def _layout_w_out(w):
    D = w.shape[1]
    a, b_, c_ = A_HEADS * HEAD_DIM, B_HEADS * HEAD_DIM, C_HEADS * HEAD_DIM
    def pad_rows(x, heads):
        return jnp.concatenate([x, jnp.zeros((((heads + 1) // 2) * LANES - x.shape[0], D), w.dtype)], axis=0)

    wa = w[:a]
    wb = pad_rows(w[a:a + b_], B_HEADS)
    wc = pad_rows(w[a + b_:a + b_ + c_], C_HEADS)
    return wa.astype(BF16), wc.astype(BF16), wb.astype(BF16)


def kernel(x, c, positions, rel_bias, ada_w, ada_b, norm_ffn1, ffn1_w_gu, ffn1_w_down, norm_mix, w_in, mla_q_norm, mla_w_uq, mla_kv_norm, mla_w_ukv, idx_k_norm, w_out, norm_ffn2, ffn2_w_gu, ffn2_w_down, final_norm):
    B, T, D = x.shape
    L = ada_w.shape[0]
    assert T % (max(DILATIONS) * BLK) == 0 and T % 512 == 0 and D % LANES == 0

    mods = _modulation(c, ada_w, ada_b).reshape(L, B, N_MOD, D)
    cos_i, sin_i = _rope_tables(positions)
    cos_m, sin_m = cos_i, sin_i
    final_gain = final_norm.reshape(1, D)
    h = x.reshape(B * T, D)
    for l in range(L):
        mod = mods[l]
        h = _ffn(h, mod, norm_ffn1[l].reshape(1, D), ffn1_w_gu[l].astype(BF16), ffn1_w_down[l].astype(BF16),
                 final_gain, ks=0, final=False, seq=T)

        proj = _in_proj(h, mod, norm_mix[l].reshape(1, D), _layout_w_in(w_in[l]), seq=T)
        proj = proj.reshape(B, T, N_TILES * LANES)
        oa = _dilated_attention(proj, rel_bias, batch=B, seq=T)
        wq, wqs, wk, wv = _layout_mla(mla_w_uq[l], mla_w_ukv[l])
        ob = _mla_attention(proj, cos_m, sin_m, mla_q_norm[l].reshape(1, -1), mla_kv_norm[l].reshape(1, -1),
                            wq, wqs, wk, wv, batch=B, seq=T)
        g = idx_k_norm[l]
        g_sw = jnp.concatenate([g[IDX_ROPE // 2:IDX_ROPE], g[:IDX_ROPE // 2], jnp.zeros_like(g[IDX_ROPE:])])
        ik_gains = jnp.stack([jnp.concatenate([g, g]), jnp.concatenate([g_sw, g_sw])])
        sel = _dsa_select(proj, cos_i, sin_i, ik_gains, batch=B, seq=T)
        oc = _dsa_attention(proj, sel, rel_bias, batch=B, seq=T)
        wa, wc, wb = _layout_w_out(w_out[l])
        h = _out_proj(oa.reshape(B * T, -1), oc.reshape(B * T, -1), ob.reshape(B * T, -1), wa, wc, wb, h, mod, seq=T)

        h = _ffn(h, mod, norm_ffn2[l].reshape(1, D), ffn2_w_gu[l].astype(BF16), ffn2_w_down[l].astype(BF16),
                 final_gain, ks=6, final=(l == L - 1), seq=T)
    return h.reshape(B, T, D)
```

```python
import functools
import math

import numpy as np
import jax
import jax.numpy as jnp
from jax import lax
from jax.experimental import pallas as pl
from jax.experimental.pallas import tpu as pltpu

F32 = jnp.float32
BF16 = jnp.bfloat16

HEAD_DIM = 64
A_HEADS, B_HEADS, C_HEADS = 6, 5, 5
MLA_Q_RANK, MLA_KV_RANK, MLA_NOPE, MLA_ROPE = 256, 128, 64, 32
IDX_HEADS, IDX_DIM, IDX_ROPE = 8, 64, 32
TOPK = 256
N_BUCKETS, MAX_DISTANCE = 32, 128
ROPE_THETA = 10000.0
RMS_EPS = 1e-6
N_MOD = 9
NEG = -1e30
DILATIONS = (1, 4, 16)

LANES = 128
BLK = 128
QB = 256
VMEM_LIMIT = 56 * 1024 * 1024

T_QA, T_KA, T_VA = 0, 3, 6
T_QC, T_KC, T_VC = 9, 12, 15
T_CQ, T_IQ, T_IQS = 18, 20, 24
T_CKV, T_KR, T_KRS, T_IK, T_IKS, T_IW = 28, 29, 30, 31, 32, 33
N_TILES = 34

KEY_FLIP = 0x7FFFFFFF
INT_MIN = -(2 ** 31)


def _cparams(*sem):
    return pltpu.CompilerParams(dimension_semantics=sem, vmem_limit_bytes=VMEM_LIMIT)


def _rms(x, g):
    return x * lax.rsqrt(jnp.mean(x * x, axis=-1, keepdims=True) + RMS_EPS) * g


def _dot(a, b):
    return jnp.dot(a, b, preferred_element_type=F32)


def _dot_t(a, b):
    return lax.dot_general(a, b, (((1,), (1,)), ((), ())), preferred_element_type=F32)


def _np_bucket(dist):
    n = np.maximum(dist, 0)
    max_exact = N_BUCKETS // 2
    nf = np.maximum(n, 1).astype(np.float32)
    large = max_exact + (np.log(nf / np.float32(max_exact)) / np.float32(math.log(MAX_DISTANCE / max_exact))
                         * np.float32(N_BUCKETS - max_exact)).astype(np.int32)
    large = np.minimum(large, N_BUCKETS - 1)
    return np.where(n < max_exact, n, large).astype(np.int32)


def _bias_from_buckets(bucket, tab_ref, col):
    acc = jnp.full(bucket.shape, NEG, F32)
    for b in range(N_BUCKETS):
        acc = jnp.where(bucket == b, tab_ref[b, col], acc)
    return acc


def _mod_kernel(c_ref, w_ref, b_ref, o_ref):
    c = c_ref[...]
    cond = c / (1.0 + jnp.exp(-c))
    w = w_ref[0]
    c_hi = cond.astype(BF16)
    c_lo = (cond - c_hi.astype(F32)).astype(BF16)
    w_hi = w.astype(BF16)
    w_lo = (w - w_hi.astype(F32)).astype(BF16)
    o_ref[0] = _dot(c_hi, w_hi) + _dot(c_hi, w_lo) + _dot(c_lo, w_hi) + b_ref[0]


def _modulation(c, ada_w, ada_b):
    L, D, N = ada_w.shape
    B = c.shape[0]
    tn = 1024
    return pl.pallas_call(
        _mod_kernel,
        grid=(L, N // tn),
        in_specs=[pl.BlockSpec((B, D), lambda l, j: (0, 0)),
                  pl.BlockSpec((1, D, tn), lambda l, j: (l, 0, j)),
                  pl.BlockSpec((1, 1, tn), lambda l, j: (l, 0, j))],
        out_specs=pl.BlockSpec((1, B, tn), lambda l, j: (l, 0, j)),
        out_shape=jax.ShapeDtypeStruct((L, B, N), F32),
        compiler_params=_cparams("arbitrary", "arbitrary"),
        name="adaln_mod",
    )(c, ada_w, ada_b.reshape(L, 1, N))


def _ffn_kernel(h_ref, mod_ref, g_ref, wgu_ref, wd_ref, fg_ref, o_ref, *, ks, final, tf):
    _ffn_body(h_ref[...], mod_ref, g_ref, wgu_ref, wd_ref, fg_ref, o_ref, ks=ks, final=final, tf=tf)


def _mix_ffn_kernel(oa_ref, oc_ref, ob_ref, wa_ref, wc_ref, wb_ref, h_ref, mod_ref, g_ref, wgu_ref, wd_ref, fg_ref,
                    o_ref, *, final, tf):
    mixed = _dot(oa_ref[...], wa_ref[...]) + _dot(oc_ref[...], wc_ref[...]) + _dot(ob_ref[...], wb_ref[...])
    x = h_ref[...] + mod_ref[0, 5:6, :] * mixed
    _ffn_body(x, mod_ref, g_ref, wgu_ref, wd_ref, fg_ref, o_ref, ks=6, final=final, tf=tf)


def _ffn_body(x, mod_ref, g_ref, wgu_ref, wd_ref, fg_ref, o_ref, *, ks, final, tf):
    F = wd_ref.shape[0]
    y = _rms(x, g_ref[...])
    n = (y * (1.0 + mod_ref[0, ks + 1:ks + 2, :]) + mod_ref[0, ks:ks + 1, :]).astype(BF16)
    acc = jnp.zeros(x.shape, F32)
    for f0 in range(0, F, tf):
        g = _dot(n, wgu_ref[:, f0:f0 + tf])
        u = _dot(n, wgu_ref[:, F + f0:F + f0 + tf])
        a = (g / (1.0 + jnp.exp(-g))) * u
        acc = acc + _dot(a.astype(BF16), wd_ref[f0:f0 + tf, :])
    out = x + (0.5 * mod_ref[0, ks + 2:ks + 3, :]) * acc
    if final:
        out = _rms(out, fg_ref[...])
    o_ref[...] = out


def _resident(a):
    return pl.BlockSpec(a.shape, lambda *_: (0,) * a.ndim, pipeline_mode=pl.Buffered(1))


def _ffn(h, mod, gain, w_gu, w_down, final_gain, *, ks, final, seq):
    M, D = h.shape
    F = w_down.shape[0]
    tm = 512
    tf = 256
    assert F % tf == 0
    per_b = seq // tm
    return pl.pallas_call(
        functools.partial(_ffn_kernel, ks=ks, final=final, tf=tf),
        grid=(M // tm,),
        in_specs=[pl.BlockSpec((tm, D), lambda i: (i, 0)),
                  pl.BlockSpec((1, N_MOD, D), lambda i: (i // per_b, 0, 0)),
                  pl.BlockSpec((1, D), lambda i: (0, 0)),
                  _resident(w_gu), _resident(w_down),
                  pl.BlockSpec((1, D), lambda i: (0, 0))],
        out_specs=pl.BlockSpec((tm, D), lambda i: (i, 0)),
        out_shape=jax.ShapeDtypeStruct((M, D), F32),
        compiler_params=_cparams("arbitrary"),
        name="ffn",
    )(h, mod, gain, w_gu, w_down, final_gain)


def _mix_ffn(oa, oc, ob, wa, wc, wb, h, mod, gain, w_gu, w_down, final_gain, *, final, seq):
    M, D = h.shape
    tm = 512
    tf = 256
    assert w_down.shape[0] % tf == 0
    per_b = seq // tm

    def rows(a):
        return pl.BlockSpec((tm, a.shape[1]), lambda i: (i, 0))

    return pl.pallas_call(
        functools.partial(_mix_ffn_kernel, final=final, tf=tf),
        grid=(M // tm,),
        in_specs=[rows(oa), rows(oc), rows(ob), _resident(wa), _resident(wc), _resident(wb), rows(h),
                  pl.BlockSpec((1, N_MOD, D), lambda i: (i // per_b, 0, 0)),
                  pl.BlockSpec((1, D), lambda i: (0, 0)),
                  _resident(w_gu), _resident(w_down),
                  pl.BlockSpec((1, D), lambda i: (0, 0))],
        out_specs=pl.BlockSpec((tm, D), lambda i: (i, 0)),
        out_shape=jax.ShapeDtypeStruct((M, D), F32),
        compiler_params=_cparams("arbitrary"),
        name="mix_ffn",
    )(oa, oc, ob, wa, wc, wb, h, mod, gain, w_gu, w_down, final_gain)


def _win_kernel(h_ref, mod_ref, g_ref, w_ref, o_ref, *, tn):
    y = _rms(h_ref[...], g_ref[...])
    n = (y * (1.0 + mod_ref[0, 4:5, :]) + mod_ref[0, 3:4, :]).astype(BF16)
    for c0 in range(0, w_ref.shape[1], tn):
        o_ref[:, c0:c0 + tn] = _dot(n, w_ref[:, c0:c0 + tn])


def _in_proj(h, mod, gain, w, *, seq):
    M, D = h.shape
    N = w.shape[1]
    tm, tn = 512, 256
    assert N % tn == 0
    per_b = seq // tm
    return pl.pallas_call(
        functools.partial(_win_kernel, tn=tn),
        grid=(M // tm,),
        in_specs=[pl.BlockSpec((tm, D), lambda i: (i, 0)),
                  pl.BlockSpec((1, N_MOD, D), lambda i: (i // per_b, 0, 0)),
                  pl.BlockSpec((1, D), lambda i: (0, 0)),
                  _resident(w)],
        out_specs=pl.BlockSpec((tm, N), lambda i: (i, 0)),
        out_shape=jax.ShapeDtypeStruct((M, N), F32),
        compiler_params=_cparams("arbitrary"),
        name="in_proj",
    )(h, mod, gain, w)


def _rope_kernel(pos_ref, inv_ref, c_ref, s_ref):
    ang = pos_ref[0].astype(F32) * inv_ref[...]
    c_ref[0] = jnp.cos(ang)
    s_ref[0] = jnp.sin(ang)


def _rope_tables(positions):
    B, T = positions.shape
    half = IDX_ROPE // 2
    assert MLA_ROPE == IDX_ROPE and MLA_NOPE == IDX_DIM
    inv = (ROPE_THETA ** (-np.arange(half, dtype=np.float32) / half)).astype(np.float32)
    zeros32 = np.zeros(IDX_DIM - IDX_ROPE, np.float32)
    inv_row = jnp.asarray(np.concatenate([inv, inv, zeros32, inv, inv, zeros32])[None, :])
    tab = jax.ShapeDtypeStruct((B, T, LANES), F32)
    spec = pl.BlockSpec((1, T, LANES), lambda b: (b, 0, 0))
    return pl.pallas_call(
        _rope_kernel,
        grid=(B,),
        in_specs=[pl.BlockSpec((1, T, 1), lambda b: (b, 0, 0)),
                  pl.BlockSpec((1, LANES), lambda b: (0, 0))],
        out_specs=[spec, spec],
        out_shape=[tab, tab],
        compiler_params=_cparams("arbitrary"),
        name="rope_tables",
    )(positions.reshape(B, T, 1), inv_row)


def _dil_kernel(tab_ref, bk_ref, q_ref, k_ref, v_ref, o_ref, bt_scr, o_scr, m_scr, s_scr, *, seq):
    b = pl.program_id(0)
    pair = pl.program_id(1)

    @pl.when(b == 0)
    def _():
        for p in range(3):
            for hh in range(2):
                bt_scr[pair, p, hh] = _bias_from_buckets(bk_ref[p], tab_ref, 2 * pair + hh)

    nblk = seq // BLK
    low = lax.broadcasted_iota(jnp.int32, (nblk, BLK, LANES), 2) < HEAD_DIM
    neg_half = jnp.full((BLK, BLK), NEG, F32)

    for p, d in enumerate(DILATIONS):
        nb = seq // (d * BLK)
        starts = [r + d * BLK * n for n in range(nb) for r in range(d)]

        def rows_of(s0, d=d):
            return pl.ds(s0, BLK) if d == 1 else pl.ds(s0, BLK, stride=d)

        def regroup(ref):
            return jnp.stack([ref[0, rows_of(s0), :] for s0 in starts])

        def band(z):
            if nb == 1:
                return z
            prev = jnp.concatenate([jnp.zeros((d,) + z.shape[1:], z.dtype), z[:-d]], axis=0)
            return jnp.concatenate([prev, z], axis=1)

        q = regroup(q_ref) * (HEAD_DIM ** -0.5)
        kband = band(regroup(k_ref).astype(BF16))
        vband = band(regroup(v_ref).astype(BF16))
        res = []
        for hh in range(2):
            qh = (jnp.where(low, q, 0.0) if hh == 0 else jnp.where(low, 0.0, q)).astype(BF16)
            logit = lax.dot_general(qh, kband, (((2,), (2,)), ((0,), (0,))), preferred_element_type=F32)
            bias = bt_scr[pair, p, hh]
            if nb == 1:
                logit = logit + bias[:, BLK:][None]
            else:
                first = jnp.concatenate([neg_half, bias[:, BLK:]], axis=1)
                logit = jnp.concatenate([logit[:d] + first[None], logit[d:] + bias[None]], axis=0)
            m = jnp.max(logit, axis=-1, keepdims=True)
            e = jnp.exp(logit - m)
            s = jnp.sum(e, axis=-1, keepdims=True)
            o = lax.dot_general(e.astype(BF16), vband, (((2,), (1,)), ((0,), (0,))), preferred_element_type=F32)
            res.append((m, s, o))
        o_all = jnp.where(low, res[0][2], res[1][2])
        m_all = jnp.where(low, res[0][0], res[1][0])
        s_all = jnp.where(low, res[0][1], res[1][1])
        for bi, s0 in enumerate(starts):
            o_scr[p, rows_of(s0), :] = o_all[bi]
            m_scr[p, rows_of(s0), :] = m_all[bi]
            s_scr[p, rows_of(s0), :] = s_all[bi]

    m_all = jnp.maximum(jnp.maximum(m_scr[0], m_scr[1]), m_scr[2])
    num = jnp.zeros((seq, LANES), F32)
    den = jnp.zeros((seq, LANES), F32)
    for p in range(3):
        w = jnp.exp(m_scr[p] - m_all)
        num = num + w * o_scr[p]
        den = den + w * s_scr[p]
    o_ref[0] = (num / den).astype(BF16)


def _dilated_attention(proj, rel_bias, *, batch, seq):
    qi = np.arange(BLK)[:, None]
    kj = np.arange(2 * BLK)[None, :]
    step = BLK + qi - kj
    in_band = (step >= 0) & (step <= BLK)
    bk = np.stack([np.where(in_band, _np_bucket(step * d), -1) for d in DILATIONS]).astype(np.int32)
    pairs = A_HEADS // 2

    def col(t0):
        return pl.BlockSpec((1, seq, LANES), lambda b, p: (b, 0, t0 + p))

    return pl.pallas_call(
        functools.partial(_dil_kernel, seq=seq),
        grid=(batch, pairs),
        in_specs=[pl.BlockSpec(memory_space=pltpu.SMEM),
                  pl.BlockSpec((3, BLK, 2 * BLK), lambda b, p: (0, 0, 0)),
                  col(T_QA), col(T_KA), col(T_VA)],
        out_specs=pl.BlockSpec((1, seq, LANES), lambda b, p: (b, 0, p)),
        out_shape=jax.ShapeDtypeStruct((batch, seq, pairs * LANES), BF16),
        scratch_shapes=[pltpu.VMEM((pairs, 3, 2, BLK, 2 * BLK), F32),
                        pltpu.VMEM((3, seq, LANES), F32),
                        pltpu.VMEM((3, seq, LANES), F32),
                        pltpu.VMEM((3, seq, LANES), F32)],
        compiler_params=_cparams("arbitrary", "arbitrary"),
        name="dilated_attn",
    )(rel_bias, jnp.asarray(bk), proj, proj, proj)


def _mla_kernel(cq_ref, ckv_ref, kr_ref, krs_ref, c_ref, s_ref, qn_ref, kvn_ref,
                wq_ref, wqs_ref, wk_ref, wv_ref, o_ref, q_scr, k_scr, v_scr, *, seq):
    rc = 512
    for c0 in range(0, seq, rc):
        rows = slice(c0, c0 + rc)
        lane_r = lax.broadcasted_iota(jnp.int32, (rc, LANES), 1)
        rot = (lane_r >= MLA_NOPE) & (lane_r < MLA_NOPE + MLA_ROPE)
        cos = jnp.where(rot, c_ref[0, rows, :], 1.0)
        sin = jnp.where(rot, s_ref[0, rows, :], 0.0)
        nq = _rms(cq_ref[0, rows, :], qn_ref[...]).astype(BF16)
        nkv = _rms(ckv_ref[0, rows, :], kvn_ref[...]).astype(BF16)
        kr = kr_ref[0, rows, :] * cos + krs_ref[0, rows, :] * sin
        for h in range(B_HEADS):
            cols = slice(h * LANES, (h + 1) * LANES)
            q_scr[h, rows, :] = (_dot(nq, wq_ref[:, cols]) * cos + _dot(nq, wqs_ref[:, cols]) * sin).astype(BF16)
            k_scr[h, rows, :] = (_dot(nkv, wk_ref[:, cols]) + kr).astype(BF16)
            v = _dot(nkv, wv_ref[:, cols])
            lane_c = lax.broadcasted_iota(jnp.int32, v.shape, 1)
            v_scr[h, rows, :] = jnp.where(lane_c < HEAD_DIM, v, 1.0).astype(BF16)

    exp2_scale = (MLA_NOPE + MLA_ROPE) ** -0.5 * math.log2(math.e)
    causal = lax.broadcasted_iota(jnp.int32, (QB, QB), 1) <= lax.broadcasted_iota(jnp.int32, (QB, QB), 0)
    low = lax.broadcasted_iota(jnp.int32, (QB, LANES), 1) < HEAD_DIM
    blocks = [(qb, h) for qb in range(seq // QB) for h in range(B_HEADS)]

    def qk(qb, h):
        return _dot_t(q_scr[h, qb * QB:(qb + 1) * QB, :], k_scr[h, 0:(qb + 1) * QB, :])

    nxt = qk(*blocks[0])
    outs = []
    for idx, (qb, h) in enumerate(blocks):
        qs, ext = qb * QB, (qb + 1) * QB
        logit = nxt
        if idx + 1 < len(blocks):
            nxt = qk(*blocks[idx + 1])
        diag = jnp.where(causal, logit[:, qs:], NEG)
        logit = diag if qb == 0 else jnp.concatenate([logit[:, :qs], diag], axis=1)
        m = jnp.max(logit, axis=-1, keepdims=True)
        e = jnp.exp2((logit - m) * exp2_scale).astype(BF16)
        pv = _dot(e, v_scr[h, 0:ext, :])
        outs.append(jnp.where(low, pv / pltpu.roll(pv, HEAD_DIM, 1), 0.0))
        if h == B_HEADS - 1:
            for pt in range((B_HEADS + 1) // 2):
                pair = outs[2 * pt]
                if 2 * pt + 1 < B_HEADS:
                    pair = pair + pltpu.roll(outs[2 * pt + 1], HEAD_DIM, 1)
                o_ref[0, qs:qs + QB, pt * LANES:(pt + 1) * LANES] = pair.astype(BF16)
            outs = []


def _mla_attention(proj, cos_m, sin_m, q_norm, kv_norm, wq, wqs, wk, wv, *, batch, seq):
    def col(t0, n):
        return pl.BlockSpec((1, seq, n * LANES), lambda b: (b, 0, t0 // n))

    def full(a):
        return pl.BlockSpec(a.shape, lambda b: (0,) * a.ndim)

    tab = pl.BlockSpec((1, seq, LANES), lambda b: (b, 0, 0))
    width = ((B_HEADS + 1) // 2) * LANES
    return pl.pallas_call(
        functools.partial(_mla_kernel, seq=seq),
        grid=(batch,),
        in_specs=[col(T_CQ, 2), col(T_CKV, 1), col(T_KR, 1), col(T_KRS, 1), tab, tab,
                  full(q_norm), full(kv_norm), full(wq), full(wqs), full(wk), full(wv)],
        out_specs=pl.BlockSpec((1, seq, width), lambda b: (b, 0, 0)),
        out_shape=jax.ShapeDtypeStruct((batch, seq, width), BF16),
        scratch_shapes=[pltpu.VMEM((B_HEADS, seq, LANES), BF16)] * 3,
        compiler_params=_cparams("arbitrary"),
        name="mla_attn",
    )(proj, proj, proj, proj, cos_m, sin_m, q_norm, kv_norm, wq, wqs, wk, wv)


def _sel_kernel(iq_ref, iqs_ref, iw_ref, cq_ref, sq_ref, ik_ref, iks_ref, ck_ref, sk_ref, g_ref,
                o_ref, ik_scr, iqm_scr, key_scr, hb_scr, *, seq):
    i = pl.program_id(1)
    qs = i * QB
    nk = seq // QB

    @pl.when(i == 0)
    def _():
        x = ik_ref[0]
        r = lax.rsqrt(jnp.mean(x * x, axis=-1, keepdims=True) + RMS_EPS)
        y = x * r * g_ref[0:1, :]
        ys = iks_ref[0] * r * g_ref[1:2, :]
        ik_scr[...] = (y * ck_ref[0] + ys * sk_ref[0]).astype(BF16)

    lane = lax.broadcasted_iota(jnp.int32, (QB, LANES), 1)
    low = lane < IDX_DIM
    cos = cq_ref[0]
    sin = sq_ref[0]
    for t in range(IDX_HEADS // 2):
        cols = slice(t * LANES, (t + 1) * LANES)
        rot = iq_ref[0, :, cols] * cos + iqs_ref[0, :, cols] * sin
        iqm_scr[2 * t] = jnp.where(low, rot, 0.0).astype(BF16)
        iqm_scr[2 * t + 1] = jnp.where(low, 0.0, rot).astype(BF16)
    w_t = (iw_ref[0] * (IDX_HEADS ** -0.5 * IDX_DIM ** -0.5)).T

    krow = lax.broadcasted_iota(jnp.int32, (QB, QB), 0)
    qcol = lax.broadcasted_iota(jnp.int32, (QB, QB), 1)
    target = jnp.minimum(qs + lax.broadcasted_iota(jnp.int32, (1, QB), 1) + 1, TOPK).astype(F32)
    min_normal = float(np.finfo(np.float32).tiny)
    min_normal_bits = int(np.array(min_normal, np.float32).view(np.int32))
    hi_mask = -(1 << 16)
    one16, zero16 = jnp.ones((), BF16), jnp.zeros((), BF16)
    sub16 = QB // 16

    def block(n_tiles):
        iqm = iqm_scr[...].reshape(IDX_HEADS * QB, LANES)
        for kb in range(n_tiles):
            rel = _dot_t(ik_scr[kb * QB:(kb + 1) * QB, :], iqm)
            sc = jnp.zeros((QB, QB), F32)
            for h in range(IDX_HEADS):
                sc = sc + jnp.maximum(rel[:, h * QB:(h + 1) * QB], 0.0) * w_t[h:h + 1, :]
            sc = jnp.where(jnp.abs(sc) < min_normal, 0.0, sc)
            if kb == n_tiles - 1:
                sc = jnp.where(krow <= qcol, sc, -jnp.inf)
            bits = lax.bitcast_convert_type(sc, jnp.int32)
            key_scr[kb] = jnp.where(bits < 0, bits ^ KEY_FLIP, bits)
            hb_scr[kb] = lax.bitcast_convert_type(bits & hi_mask, F32).astype(BF16)

        def count_hi(cand16):
            acc = jnp.zeros((16, QB), BF16)
            for c in range(n_tiles):
                x = hb_scr[c].reshape(sub16, 16, QB)
                for r in range(sub16):
                    acc = acc + jnp.where(x[r] >= cand16, one16, zero16)
            return jnp.sum(acc.astype(F32), axis=0, keepdims=True)

        def count(pred):
            acc = jnp.zeros((8, QB), F32)
            for c in range(n_tiles):
                acc = acc + jnp.sum(pred(key_scr[c]).astype(F32).reshape(QB // 8, 8, QB), axis=0)
            return jnp.sum(acc, axis=0, keepdims=True)

        def hi_step(bi, prefix):
            cand_u = prefix | lax.shift_left(jnp.int32(1), 31 - bi)
            k = cand_u ^ INT_MIN
            cbits = jnp.where(k < 0, k ^ KEY_FLIP, k) & hi_mask
            cbits = jnp.where((cbits > 0) & (cbits < min_normal_bits), min_normal_bits, cbits)
            cand = lax.bitcast_convert_type(cbits, F32)
            cnt = count_hi(jnp.broadcast_to(cand, (16, QB)).astype(BF16))
            return jnp.where(cnt >= target, cand_u, prefix)

        def lo_step(bi, prefix):
            cand_u = prefix | lax.shift_left(jnp.int32(1), 15 - bi)
            cand = cand_u ^ INT_MIN
            cnt = count(lambda k: k >= cand)
            return jnp.where(cnt >= target, cand_u, prefix)

        prefix = lax.fori_loop(0, 16, hi_step, jnp.zeros((1, QB), jnp.int32))
        thr = lax.fori_loop(0, 16, lo_step, prefix) ^ INT_MIN
        need = target - count(lambda k: k > thr)

        lower = (qcol <= krow).astype(BF16)
        carry = jnp.zeros((1, QB), F32)
        for c in range(n_tiles):
            kc = key_scr[c]
            eq = kc == thr
            rank = carry + _dot(lower, eq.astype(BF16))
            sel = (kc > thr) | (eq & (rank <= need))
            o_ref[0, 0, c] = jnp.where(sel, 0.0, NEG).T.astype(BF16)
            carry = rank[QB - 1:QB, :]
        for c in range(n_tiles, nk):
            o_ref[0, 0, c] = jnp.full((QB, QB), NEG, BF16)

    for n_tiles in range(1, nk + 1):
        pl.when(i == n_tiles - 1)(functools.partial(block, n_tiles))


def _dsa_select(proj, cos_i, sin_i, ik_gains, *, batch, seq):
    nq = seq // QB

    def qcol(t0, n):
        return pl.BlockSpec((1, QB, n * LANES), lambda b, i: (b, i, t0 // n))

    def kcol(t0):
        return pl.BlockSpec((1, seq, LANES), lambda b, i: (b, 0, t0))

    qtab = pl.BlockSpec((1, QB, LANES), lambda b, i: (b, i, 0))
    ktab = pl.BlockSpec((1, seq, LANES), lambda b, i: (b, 0, 0))
    return pl.pallas_call(
        functools.partial(_sel_kernel, seq=seq),
        grid=(batch, nq),
        in_specs=[qcol(T_IQ, 4), qcol(T_IQS, 4), qcol(T_IW, 1), qtab, qtab,
                  kcol(T_IK), kcol(T_IKS), ktab, ktab,
                  pl.BlockSpec((2, LANES), lambda b, i: (0, 0))],
        out_specs=pl.BlockSpec((1, 1, nq, QB, QB), lambda b, i: (b, i, 0, 0, 0)),
        out_shape=jax.ShapeDtypeStruct((batch, nq, nq, QB, QB), BF16),
        scratch_shapes=[pltpu.VMEM((seq, LANES), BF16),
                        pltpu.VMEM((IDX_HEADS, QB, LANES), BF16),
                        pltpu.VMEM((nq, QB, QB), jnp.int32),
                        pltpu.VMEM((nq, QB, QB), BF16)],
        compiler_params=_cparams("arbitrary", "arbitrary"),
        name="dsa_select",
    )(proj, proj, proj, cos_i, sin_i, proj, proj, cos_i, sin_i, ik_gains)


def _dsa_kernel(tab_ref, bk_ref, q_ref, k_ref, v_ref, sel_ref, o_ref, kb_scr, vb_scr, bt_scr):
    b = pl.program_id(0)
    i = pl.program_id(1)

    n_pairs = (C_HEADS + 1) // 2
    heads_of = [min(2, C_HEADS - 2 * pt) for pt in range(n_pairs)]

    far = N_BUCKETS - 1

    @pl.when((b == 0) & (i == 0))
    def _():
        for pt in range(n_pairs):
            for hh in range(heads_of[pt]):
                col = A_HEADS + 2 * pt + hh
                for dd in range(2):
                    bt_scr[2 * pt + dd, hh * QB:(hh + 1) * QB, :] = (
                        _bias_from_buckets(bk_ref[dd], tab_ref, col) - tab_ref[far, col])

    @pl.when(i == 0)
    def _():
        kb_scr[...] = k_ref[0].astype(BF16)
        vb_scr[...] = v_ref[0].astype(BF16)

    lane = lax.broadcasted_iota(jnp.int32, (QB, LANES), 1)
    low = lane < HEAD_DIM

    def q_block(qb):
        ext = (qb + 1) * QB
        for pt in range(n_pairs):
            heads = heads_of[pt]
            rows = heads * QB
            cols = slice(pt * LANES, (pt + 1) * LANES)
            q = q_ref[0, :, cols] * (HEAD_DIM ** -0.5)
            halves = [jnp.where(low, q, 0.0), jnp.where(low, 0.0, q)][:heads]
            q2 = jnp.concatenate(halves, axis=0).astype(BF16)
            logit = _dot_t(q2, kb_scr[0:ext, cols])
            pieces = []
            for c in range(qb + 1):
                t = logit[:, c * QB:(c + 1) * QB] + jnp.concatenate([sel_ref[0, 0, c].astype(F32)] * heads, axis=0)
                if qb - c < 2:
                    t = t + bt_scr[2 * pt + (qb - c), 0:rows, :]
                pieces.append(t)
            logit = pieces[0] if qb == 0 else jnp.concatenate(pieces, axis=1)
            m = jnp.max(logit, axis=-1, keepdims=True)
            e = jnp.exp(logit - m)
            s = jnp.sum(e, axis=-1, keepdims=True)
            out = _dot(e.astype(BF16), vb_scr[0:ext, cols]) / s
            upper = out[QB:] if heads == 2 else 0.0
            o_ref[0, :, cols] = jnp.where(low, out[:QB], upper).astype(BF16)

    for qb in range(sel_ref.shape[2]):
        pl.when(i == qb)(functools.partial(q_block, qb))


def _dsa_attention(proj, sel, rel_bias, *, batch, seq):
    nq = seq // QB
    qi = np.arange(QB)[:, None]
    kj = np.arange(QB)[None, :]
    bk = np.stack([_np_bucket(qi - kj), _np_bucket(QB + qi - kj)]).astype(np.int32)
    assert _np_bucket(np.array(QB + 1)) == N_BUCKETS - 1
    width = ((C_HEADS + 1) // 2) * LANES
    return pl.pallas_call(
        _dsa_kernel,
        grid=(batch, nq),
        in_specs=[pl.BlockSpec(memory_space=pltpu.SMEM),
                  pl.BlockSpec((2, QB, QB), lambda b, i: (0, 0, 0)),
                  pl.BlockSpec((1, QB, width), lambda b, i: (b, i, T_QC * LANES // width)),
                  pl.BlockSpec((1, seq, width), lambda b, i: (b, 0, T_KC * LANES // width)),
                  pl.BlockSpec((1, seq, width), lambda b, i: (b, 0, T_VC * LANES // width)),
                  pl.BlockSpec((1, 1, nq, QB, QB), lambda b, i: (b, i, 0, 0, 0))],
        out_specs=pl.BlockSpec((1, QB, width), lambda b, i: (b, i, 0)),
        out_shape=jax.ShapeDtypeStruct((batch, seq, width), BF16),
        scratch_shapes=[pltpu.VMEM((seq, width), BF16),
                        pltpu.VMEM((seq, width), BF16),
                        pltpu.VMEM((2 * ((C_HEADS + 1) // 2), 2 * QB, QB), F32)],
        compiler_params=_cparams("arbitrary", "arbitrary"),
        name="dsa_attn",
    )(rel_bias, jnp.asarray(bk), proj, proj, proj, sel)


def _swap_halves(w, rope):
    half = rope // 2
    return jnp.concatenate([-w[..., half:rope], w[..., :half], jnp.zeros_like(w[..., rope:])], axis=-1)


def _pad_cols(w, n):
    return jnp.concatenate([w, jnp.zeros(w.shape[:-1] + (n - w.shape[-1],), w.dtype)], axis=-1)


def _layout_w_in(w):
    D = w.shape[0]
    sizes = (A_HEADS * HEAD_DIM,) * 3 + (MLA_Q_RANK, MLA_KV_RANK, MLA_ROPE) + (C_HEADS * HEAD_DIM,) * 3 + (
        IDX_HEADS * IDX_DIM, IDX_DIM, IDX_HEADS)
    offs = np.cumsum((0,) + sizes)
    qa, ka, va, cq, ckv, kr, qc, kc, vc, iq, ik, iw = [w[:, offs[n]:offs[n + 1]] for n in range(len(sizes))]
    c_w = ((C_HEADS + 1) // 2) * LANES
    z64 = jnp.zeros((D, MLA_NOPE), w.dtype)
    iq_h = iq.reshape(D, IDX_HEADS, IDX_DIM)
    iq_sw = _swap_halves(iq_h, IDX_ROPE).reshape(D, IDX_HEADS * IDX_DIM)
    ik_sw = _swap_halves(ik, IDX_ROPE)
    cols = [qa, ka, va, _pad_cols(qc, c_w), _pad_cols(kc, c_w), _pad_cols(vc, c_w), cq, iq, iq_sw, ckv,
            _pad_cols(jnp.concatenate([z64, kr], axis=-1), LANES),
            _pad_cols(jnp.concatenate([z64, _swap_halves(kr, MLA_ROPE)], axis=-1), LANES),
            jnp.concatenate([ik, ik], axis=-1), jnp.concatenate([ik_sw, ik_sw], axis=-1), _pad_cols(iw, LANES)]
    out = jnp.concatenate(cols, axis=-1)
    assert out.shape[1] == N_TILES * LANES
    return out.astype(BF16)


def _layout_mla(w_uq, w_ukv):
    rq = w_uq.reshape(MLA_Q_RANK, B_HEADS, MLA_NOPE + MLA_ROPE)
    nope, rope = rq[..., :MLA_NOPE], rq[..., MLA_NOPE:]
    wq = _pad_cols(jnp.concatenate([nope, rope], axis=-1), LANES)
    wqs = _pad_cols(jnp.concatenate([jnp.zeros_like(nope), _swap_halves(rope, MLA_ROPE)], axis=-1), LANES)
    rkv = w_ukv.reshape(MLA_KV_RANK, B_HEADS, MLA_NOPE + HEAD_DIM)
    wk = _pad_cols(rkv[..., :MLA_NOPE], LANES)
    wv = _pad_cols(rkv[..., MLA_NOPE:], LANES)
    flat = lambda a, r: a.reshape(r, B_HEADS * LANES).astype(BF16)
    return flat(wq, MLA_Q_RANK), flat(wqs, MLA_Q_RANK), flat(wk, MLA_KV_RANK), flat(wv, MLA_KV_RANK)


def _layout_w_out(w):
    D = w.shape[1]
    a, b_, c_ = A_HEADS * HEAD_DIM, B_HEADS * HEAD_DIM, C_HEADS * HEAD_DIM
    def pad_rows(x, heads):
        return jnp.concatenate([x, jnp.zeros((((heads + 1) // 2) * LANES - x.shape[0], D), w.dtype)], axis=0)

    wa = w[:a]
    wb = pad_rows(w[a:a + b_], B_HEADS)
    wc = pad_rows(w[a + b_:a + b_ + c_], C_HEADS)
    return wa.astype(BF16), wc.astype(BF16), wb.astype(BF16)


def kernel(x, c, positions, rel_bias, ada_w, ada_b, norm_ffn1, ffn1_w_gu, ffn1_w_down, norm_mix, w_in, mla_q_norm, mla_w_uq, mla_kv_norm, mla_w_ukv, idx_k_norm, w_out, norm_ffn2, ffn2_w_gu, ffn2_w_down, final_norm):
    B, T, D = x.shape
    L = ada_w.shape[0]
    assert T % (max(DILATIONS) * BLK) == 0 and T % 512 == 0 and D % LANES == 0

    mods = _modulation(c, ada_w, ada_b).reshape(L, B, N_MOD, D)
    cos_i, sin_i = _rope_tables(positions)
    cos_m, sin_m = cos_i, sin_i
    final_gain = final_norm.reshape(1, D)
    h = x.reshape(B * T, D)
    for l in range(L):
        mod = mods[l]
        h = _ffn(h, mod, norm_ffn1[l].reshape(1, D), ffn1_w_gu[l].astype(BF16), ffn1_w_down[l].astype(BF16),
                 final_gain, ks=0, final=False, seq=T)

        proj = _in_proj(h, mod, norm_mix[l].reshape(1, D), _layout_w_in(w_in[l]), seq=T)
        proj = proj.reshape(B, T, N_TILES * LANES)
        oa = _dilated_attention(proj, rel_bias, batch=B, seq=T)
        wq, wqs, wk, wv = _layout_mla(mla_w_uq[l], mla_w_ukv[l])
        ob = _mla_attention(proj, cos_m, sin_m, mla_q_norm[l].reshape(1, -1), mla_kv_norm[l].reshape(1, -1),
                            wq, wqs, wk, wv, batch=B, seq=T)
        g = idx_k_norm[l]
        g_sw = jnp.concatenate([g[IDX_ROPE // 2:IDX_ROPE], g[:IDX_ROPE // 2], jnp.zeros_like(g[IDX_ROPE:])])
        ik_gains = jnp.stack([jnp.concatenate([g, g]), jnp.concatenate([g_sw, g_sw])])
        sel = _dsa_select(proj, cos_i, sin_i, ik_gains, batch=B, seq=T)
        oc = _dsa_attention(proj, sel, rel_bias, batch=B, seq=T)
        wa, wc, wb = _layout_w_out(w_out[l])
        h = _mix_ffn(oa.reshape(B * T, -1), oc.reshape(B * T, -1), ob.reshape(B * T, -1), wa, wc, wb, h, mod,
                     norm_ffn2[l].reshape(1, D), ffn2_w_gu[l].astype(BF16), ffn2_w_down[l].astype(BF16),
                     final_gain, final=(l == L - 1), seq=T)
    return h.reshape(B, T, D)
```

```python
import functools
import math

import numpy as np
import jax
import jax.numpy as jnp
from jax import lax
from jax.experimental import pallas as pl
from jax.experimental.pallas import tpu as pltpu

F32 = jnp.float32
BF16 = jnp.bfloat16

HEAD_DIM = 64
A_HEADS, B_HEADS, C_HEADS = 6, 5, 5
MLA_Q_RANK, MLA_KV_RANK, MLA_NOPE, MLA_ROPE = 256, 128, 64, 32
IDX_HEADS, IDX_DIM, IDX_ROPE = 8, 64, 32
TOPK = 256
N_BUCKETS, MAX_DISTANCE = 32, 128
ROPE_THETA = 10000.0
RMS_EPS = 1e-6
N_MOD = 9
NEG = -1e30
DILATIONS = (1, 4, 16)

LANES = 128
BLK = 128
QB = 256
FFN_ROWS = 1024
FFN_CHUNK = 256
PROJ_ROWS = 512
VMEM_LIMIT = 56 * 1024 * 1024

T_QA, T_KA, T_VA = 0, 3, 6
T_QC, T_KC, T_VC = 9, 12, 15
T_CQ, T_IQ, T_IQS = 18, 20, 24
T_CKV, T_KR, T_KRS, T_IK, T_IKS, T_IW = 28, 29, 30, 31, 32, 33
N_TILES = 34

KEY_FLIP = 0x7FFFFFFF
INT_MIN = -(2 ** 31)


def _cparams(*sem):
    return pltpu.CompilerParams(dimension_semantics=sem, vmem_limit_bytes=VMEM_LIMIT)


def _rms(x, g):
    return x * lax.rsqrt(jnp.mean(x * x, axis=-1, keepdims=True) + RMS_EPS) * g


def _dot(a, b):
    return jnp.dot(a, b, preferred_element_type=F32)


def _dot_t(a, b):
    return lax.dot_general(a, b, (((1,), (1,)), ((), ())), preferred_element_type=F32)


def _np_bucket(dist):
    n = np.maximum(dist, 0)
    max_exact = N_BUCKETS // 2
    nf = np.maximum(n, 1).astype(np.float32)
    large = max_exact + (np.log(nf / np.float32(max_exact)) / np.float32(math.log(MAX_DISTANCE / max_exact))
                         * np.float32(N_BUCKETS - max_exact)).astype(np.int32)
    large = np.minimum(large, N_BUCKETS - 1)
    return np.where(n < max_exact, n, large).astype(np.int32)


def _bias_from_buckets(bucket, tab_ref, col):
    acc = jnp.full(bucket.shape, NEG, F32)
    for b in range(N_BUCKETS):
        acc = jnp.where(bucket == b, tab_ref[b, col], acc)
    return acc


def _mod_kernel(c_ref, w_ref, b_ref, o_ref):
    c = c_ref[...]
    cond = c / (1.0 + jnp.exp(-c))
    w = w_ref[0]
    c_hi = cond.astype(BF16)
    c_lo = (cond - c_hi.astype(F32)).astype(BF16)
    w_hi = w.astype(BF16)
    w_lo = (w - w_hi.astype(F32)).astype(BF16)
    o_ref[0] = _dot(c_hi, w_hi) + _dot(c_hi, w_lo) + _dot(c_lo, w_hi) + b_ref[0]


def _modulation(c, ada_w, ada_b):
    L, D, N = ada_w.shape
    B = c.shape[0]
    tn = 1024
    return pl.pallas_call(
        _mod_kernel,
        grid=(L, N // tn),
        in_specs=[pl.BlockSpec((B, D), lambda l, j: (0, 0)),
                  pl.BlockSpec((1, D, tn), lambda l, j: (l, 0, j)),
                  pl.BlockSpec((1, 1, tn), lambda l, j: (l, 0, j))],
        out_specs=pl.BlockSpec((1, B, tn), lambda l, j: (l, 0, j)),
        out_shape=jax.ShapeDtypeStruct((L, B, N), F32),
        compiler_params=_cparams("arbitrary", "arbitrary"),
        name="adaln_mod",
    )(c, ada_w, ada_b.reshape(L, 1, N))


def _ffn_kernel(h_ref, mod_ref, g_ref, wgu_ref, wd_ref, fg_ref, o_ref, *, ks, final, tf):
    _ffn_body(h_ref[...], mod_ref, g_ref, wgu_ref, wd_ref, fg_ref, o_ref, ks=ks, final=final, tf=tf)


def _mix_ffn_kernel(oa_ref, oc_ref, ob_ref, wa_ref, wc_ref, wb_ref, h_ref, mod_ref, g_ref, wgu_ref, wd_ref, fg_ref,
                    o_ref, *, final, tf):
    mixed = _dot(oa_ref[...], wa_ref[...]) + _dot(oc_ref[...], wc_ref[...]) + _dot(ob_ref[...], wb_ref[...])
    x = h_ref[...] + mod_ref[0, 5:6, :] * mixed
    _ffn_body(x, mod_ref, g_ref, wgu_ref, wd_ref, fg_ref, o_ref, ks=6, final=final, tf=tf)


def _ffn_body(x, mod_ref, g_ref, wgu_ref, wd_ref, fg_ref, o_ref, *, ks, final, tf):
    F = wd_ref.shape[0]
    y = _rms(x, g_ref[...])
    n = (y * (1.0 + mod_ref[0, ks + 1:ks + 2, :]) + mod_ref[0, ks:ks + 1, :]).astype(BF16)
    acc = jnp.zeros(x.shape, F32)
    for f0 in range(0, F, tf):
        g = _dot(n, wgu_ref[:, f0:f0 + tf])
        u = _dot(n, wgu_ref[:, F + f0:F + f0 + tf])
        a = (g / (1.0 + jnp.exp(-g))) * u
        acc = acc + _dot(a.astype(BF16), wd_ref[f0:f0 + tf, :])
    out = x + (0.5 * mod_ref[0, ks + 2:ks + 3, :]) * acc
    if final:
        out = _rms(out, fg_ref[...])
    o_ref[...] = out


def _resident(a):
    return pl.BlockSpec(a.shape, lambda *_: (0,) * a.ndim, pipeline_mode=pl.Buffered(1))


def _ffn(h, mod, gain, w_gu, w_down, final_gain, *, ks, final, seq):
    M, D = h.shape
    F = w_down.shape[0]
    tm, tf = FFN_ROWS, FFN_CHUNK
    assert F % tf == 0 and seq % tm == 0
    per_b = seq // tm
    return pl.pallas_call(
        functools.partial(_ffn_kernel, ks=ks, final=final, tf=tf),
        grid=(M // tm,),
        in_specs=[pl.BlockSpec((tm, D), lambda i: (i, 0)),
                  pl.BlockSpec((1, N_MOD, D), lambda i: (i // per_b, 0, 0)),
                  pl.BlockSpec((1, D), lambda i: (0, 0)),
                  _resident(w_gu), _resident(w_down),
                  pl.BlockSpec((1, D), lambda i: (0, 0))],
        out_specs=pl.BlockSpec((tm, D), lambda i: (i, 0)),
        out_shape=jax.ShapeDtypeStruct((M, D), F32),
        compiler_params=_cparams("arbitrary"),
        name="ffn",
    )(h, mod, gain, w_gu, w_down, final_gain)


def _mix_ffn(oa, oc, ob, wa, wc, wb, h, mod, gain, w_gu, w_down, final_gain, *, final, seq):
    M, D = h.shape
    tm, tf = FFN_ROWS, FFN_CHUNK
    assert w_down.shape[0] % tf == 0 and seq % tm == 0
    per_b = seq // tm

    def rows(a):
        return pl.BlockSpec((tm, a.shape[1]), lambda i: (i, 0))

    return pl.pallas_call(
        functools.partial(_mix_ffn_kernel, final=final, tf=tf),
        grid=(M // tm,),
        in_specs=[rows(oa), rows(oc), rows(ob), _resident(wa), _resident(wc), _resident(wb), rows(h),
                  pl.BlockSpec((1, N_MOD, D), lambda i: (i // per_b, 0, 0)),
                  pl.BlockSpec((1, D), lambda i: (0, 0)),
                  _resident(w_gu), _resident(w_down),
                  pl.BlockSpec((1, D), lambda i: (0, 0))],
        out_specs=pl.BlockSpec((tm, D), lambda i: (i, 0)),
        out_shape=jax.ShapeDtypeStruct((M, D), F32),
        compiler_params=_cparams("arbitrary"),
        name="mix_ffn",
    )(oa, oc, ob, wa, wc, wb, h, mod, gain, w_gu, w_down, final_gain)


def _win_kernel(h_ref, mod_ref, g_ref, w_ref, o_ref, *, tn):
    y = _rms(h_ref[...], g_ref[...])
    n = (y * (1.0 + mod_ref[0, 4:5, :]) + mod_ref[0, 3:4, :]).astype(BF16)
    for c0 in range(0, w_ref.shape[1], tn):
        o_ref[:, c0:c0 + tn] = _dot(n, w_ref[:, c0:c0 + tn])


def _in_proj(h, mod, gain, w, *, seq):
    M, D = h.shape
    N = w.shape[1]
    tm, tn = PROJ_ROWS, 2 * LANES
    assert N % tn == 0
    per_b = seq // tm
    return pl.pallas_call(
        functools.partial(_win_kernel, tn=tn),
        grid=(M // tm,),
        in_specs=[pl.BlockSpec((tm, D), lambda i: (i, 0)),
                  pl.BlockSpec((1, N_MOD, D), lambda i: (i // per_b, 0, 0)),
                  pl.BlockSpec((1, D), lambda i: (0, 0)),
                  _resident(w)],
        out_specs=pl.BlockSpec((tm, N), lambda i: (i, 0)),
        out_shape=jax.ShapeDtypeStruct((M, N), F32),
        compiler_params=_cparams("arbitrary"),
        name="in_proj",
    )(h, mod, gain, w)


def _rope_kernel(pos_ref, inv_ref, c_ref, s_ref):
    ang = pos_ref[0].astype(F32) * inv_ref[...]
    c_ref[0] = jnp.cos(ang)
    s_ref[0] = jnp.sin(ang)


def _rope_tables(positions):
    B, T = positions.shape
    half = IDX_ROPE // 2
    assert MLA_ROPE == IDX_ROPE and MLA_NOPE == IDX_DIM
    inv = (ROPE_THETA ** (-np.arange(half, dtype=np.float32) / half)).astype(np.float32)
    zeros32 = np.zeros(IDX_DIM - IDX_ROPE, np.float32)
    inv_row = jnp.asarray(np.concatenate([inv, inv, zeros32, inv, inv, zeros32])[None, :])
    tab = jax.ShapeDtypeStruct((B, T, LANES), F32)
    spec = pl.BlockSpec((1, T, LANES), lambda b: (b, 0, 0))
    return pl.pallas_call(
        _rope_kernel,
        grid=(B,),
        in_specs=[pl.BlockSpec((1, T, 1), lambda b: (b, 0, 0)),
                  pl.BlockSpec((1, LANES), lambda b: (0, 0))],
        out_specs=[spec, spec],
        out_shape=[tab, tab],
        compiler_params=_cparams("arbitrary"),
        name="rope_tables",
    )(positions.reshape(B, T, 1), inv_row)


def _dil_kernel(tab_ref, bk_ref, q_ref, k_ref, v_ref, o_ref, bt_scr, o_scr, m_scr, s_scr, *, seq):
    b = pl.program_id(0)
    pair = pl.program_id(1)

    @pl.when(b == 0)
    def _():
        for p in range(3):
            for hh in range(2):
                bt_scr[pair, p, hh] = _bias_from_buckets(bk_ref[p], tab_ref, 2 * pair + hh)

    nblk = seq // BLK
    low = lax.broadcasted_iota(jnp.int32, (nblk, BLK, LANES), 2) < HEAD_DIM
    neg_half = jnp.full((BLK, BLK), NEG, F32)

    for p, d in enumerate(DILATIONS):
        nb = seq // (d * BLK)
        starts = [r + d * BLK * n for n in range(nb) for r in range(d)]

        def rows_of(s0, d=d):
            return pl.ds(s0, BLK) if d == 1 else pl.ds(s0, BLK, stride=d)

        def regroup(ref):
            return jnp.stack([ref[0, rows_of(s0), :] for s0 in starts])

        def band(z):
            if nb == 1:
                return z
            prev = jnp.concatenate([jnp.zeros((d,) + z.shape[1:], z.dtype), z[:-d]], axis=0)
            return jnp.concatenate([prev, z], axis=1)

        q = regroup(q_ref) * (HEAD_DIM ** -0.5)
        kband = band(regroup(k_ref).astype(BF16))
        vband = band(regroup(v_ref).astype(BF16))
        res = []
        for hh in range(2):
            qh = (jnp.where(low, q, 0.0) if hh == 0 else jnp.where(low, 0.0, q)).astype(BF16)
            logit = lax.dot_general(qh, kband, (((2,), (2,)), ((0,), (0,))), preferred_element_type=F32)
            bias = bt_scr[pair, p, hh]
            if nb == 1:
                logit = logit + bias[:, BLK:][None]
            else:
                first = jnp.concatenate([neg_half, bias[:, BLK:]], axis=1)
                logit = jnp.concatenate([logit[:d] + first[None], logit[d:] + bias[None]], axis=0)
            m = jnp.max(logit, axis=-1, keepdims=True)
            e = jnp.exp(logit - m)
            s = jnp.sum(e, axis=-1, keepdims=True)
            o = lax.dot_general(e.astype(BF16), vband, (((2,), (1,)), ((0,), (0,))), preferred_element_type=F32)
            res.append((m, s, o))
        o_all = jnp.where(low, res[0][2], res[1][2])
        m_all = jnp.where(low, res[0][0], res[1][0])
        s_all = jnp.where(low, res[0][1], res[1][1])
        for bi, s0 in enumerate(starts):
            o_scr[p, rows_of(s0), :] = o_all[bi]
            m_scr[p, rows_of(s0), :] = m_all[bi]
            s_scr[p, rows_of(s0), :] = s_all[bi]

    m_all = jnp.maximum(jnp.maximum(m_scr[0], m_scr[1]), m_scr[2])
    num = jnp.zeros((seq, LANES), F32)
    den = jnp.zeros((seq, LANES), F32)
    for p in range(3):
        w = jnp.exp(m_scr[p] - m_all)
        num = num + w * o_scr[p]
        den = den + w * s_scr[p]
    o_ref[0] = (num / den).astype(BF16)


def _dilated_attention(proj, rel_bias, *, batch, seq):
    qi = np.arange(BLK)[:, None]
    kj = np.arange(2 * BLK)[None, :]
    step = BLK + qi - kj
    in_band = (step >= 0) & (step <= BLK)
    bk = np.stack([np.where(in_band, _np_bucket(step * d), -1) for d in DILATIONS]).astype(np.int32)
    pairs = A_HEADS // 2

    def col(t0):
        return pl.BlockSpec((1, seq, LANES), lambda b, p: (b, 0, t0 + p))

    return pl.pallas_call(
        functools.partial(_dil_kernel, seq=seq),
        grid=(batch, pairs),
        in_specs=[pl.BlockSpec(memory_space=pltpu.SMEM),
                  pl.BlockSpec((3, BLK, 2 * BLK), lambda b, p: (0, 0, 0)),
                  col(T_QA), col(T_KA), col(T_VA)],
        out_specs=pl.BlockSpec((1, seq, LANES), lambda b, p: (b, 0, p)),
        out_shape=jax.ShapeDtypeStruct((batch, seq, pairs * LANES), BF16),
        scratch_shapes=[pltpu.VMEM((pairs, 3, 2, BLK, 2 * BLK), F32),
                        pltpu.VMEM((3, seq, LANES), F32),
                        pltpu.VMEM((3, seq, LANES), F32),
                        pltpu.VMEM((3, seq, LANES), F32)],
        compiler_params=_cparams("arbitrary", "arbitrary"),
        name="dilated_attn",
    )(rel_bias, jnp.asarray(bk), proj, proj, proj)


def _mla_kernel(cq_ref, ckv_ref, kr_ref, krs_ref, c_ref, s_ref, qn_ref, kvn_ref,
                wq_ref, wqs_ref, wk_ref, wv_ref, o_ref, q_scr, k_scr, v_scr, *, seq):
    rc = 512
    for c0 in range(0, seq, rc):
        rows = slice(c0, c0 + rc)
        lane_r = lax.broadcasted_iota(jnp.int32, (rc, LANES), 1)
        rot = (lane_r >= MLA_NOPE) & (lane_r < MLA_NOPE + MLA_ROPE)
        cos = jnp.where(rot, c_ref[0, rows, :], 1.0)
        sin = jnp.where(rot, s_ref[0, rows, :], 0.0)
        nq = _rms(cq_ref[0, rows, :], qn_ref[...]).astype(BF16)
        nkv = _rms(ckv_ref[0, rows, :], kvn_ref[...]).astype(BF16)
        kr = kr_ref[0, rows, :] * cos + krs_ref[0, rows, :] * sin
        for h in range(B_HEADS):
            cols = slice(h * LANES, (h + 1) * LANES)
            q_scr[h, rows, :] = (_dot(nq, wq_ref[:, cols]) * cos + _dot(nq, wqs_ref[:, cols]) * sin).astype(BF16)
            k_scr[h, rows, :] = (_dot(nkv, wk_ref[:, cols]) + kr).astype(BF16)
            v = _dot(nkv, wv_ref[:, cols])
            lane_c = lax.broadcasted_iota(jnp.int32, v.shape, 1)
            v_scr[h, rows, :] = jnp.where(lane_c < HEAD_DIM, v, 1.0).astype(BF16)

    exp2_scale = (MLA_NOPE + MLA_ROPE) ** -0.5 * math.log2(math.e)
    causal = lax.broadcasted_iota(jnp.int32, (QB, QB), 1) <= lax.broadcasted_iota(jnp.int32, (QB, QB), 0)
    low = lax.broadcasted_iota(jnp.int32, (QB, LANES), 1) < HEAD_DIM
    blocks = [(qb, h) for qb in range(seq // QB) for h in range(B_HEADS)]

    def qk(qb, h):
        return _dot_t(q_scr[h, qb * QB:(qb + 1) * QB, :], k_scr[h, 0:(qb + 1) * QB, :])

    nxt = qk(*blocks[0])
    outs = []
    for idx, (qb, h) in enumerate(blocks):
        qs, ext = qb * QB, (qb + 1) * QB
        logit = nxt
        if idx + 1 < len(blocks):
            nxt = qk(*blocks[idx + 1])
        diag = jnp.where(causal, logit[:, qs:], NEG)
        logit = diag if qb == 0 else jnp.concatenate([logit[:, :qs], diag], axis=1)
        m = jnp.max(logit, axis=-1, keepdims=True)
        e = jnp.exp2((logit - m) * exp2_scale).astype(BF16)
        pv = _dot(e, v_scr[h, 0:ext, :])
        outs.append(jnp.where(low, pv / pltpu.roll(pv, HEAD_DIM, 1), 0.0))
        if h == B_HEADS - 1:
            for pt in range((B_HEADS + 1) // 2):
                pair = outs[2 * pt]
                if 2 * pt + 1 < B_HEADS:
                    pair = pair + pltpu.roll(outs[2 * pt + 1], HEAD_DIM, 1)
                o_ref[0, qs:qs + QB, pt * LANES:(pt + 1) * LANES] = pair.astype(BF16)
            outs = []


def _mla_attention(proj, cos_m, sin_m, q_norm, kv_norm, wq, wqs, wk, wv, *, batch, seq):
    def col(t0, n):
        return pl.BlockSpec((1, seq, n * LANES), lambda b: (b, 0, t0 // n))

    def full(a):
        return pl.BlockSpec(a.shape, lambda b: (0,) * a.ndim)

    tab = pl.BlockSpec((1, seq, LANES), lambda b: (b, 0, 0))
    width = ((B_HEADS + 1) // 2) * LANES
    return pl.pallas_call(
        functools.partial(_mla_kernel, seq=seq),
        grid=(batch,),
        in_specs=[col(T_CQ, 2), col(T_CKV, 1), col(T_KR, 1), col(T_KRS, 1), tab, tab,
                  full(q_norm), full(kv_norm), full(wq), full(wqs), full(wk), full(wv)],
        out_specs=pl.BlockSpec((1, seq, width), lambda b: (b, 0, 0)),
        out_shape=jax.ShapeDtypeStruct((batch, seq, width), BF16),
        scratch_shapes=[pltpu.VMEM((B_HEADS, seq, LANES), BF16)] * 3,
        compiler_params=_cparams("arbitrary"),
        name="mla_attn",
    )(proj, proj, proj, proj, cos_m, sin_m, q_norm, kv_norm, wq, wqs, wk, wv)


def _sel_kernel(iq_ref, iqs_ref, iw_ref, cq_ref, sq_ref, ik_ref, iks_ref, ck_ref, sk_ref, g_ref,
                o_ref, ik_scr, iqm_scr, key_scr, hb_scr, *, seq):
    i = pl.program_id(1)
    qs = i * QB
    nk = seq // QB

    @pl.when(i == 0)
    def _():
        x = ik_ref[0]
        r = lax.rsqrt(jnp.mean(x * x, axis=-1, keepdims=True) + RMS_EPS)
        y = x * r * g_ref[0:1, :]
        ys = iks_ref[0] * r * g_ref[1:2, :]
        ik_scr[...] = (y * ck_ref[0] + ys * sk_ref[0]).astype(BF16)

    lane = lax.broadcasted_iota(jnp.int32, (QB, LANES), 1)
    low = lane < IDX_DIM
    cos = cq_ref[0]
    sin = sq_ref[0]
    for t in range(IDX_HEADS // 2):
        cols = slice(t * LANES, (t + 1) * LANES)
        rot = iq_ref[0, :, cols] * cos + iqs_ref[0, :, cols] * sin
        iqm_scr[2 * t] = jnp.where(low, rot, 0.0).astype(BF16)
        iqm_scr[2 * t + 1] = jnp.where(low, 0.0, rot).astype(BF16)
    w_t = (iw_ref[0] * (IDX_HEADS ** -0.5 * IDX_DIM ** -0.5)).T

    krow = lax.broadcasted_iota(jnp.int32, (QB, QB), 0)
    qcol = lax.broadcasted_iota(jnp.int32, (QB, QB), 1)
    target = jnp.minimum(qs + lax.broadcasted_iota(jnp.int32, (1, QB), 1) + 1, TOPK).astype(F32)
    min_normal = float(np.finfo(np.float32).tiny)
    min_normal_bits = int(np.array(min_normal, np.float32).view(np.int32))
    hi_mask = -(1 << 16)
    one16, zero16 = jnp.ones((), BF16), jnp.zeros((), BF16)
    sub16 = QB // 16

    def block(n_tiles):
        iqm = iqm_scr[...].reshape(IDX_HEADS * QB, LANES)
        for kb in range(n_tiles):
            rel = _dot_t(ik_scr[kb * QB:(kb + 1) * QB, :], iqm)
            sc = jnp.zeros((QB, QB), F32)
            for h in range(IDX_HEADS):
                sc = sc + jnp.maximum(rel[:, h * QB:(h + 1) * QB], 0.0) * w_t[h:h + 1, :]
            sc = jnp.where(jnp.abs(sc) < min_normal, 0.0, sc)
            if kb == n_tiles - 1:
                sc = jnp.where(krow <= qcol, sc, -jnp.inf)
            bits = lax.bitcast_convert_type(sc, jnp.int32)
            key_scr[kb] = jnp.where(bits < 0, bits ^ KEY_FLIP, bits)
            hb_scr[kb] = lax.bitcast_convert_type(bits & hi_mask, F32).astype(BF16)

        def count_hi(cand16):
            acc = jnp.zeros((16, QB), BF16)
            for c in range(n_tiles):
                x = hb_scr[c].reshape(sub16, 16, QB)
                for r in range(sub16):
                    acc = acc + jnp.where(x[r] >= cand16, one16, zero16)
            return jnp.sum(acc.astype(F32), axis=0, keepdims=True)

        def count(pred):
            acc = jnp.zeros((8, QB), F32)
            for c in range(n_tiles):
                acc = acc + jnp.sum(pred(key_scr[c]).astype(F32).reshape(QB // 8, 8, QB), axis=0)
            return jnp.sum(acc, axis=0, keepdims=True)

        def hi_step(bi, prefix):
            cand_u = prefix | lax.shift_left(jnp.int32(1), 31 - bi)
            k = cand_u ^ INT_MIN
            cbits = jnp.where(k < 0, k ^ KEY_FLIP, k) & hi_mask
            cbits = jnp.where((cbits > 0) & (cbits < min_normal_bits), min_normal_bits, cbits)
            cand = lax.bitcast_convert_type(cbits, F32)
            cnt = count_hi(jnp.broadcast_to(cand, (16, QB)).astype(BF16))
            return jnp.where(cnt >= target, cand_u, prefix)

        def lo_step(bi, prefix):
            cand_u = prefix | lax.shift_left(jnp.int32(1), 15 - bi)
            cand = cand_u ^ INT_MIN
            cnt = count(lambda k: k >= cand)
            return jnp.where(cnt >= target, cand_u, prefix)

        prefix = lax.fori_loop(0, 16, hi_step, jnp.zeros((1, QB), jnp.int32))
        thr = lax.fori_loop(0, 16, lo_step, prefix) ^ INT_MIN
        need = target - count(lambda k: k > thr)

        lower = (qcol <= krow).astype(BF16)
        carry = jnp.zeros((1, QB), F32)
        for c in range(n_tiles):
            kc = key_scr[c]
            eq = kc == thr
            rank = carry + _dot(lower, eq.astype(BF16))
            sel = (kc > thr) | (eq & (rank <= need))
            o_ref[0, 0, c] = jnp.where(sel, 0.0, NEG).T.astype(BF16)
            carry = rank[QB - 1:QB, :]
        for c in range(n_tiles, nk):
            o_ref[0, 0, c] = jnp.full((QB, QB), NEG, BF16)

    for n_tiles in range(1, nk + 1):
        pl.when(i == n_tiles - 1)(functools.partial(block, n_tiles))


def _dsa_select(proj, cos_i, sin_i, ik_gains, *, batch, seq):
    nq = seq // QB

    def qcol(t0, n):
        return pl.BlockSpec((1, QB, n * LANES), lambda b, i: (b, i, t0 // n))

    def kcol(t0):
        return pl.BlockSpec((1, seq, LANES), lambda b, i: (b, 0, t0))

    qtab = pl.BlockSpec((1, QB, LANES), lambda b, i: (b, i, 0))
    ktab = pl.BlockSpec((1, seq, LANES), lambda b, i: (b, 0, 0))
    return pl.pallas_call(
        functools.partial(_sel_kernel, seq=seq),
        grid=(batch, nq),
        in_specs=[qcol(T_IQ, 4), qcol(T_IQS, 4), qcol(T_IW, 1), qtab, qtab,
                  kcol(T_IK), kcol(T_IKS), ktab, ktab,
                  pl.BlockSpec((2, LANES), lambda b, i: (0, 0))],
        out_specs=pl.BlockSpec((1, 1, nq, QB, QB), lambda b, i: (b, i, 0, 0, 0)),
        out_shape=jax.ShapeDtypeStruct((batch, nq, nq, QB, QB), BF16),
        scratch_shapes=[pltpu.VMEM((seq, LANES), BF16),
                        pltpu.VMEM((IDX_HEADS, QB, LANES), BF16),
                        pltpu.VMEM((nq, QB, QB), jnp.int32),
                        pltpu.VMEM((nq, QB, QB), BF16)],
        compiler_params=_cparams("arbitrary", "arbitrary"),
        name="dsa_select",
    )(proj, proj, proj, cos_i, sin_i, proj, proj, cos_i, sin_i, ik_gains)


def _dsa_kernel(tab_ref, bk_ref, q_ref, k_ref, v_ref, sel_ref, o_ref, kb_scr, vb_scr, bt_scr):
    b = pl.program_id(0)
    i = pl.program_id(1)

    n_pairs = (C_HEADS + 1) // 2
    heads_of = [min(2, C_HEADS - 2 * pt) for pt in range(n_pairs)]

    far = N_BUCKETS - 1

    @pl.when((b == 0) & (i == 0))
    def _():
        for pt in range(n_pairs):
            for hh in range(heads_of[pt]):
                col = A_HEADS + 2 * pt + hh
                for dd in range(2):
                    bt_scr[2 * pt + dd, hh * QB:(hh + 1) * QB, :] = (
                        _bias_from_buckets(bk_ref[dd], tab_ref, col) - tab_ref[far, col])

    @pl.when(i == 0)
    def _():
        kb_scr[...] = k_ref[0].astype(BF16)
        vb_scr[...] = v_ref[0].astype(BF16)

    lane = lax.broadcasted_iota(jnp.int32, (QB, LANES), 1)
    low = lane < HEAD_DIM

    def q_blocks(step):
        chains = [(2 * step + j, pt) for j in range(2) for pt in range(n_pairs)]

        def qk(qb, pt):
            j = qb - 2 * step
            cols = slice(pt * LANES, (pt + 1) * LANES)
            q = q_ref[0, j * QB:(j + 1) * QB, cols] * (HEAD_DIM ** -0.5)
            halves = [jnp.where(low, q, 0.0), jnp.where(low, 0.0, q)][:heads_of[pt]]
            return _dot_t(jnp.concatenate(halves, axis=0).astype(BF16), kb_scr[0:(qb + 1) * QB, cols])

        nxt = qk(*chains[0])
        for idx, (qb, pt) in enumerate(chains):
            j = qb - 2 * step
            ext = (qb + 1) * QB
            heads = heads_of[pt]
            rows = heads * QB
            cols = slice(pt * LANES, (pt + 1) * LANES)
            logit = nxt
            if idx + 1 < len(chains):
                nxt = qk(*chains[idx + 1])
            pieces = []
            for c in range(qb + 1):
                t = logit[:, c * QB:(c + 1) * QB] + jnp.concatenate([sel_ref[0, j, c].astype(F32)] * heads, axis=0)
                if qb - c < 2:
                    t = t + bt_scr[2 * pt + (qb - c), 0:rows, :]
                pieces.append(t)
            logit = pieces[0] if qb == 0 else jnp.concatenate(pieces, axis=1)
            m = jnp.max(logit, axis=-1, keepdims=True)
            e = jnp.exp(logit - m)
            s = jnp.sum(e, axis=-1, keepdims=True)
            out = _dot(e.astype(BF16), vb_scr[0:ext, cols]) / s
            upper = out[QB:] if heads == 2 else 0.0
            o_ref[0, j * QB:(j + 1) * QB, cols] = jnp.where(low, out[:QB], upper).astype(BF16)

    for step in range(sel_ref.shape[2] // 2):
        pl.when(i == step)(functools.partial(q_blocks, step))


def _dsa_attention(proj, sel, rel_bias, *, batch, seq):
    nq = seq // QB
    qi = np.arange(QB)[:, None]
    kj = np.arange(QB)[None, :]
    bk = np.stack([_np_bucket(qi - kj), _np_bucket(QB + qi - kj)]).astype(np.int32)
    assert _np_bucket(np.array(QB + 1)) == N_BUCKETS - 1
    width = ((C_HEADS + 1) // 2) * LANES
    return pl.pallas_call(
        _dsa_kernel,
        grid=(batch, nq // 2),
        in_specs=[pl.BlockSpec(memory_space=pltpu.SMEM),
                  pl.BlockSpec((2, QB, QB), lambda b, i: (0, 0, 0)),
                  pl.BlockSpec((1, 2 * QB, width), lambda b, i: (b, i, T_QC * LANES // width)),
                  pl.BlockSpec((1, seq, width), lambda b, i: (b, 0, T_KC * LANES // width)),
                  pl.BlockSpec((1, seq, width), lambda b, i: (b, 0, T_VC * LANES // width)),
                  pl.BlockSpec((1, 2, nq, QB, QB), lambda b, i: (b, i, 0, 0, 0))],
        out_specs=pl.BlockSpec((1, 2 * QB, width), lambda b, i: (b, i, 0)),
        out_shape=jax.ShapeDtypeStruct((batch, seq, width), BF16),
        scratch_shapes=[pltpu.VMEM((seq, width), BF16),
                        pltpu.VMEM((seq, width), BF16),
                        pltpu.VMEM((2 * ((C_HEADS + 1) // 2), 2 * QB, QB), F32)],
        compiler_params=_cparams("arbitrary", "arbitrary"),
        name="dsa_attn",
    )(rel_bias, jnp.asarray(bk), proj, proj, proj, sel)


def _swap_halves(w, rope):
    half = rope // 2
    return jnp.concatenate([-w[..., half:rope], w[..., :half], jnp.zeros_like(w[..., rope:])], axis=-1)


def _pad_cols(w, n):
    return jnp.concatenate([w, jnp.zeros(w.shape[:-1] + (n - w.shape[-1],), w.dtype)], axis=-1)


def _layout_w_in(w):
    D = w.shape[0]
    sizes = (A_HEADS * HEAD_DIM,) * 3 + (MLA_Q_RANK, MLA_KV_RANK, MLA_ROPE) + (C_HEADS * HEAD_DIM,) * 3 + (
        IDX_HEADS * IDX_DIM, IDX_DIM, IDX_HEADS)
    offs = np.cumsum((0,) + sizes)
    qa, ka, va, cq, ckv, kr, qc, kc, vc, iq, ik, iw = [w[:, offs[n]:offs[n + 1]] for n in range(len(sizes))]
    c_w = ((C_HEADS + 1) // 2) * LANES
    z64 = jnp.zeros((D, MLA_NOPE), w.dtype)
    iq_h = iq.reshape(D, IDX_HEADS, IDX_DIM)
    iq_sw = _swap_halves(iq_h, IDX_ROPE).reshape(D, IDX_HEADS * IDX_DIM)
    ik_sw = _swap_halves(ik, IDX_ROPE)
    cols = [qa, ka, va, _pad_cols(qc, c_w), _pad_cols(kc, c_w), _pad_cols(vc, c_w), cq, iq, iq_sw, ckv,
            _pad_cols(jnp.concatenate([z64, kr], axis=-1), LANES),
            _pad_cols(jnp.concatenate([z64, _swap_halves(kr, MLA_ROPE)], axis=-1), LANES),
            jnp.concatenate([ik, ik], axis=-1), jnp.concatenate([ik_sw, ik_sw], axis=-1), _pad_cols(iw, LANES)]
    out = jnp.concatenate(cols, axis=-1)
    assert out.shape[1] == N_TILES * LANES
    return out.astype(BF16)


def _layout_mla(w_uq, w_ukv):
    rq = w_uq.reshape(MLA_Q_RANK, B_HEADS, MLA_NOPE + MLA_ROPE)
    nope, rope = rq[..., :MLA_NOPE], rq[..., MLA_NOPE:]
    wq = _pad_cols(jnp.concatenate([nope, rope], axis=-1), LANES)
    wqs = _pad_cols(jnp.concatenate([jnp.zeros_like(nope), _swap_halves(rope, MLA_ROPE)], axis=-1), LANES)
    rkv = w_ukv.reshape(MLA_KV_RANK, B_HEADS, MLA_NOPE + HEAD_DIM)
    wk = _pad_cols(rkv[..., :MLA_NOPE], LANES)
    wv = _pad_cols(rkv[..., MLA_NOPE:], LANES)
    flat = lambda a, r: a.reshape(r, B_HEADS * LANES).astype(BF16)
    return flat(wq, MLA_Q_RANK), flat(wqs, MLA_Q_RANK), flat(wk, MLA_KV_RANK), flat(wv, MLA_KV_RANK)


def _layout_w_out(w):
    D = w.shape[1]
    a, b_, c_ = A_HEADS * HEAD_DIM, B_HEADS * HEAD_DIM, C_HEADS * HEAD_DIM
    def pad_rows(x, heads):
        return jnp.concatenate([x, jnp.zeros((((heads + 1) // 2) * LANES - x.shape[0], D), w.dtype)], axis=0)

    wa = w[:a]
    wb = pad_rows(w[a:a + b_], B_HEADS)
    wc = pad_rows(w[a + b_:a + b_ + c_], C_HEADS)
    return wa.astype(BF16), wc.astype(BF16), wb.astype(BF16)


def kernel(x, c, positions, rel_bias, ada_w, ada_b, norm_ffn1, ffn1_w_gu, ffn1_w_down, norm_mix, w_in, mla_q_norm, mla_w_uq, mla_kv_norm, mla_w_ukv, idx_k_norm, w_out, norm_ffn2, ffn2_w_gu, ffn2_w_down, final_norm):
    B, T, D = x.shape
    L = ada_w.shape[0]
    assert T % (max(DILATIONS) * BLK) == 0 and T % 512 == 0 and D % LANES == 0

    mods = _modulation(c, ada_w, ada_b).reshape(L, B, N_MOD, D)
    cos_i, sin_i = _rope_tables(positions)
    cos_m, sin_m = cos_i, sin_i
    final_gain = final_norm.reshape(1, D)
    h = x.reshape(B * T, D)
    for l in range(L):
        mod = mods[l]
        h = _ffn(h, mod, norm_ffn1[l].reshape(1, D), ffn1_w_gu[l].astype(BF16), ffn1_w_down[l].astype(BF16),
                 final_gain, ks=0, final=False, seq=T)

        proj = _in_proj(h, mod, norm_mix[l].reshape(1, D), _layout_w_in(w_in[l]), seq=T)
        proj = proj.reshape(B, T, N_TILES * LANES)
        oa = _dilated_attention(proj, rel_bias, batch=B, seq=T)
        wq, wqs, wk, wv = _layout_mla(mla_w_uq[l], mla_w_ukv[l])
        ob = _mla_attention(proj, cos_m, sin_m, mla_q_norm[l].reshape(1, -1), mla_kv_norm[l].reshape(1, -1),
                            wq, wqs, wk, wv, batch=B, seq=T)
        g = idx_k_norm[l]
        g_sw = jnp.concatenate([g[IDX_ROPE // 2:IDX_ROPE], g[:IDX_ROPE // 2], jnp.zeros_like(g[IDX_ROPE:])])
        ik_gains = jnp.stack([jnp.concatenate([g, g]), jnp.concatenate([g_sw, g_sw])])
        sel = _dsa_select(proj, cos_i, sin_i, ik_gains, batch=B, seq=T)
        oc = _dsa_attention(proj, sel, rel_bias, batch=B, seq=T)
        wa, wc, wb = _layout_w_out(w_out[l])
        h = _mix_ffn(oa.reshape(B * T, -1), oc.reshape(B * T, -1), ob.reshape(B * T, -1), wa, wc, wb, h, mod,
                     norm_ffn2[l].reshape(1, D), ffn2_w_gu[l].astype(BF16), ffn2_w_down[l].astype(BF16),
                     final_gain, final=(l == L - 1), seq=T)
    return h.reshape(B, T, D)
```

```python
import functools
import math

import numpy as np
import jax
import jax.numpy as jnp
from jax import lax
from jax.experimental import pallas as pl
from jax.experimental.pallas import tpu as pltpu

F32 = jnp.float32
BF16 = jnp.bfloat16

HEAD_DIM = 64
A_HEADS, B_HEADS, C_HEADS = 6, 5, 5
MLA_Q_RANK, MLA_KV_RANK, MLA_NOPE, MLA_ROPE = 256, 128, 64, 32
IDX_HEADS, IDX_DIM, IDX_ROPE = 8, 64, 32
TOPK = 256
N_BUCKETS, MAX_DISTANCE = 32, 128
ROPE_THETA = 10000.0
RMS_EPS = 1e-6
N_MOD = 9
NEG = -1e30
DILATIONS = (1, 4, 16)

LANES = 128
BLK = 128
QB = 256
FFN_ROWS = 1024
FFN_CHUNK = 256
PROJ_ROWS = 512
VMEM_LIMIT = 56 * 1024 * 1024

T_QA, T_KA, T_VA = 0, 3, 6
T_QC, T_KC, T_VC = 9, 12, 15
T_CQ, T_IQ, T_IQS = 18, 20, 24
T_CKV, T_KR, T_KRS, T_IK, T_IKS, T_IW = 28, 29, 30, 31, 32, 33
N_TILES = 34

KEY_FLIP = 0x7FFFFFFF
INT_MIN = -(2 ** 31)


def _cparams(*sem):
    return pltpu.CompilerParams(dimension_semantics=sem, vmem_limit_bytes=VMEM_LIMIT)


def _rms(x, g):
    return x * lax.rsqrt(jnp.mean(x * x, axis=-1, keepdims=True) + RMS_EPS) * g


def _dot(a, b):
    return jnp.dot(a, b, preferred_element_type=F32)


def _dot_t(a, b):
    return lax.dot_general(a, b, (((1,), (1,)), ((), ())), preferred_element_type=F32)


def _np_bucket(dist):
    n = np.maximum(dist, 0)
    max_exact = N_BUCKETS // 2
    nf = np.maximum(n, 1).astype(np.float32)
    large = max_exact + (np.log(nf / np.float32(max_exact)) / np.float32(math.log(MAX_DISTANCE / max_exact))
                         * np.float32(N_BUCKETS - max_exact)).astype(np.int32)
    large = np.minimum(large, N_BUCKETS - 1)
    return np.where(n < max_exact, n, large).astype(np.int32)


def _bias_from_buckets(bucket, tab_ref, col):
    acc = jnp.full(bucket.shape, NEG, F32)
    for b in range(N_BUCKETS):
        acc = jnp.where(bucket == b, tab_ref[b, col], acc)
    return acc


def _mod_kernel(c_ref, w_ref, b_ref, o_ref):
    c = c_ref[...]
    cond = c / (1.0 + jnp.exp(-c))
    w = w_ref[0]
    c_hi = cond.astype(BF16)
    c_lo = (cond - c_hi.astype(F32)).astype(BF16)
    w_hi = w.astype(BF16)
    w_lo = (w - w_hi.astype(F32)).astype(BF16)
    o_ref[0] = _dot(c_hi, w_hi) + _dot(c_hi, w_lo) + _dot(c_lo, w_hi) + b_ref[0]


def _modulation(c, ada_w, ada_b):
    L, D, N = ada_w.shape
    B = c.shape[0]
    tn = 1024
    return pl.pallas_call(
        _mod_kernel,
        grid=(L, N // tn),
        in_specs=[pl.BlockSpec((B, D), lambda l, j: (0, 0)),
                  pl.BlockSpec((1, D, tn), lambda l, j: (l, 0, j)),
                  pl.BlockSpec((1, 1, tn), lambda l, j: (l, 0, j))],
        out_specs=pl.BlockSpec((1, B, tn), lambda l, j: (l, 0, j)),
        out_shape=jax.ShapeDtypeStruct((L, B, N), F32),
        compiler_params=_cparams("arbitrary", "arbitrary"),
        name="adaln_mod",
    )(c, ada_w, ada_b.reshape(L, 1, N))


def _ffn_kernel(h_ref, mod_ref, g_ref, wgu_ref, wd_ref, fg_ref, o_ref, *, ks, final, tf):
    _ffn_body(h_ref[...], mod_ref, g_ref, wgu_ref, wd_ref, fg_ref, o_ref, ks=ks, final=final, tf=tf)


def _mix_ffn_kernel(oa_ref, oc_ref, ob_ref, wo_ref, h_ref, mod_ref, g_ref, wgu_ref, wd_ref, fg_ref,
                    o_ref, *, final, tf):
    heads = jnp.concatenate([oa_ref[...], oc_ref[...], ob_ref[...]], axis=1)
    x = h_ref[...] + mod_ref[0, 5:6, :] * _dot(heads, wo_ref[...])
    _ffn_body(x, mod_ref, g_ref, wgu_ref, wd_ref, fg_ref, o_ref, ks=6, final=final, tf=tf)


def _ffn_body(x, mod_ref, g_ref, wgu_ref, wd_ref, fg_ref, o_ref, *, ks, final, tf):
    F = wd_ref.shape[0]
    y = _rms(x, g_ref[...])
    n = (y * (1.0 + mod_ref[0, ks + 1:ks + 2, :]) + mod_ref[0, ks:ks + 1, :]).astype(BF16)
    acc = jnp.zeros(x.shape, F32)
    for f0 in range(0, F, tf):
        g = _dot(n, wgu_ref[:, f0:f0 + tf])
        u = _dot(n, wgu_ref[:, F + f0:F + f0 + tf])
        a = (g / (1.0 + jnp.exp(-g))) * u
        acc = acc + _dot(a.astype(BF16), wd_ref[f0:f0 + tf, :])
    out = x + (0.5 * mod_ref[0, ks + 2:ks + 3, :]) * acc
    if final:
        out = _rms(out, fg_ref[...])
    o_ref[...] = out


def _resident(a):
    return pl.BlockSpec(a.shape, lambda *_: (0,) * a.ndim, pipeline_mode=pl.Buffered(1))


def _ffn(h, mod, gain, w_gu, w_down, final_gain, *, ks, final, seq):
    M, D = h.shape
    F = w_down.shape[0]
    tm, tf = FFN_ROWS, FFN_CHUNK
    assert F % tf == 0 and seq % tm == 0
    per_b = seq // tm
    return pl.pallas_call(
        functools.partial(_ffn_kernel, ks=ks, final=final, tf=tf),
        grid=(M // tm,),
        in_specs=[pl.BlockSpec((tm, D), lambda i: (i, 0)),
                  pl.BlockSpec((1, N_MOD, D), lambda i: (i // per_b, 0, 0)),
                  pl.BlockSpec((1, D), lambda i: (0, 0)),
                  _resident(w_gu), _resident(w_down),
                  pl.BlockSpec((1, D), lambda i: (0, 0))],
        out_specs=pl.BlockSpec((tm, D), lambda i: (i, 0)),
        out_shape=jax.ShapeDtypeStruct((M, D), F32),
        compiler_params=_cparams("arbitrary"),
        name="ffn",
    )(h, mod, gain, w_gu, w_down, final_gain)


def _mix_ffn(oa, oc, ob, w_o, h, mod, gain, w_gu, w_down, final_gain, *, final, seq):
    M, D = h.shape
    tm, tf = FFN_ROWS, FFN_CHUNK
    assert w_down.shape[0] % tf == 0 and seq % tm == 0
    per_b = seq // tm

    def rows(a):
        return pl.BlockSpec((tm, a.shape[1]), lambda i: (i, 0))

    return pl.pallas_call(
        functools.partial(_mix_ffn_kernel, final=final, tf=tf),
        grid=(M // tm,),
        in_specs=[rows(oa), rows(oc), rows(ob), _resident(w_o), rows(h),
                  pl.BlockSpec((1, N_MOD, D), lambda i: (i // per_b, 0, 0)),
                  pl.BlockSpec((1, D), lambda i: (0, 0)),
                  _resident(w_gu), _resident(w_down),
                  pl.BlockSpec((1, D), lambda i: (0, 0))],
        out_specs=pl.BlockSpec((tm, D), lambda i: (i, 0)),
        out_shape=jax.ShapeDtypeStruct((M, D), F32),
        compiler_params=_cparams("arbitrary"),
        name="mix_ffn",
    )(oa, oc, ob, w_o, h, mod, gain, w_gu, w_down, final_gain)


def _win_kernel(h_ref, mod_ref, g_ref, w_ref, o_ref, *, tn):
    y = _rms(h_ref[...], g_ref[...])
    n = (y * (1.0 + mod_ref[0, 4:5, :]) + mod_ref[0, 3:4, :]).astype(BF16)
    for c0 in range(0, w_ref.shape[1], tn):
        o_ref[:, c0:c0 + tn] = _dot(n, w_ref[:, c0:c0 + tn])


def _in_proj(h, mod, gain, w, *, seq):
    M, D = h.shape
    N = w.shape[1]
    tm, tn = PROJ_ROWS, 2 * LANES
    assert N % tn == 0
    per_b = seq // tm
    return pl.pallas_call(
        functools.partial(_win_kernel, tn=tn),
        grid=(M // tm,),
        in_specs=[pl.BlockSpec((tm, D), lambda i: (i, 0)),
                  pl.BlockSpec((1, N_MOD, D), lambda i: (i // per_b, 0, 0)),
                  pl.BlockSpec((1, D), lambda i: (0, 0)),
                  _resident(w)],
        out_specs=pl.BlockSpec((tm, N), lambda i: (i, 0)),
        out_shape=jax.ShapeDtypeStruct((M, N), F32),
        compiler_params=_cparams("arbitrary"),
        name="in_proj",
    )(h, mod, gain, w)


def _rope_kernel(pos_ref, inv_ref, c_ref, s_ref):
    ang = pos_ref[0].astype(F32) * inv_ref[...]
    c_ref[0] = jnp.cos(ang)
    s_ref[0] = jnp.sin(ang)


def _rope_tables(positions):
    B, T = positions.shape
    half = IDX_ROPE // 2
    assert MLA_ROPE == IDX_ROPE and MLA_NOPE == IDX_DIM
    inv = (ROPE_THETA ** (-np.arange(half, dtype=np.float32) / half)).astype(np.float32)
    zeros32 = np.zeros(IDX_DIM - IDX_ROPE, np.float32)
    inv_row = jnp.asarray(np.concatenate([inv, inv, zeros32, inv, inv, zeros32])[None, :])
    tab = jax.ShapeDtypeStruct((B, T, LANES), F32)
    spec = pl.BlockSpec((1, T, LANES), lambda b: (b, 0, 0))
    return pl.pallas_call(
        _rope_kernel,
        grid=(B,),
        in_specs=[pl.BlockSpec((1, T, 1), lambda b: (b, 0, 0)),
                  pl.BlockSpec((1, LANES), lambda b: (0, 0))],
        out_specs=[spec, spec],
        out_shape=[tab, tab],
        compiler_params=_cparams("arbitrary"),
        name="rope_tables",
    )(positions.reshape(B, T, 1), inv_row)


def _dil_kernel(tab_ref, bk_ref, q_ref, k_ref, v_ref, o_ref, bt_scr, o_scr, m_scr, s_scr, *, seq):
    b = pl.program_id(0)
    pair = pl.program_id(1)

    @pl.when(b == 0)
    def _():
        for p in range(3):
            for hh in range(2):
                bt_scr[pair, p, hh] = _bias_from_buckets(bk_ref[p], tab_ref, 2 * pair + hh)

    nblk = seq // BLK
    low = lax.broadcasted_iota(jnp.int32, (nblk, BLK, LANES), 2) < HEAD_DIM
    neg_half = jnp.full((BLK, BLK), NEG, F32)

    for p, d in enumerate(DILATIONS):
        nb = seq // (d * BLK)
        starts = [r + d * BLK * n for n in range(nb) for r in range(d)]

        def rows_of(s0, d=d):
            return pl.ds(s0, BLK) if d == 1 else pl.ds(s0, BLK, stride=d)

        def regroup(ref):
            return jnp.stack([ref[0, rows_of(s0), :] for s0 in starts])

        def band(z):
            if nb == 1:
                return z
            prev = jnp.concatenate([jnp.zeros((d,) + z.shape[1:], z.dtype), z[:-d]], axis=0)
            return jnp.concatenate([prev, z], axis=1)

        q = regroup(q_ref) * (HEAD_DIM ** -0.5)
        kband = band(regroup(k_ref).astype(BF16))
        vband = band(regroup(v_ref).astype(BF16))
        res = []
        for hh in range(2):
            qh = (jnp.where(low, q, 0.0) if hh == 0 else jnp.where(low, 0.0, q)).astype(BF16)
            logit = lax.dot_general(qh, kband, (((2,), (2,)), ((0,), (0,))), preferred_element_type=F32)
            bias = bt_scr[pair, p, hh]
            if nb == 1:
                logit = logit + bias[:, BLK:][None]
            else:
                first = jnp.concatenate([neg_half, bias[:, BLK:]], axis=1)
                logit = jnp.concatenate([logit[:d] + first[None], logit[d:] + bias[None]], axis=0)
            m = jnp.max(logit, axis=-1, keepdims=True)
            e = jnp.exp(logit - m)
            s = jnp.sum(e, axis=-1, keepdims=True)
            o = lax.dot_general(e.astype(BF16), vband, (((2,), (1,)), ((0,), (0,))), preferred_element_type=F32)
            res.append((m, s, o))
        o_all = jnp.where(low, res[0][2], res[1][2])
        m_all = jnp.where(low, res[0][0], res[1][0])
        s_all = jnp.where(low, res[0][1], res[1][1])
        for bi, s0 in enumerate(starts):
            o_scr[p, rows_of(s0), :] = o_all[bi]
            m_scr[p, rows_of(s0), :] = m_all[bi]
            s_scr[p, rows_of(s0), :] = s_all[bi]

    m_all = jnp.maximum(jnp.maximum(m_scr[0], m_scr[1]), m_scr[2])
    num = jnp.zeros((seq, LANES), F32)
    den = jnp.zeros((seq, LANES), F32)
    for p in range(3):
        w = jnp.exp(m_scr[p] - m_all)
        num = num + w * o_scr[p]
        den = den + w * s_scr[p]
    o_ref[0] = (num / den).astype(BF16)


def _dilated_attention(proj, rel_bias, *, batch, seq):
    qi = np.arange(BLK)[:, None]
    kj = np.arange(2 * BLK)[None, :]
    step = BLK + qi - kj
    in_band = (step >= 0) & (step <= BLK)
    bk = np.stack([np.where(in_band, _np_bucket(step * d), -1) for d in DILATIONS]).astype(np.int32)
    pairs = A_HEADS // 2

    def col(t0):
        return pl.BlockSpec((1, seq, LANES), lambda b, p: (b, 0, t0 + p))

    return pl.pallas_call(
        functools.partial(_dil_kernel, seq=seq),
        grid=(batch, pairs),
        in_specs=[pl.BlockSpec(memory_space=pltpu.SMEM),
                  pl.BlockSpec((3, BLK, 2 * BLK), lambda b, p: (0, 0, 0)),
                  col(T_QA), col(T_KA), col(T_VA)],
        out_specs=pl.BlockSpec((1, seq, LANES), lambda b, p: (b, 0, p)),
        out_shape=jax.ShapeDtypeStruct((batch, seq, pairs * LANES), BF16),
        scratch_shapes=[pltpu.VMEM((pairs, 3, 2, BLK, 2 * BLK), F32),
                        pltpu.VMEM((3, seq, LANES), F32),
                        pltpu.VMEM((3, seq, LANES), F32),
                        pltpu.VMEM((3, seq, LANES), F32)],
        compiler_params=_cparams("arbitrary", "arbitrary"),
        name="dilated_attn",
    )(rel_bias, jnp.asarray(bk), proj, proj, proj)


def _mla_kernel(cq_ref, ckv_ref, kr_ref, krs_ref, c_ref, s_ref, qn_ref, kvn_ref,
                wq_ref, wqs_ref, wk_ref, wv_ref, o_ref, q_scr, k_scr, v_scr, *, seq):
    rc = 512
    for c0 in range(0, seq, rc):
        rows = slice(c0, c0 + rc)
        lane_r = lax.broadcasted_iota(jnp.int32, (rc, LANES), 1)
        rot = (lane_r >= MLA_NOPE) & (lane_r < MLA_NOPE + MLA_ROPE)
        cos = jnp.where(rot, c_ref[0, rows, :], 1.0)
        sin = jnp.where(rot, s_ref[0, rows, :], 0.0)
        nq = _rms(cq_ref[0, rows, :], qn_ref[...]).astype(BF16)
        nkv = _rms(ckv_ref[0, rows, :], kvn_ref[...]).astype(BF16)
        kr = kr_ref[0, rows, :] * cos + krs_ref[0, rows, :] * sin
        for h in range(B_HEADS):
            cols = slice(h * LANES, (h + 1) * LANES)
            q_scr[h, rows, :] = (_dot(nq, wq_ref[:, cols]) * cos + _dot(nq, wqs_ref[:, cols]) * sin).astype(BF16)
            k_scr[h, rows, :] = (_dot(nkv, wk_ref[:, cols]) + kr).astype(BF16)
            v = _dot(nkv, wv_ref[:, cols])
            lane_c = lax.broadcasted_iota(jnp.int32, v.shape, 1)
            v_scr[h, rows, :] = jnp.where(lane_c < HEAD_DIM, v, 1.0).astype(BF16)

    exp2_scale = (MLA_NOPE + MLA_ROPE) ** -0.5 * math.log2(math.e)
    causal = lax.broadcasted_iota(jnp.int32, (QB, QB), 1) <= lax.broadcasted_iota(jnp.int32, (QB, QB), 0)
    low = lax.broadcasted_iota(jnp.int32, (QB, LANES), 1) < HEAD_DIM
    blocks = [(qb, h) for qb in range(seq // QB) for h in range(B_HEADS)]

    def qk(qb, h):
        return _dot_t(q_scr[h, qb * QB:(qb + 1) * QB, :], k_scr[h, 0:(qb + 1) * QB, :])

    nxt = qk(*blocks[0])
    outs = []
    for idx, (qb, h) in enumerate(blocks):
        qs, ext = qb * QB, (qb + 1) * QB
        logit = nxt
        if idx + 1 < len(blocks):
            nxt = qk(*blocks[idx + 1])
        diag = jnp.where(causal, logit[:, qs:], NEG)
        logit = diag if qb == 0 else jnp.concatenate([logit[:, :qs], diag], axis=1)
        m = jnp.max(logit, axis=-1, keepdims=True)
        e = jnp.exp2((logit - m) * exp2_scale).astype(BF16)
        pv = _dot(e, v_scr[h, 0:ext, :])
        outs.append(jnp.where(low, pv / pltpu.roll(pv, HEAD_DIM, 1), 0.0))
        if h == B_HEADS - 1:
            for pt in range((B_HEADS + 1) // 2):
                pair = outs[2 * pt]
                if 2 * pt + 1 < B_HEADS:
                    pair = pair + pltpu.roll(outs[2 * pt + 1], HEAD_DIM, 1)
                o_ref[0, qs:qs + QB, pt * LANES:(pt + 1) * LANES] = pair.astype(BF16)
            outs = []


def _mla_attention(proj, cos_m, sin_m, q_norm, kv_norm, wq, wqs, wk, wv, *, batch, seq):
    def col(t0, n):
        return pl.BlockSpec((1, seq, n * LANES), lambda b: (b, 0, t0 // n))

    def full(a):
        return pl.BlockSpec(a.shape, lambda b: (0,) * a.ndim)

    tab = pl.BlockSpec((1, seq, LANES), lambda b: (b, 0, 0))
    width = ((B_HEADS + 1) // 2) * LANES
    return pl.pallas_call(
        functools.partial(_mla_kernel, seq=seq),
        grid=(batch,),
        in_specs=[col(T_CQ, 2), col(T_CKV, 1), col(T_KR, 1), col(T_KRS, 1), tab, tab,
                  full(q_norm), full(kv_norm), full(wq), full(wqs), full(wk), full(wv)],
        out_specs=pl.BlockSpec((1, seq, width), lambda b: (b, 0, 0)),
        out_shape=jax.ShapeDtypeStruct((batch, seq, width), BF16),
        scratch_shapes=[pltpu.VMEM((B_HEADS, seq, LANES), BF16)] * 3,
        compiler_params=_cparams("arbitrary"),
        name="mla_attn",
    )(proj, proj, proj, proj, cos_m, sin_m, q_norm, kv_norm, wq, wqs, wk, wv)


def _sel_kernel(*refs, seq):
    for j in range(2):
        _sel_block(j, *refs, seq=seq)


def _sel_block(j, iq_ref, iqs_ref, iw_ref, cq_ref, sq_ref, ik_ref, iks_ref, ck_ref, sk_ref, g_ref,
               o_ref, ik_scr, iqm_scr, key_scr, hb_scr, *, seq):
    i = 2 * pl.program_id(1) + j
    qrows = slice(j * QB, (j + 1) * QB)
    qs = i * QB
    nk = seq // QB

    @pl.when(i == 0)
    def _():
        x = ik_ref[0]
        r = lax.rsqrt(jnp.mean(x * x, axis=-1, keepdims=True) + RMS_EPS)
        y = x * r * g_ref[0:1, :]
        ys = iks_ref[0] * r * g_ref[1:2, :]
        ik_scr[...] = (y * ck_ref[0] + ys * sk_ref[0]).astype(BF16)

    lane = lax.broadcasted_iota(jnp.int32, (QB, LANES), 1)
    low = lane < IDX_DIM
    cos = cq_ref[0, qrows, :]
    sin = sq_ref[0, qrows, :]
    for t in range(IDX_HEADS // 2):
        cols = slice(t * LANES, (t + 1) * LANES)
        rot = iq_ref[0, qrows, cols] * cos + iqs_ref[0, qrows, cols] * sin
        iqm_scr[2 * t] = jnp.where(low, rot, 0.0).astype(BF16)
        iqm_scr[2 * t + 1] = jnp.where(low, 0.0, rot).astype(BF16)
    w_t = (iw_ref[0, qrows, :] * (IDX_HEADS ** -0.5 * IDX_DIM ** -0.5)).T

    krow = lax.broadcasted_iota(jnp.int32, (QB, QB), 0)
    qcol = lax.broadcasted_iota(jnp.int32, (QB, QB), 1)
    target = jnp.minimum(qs + lax.broadcasted_iota(jnp.int32, (1, QB), 1) + 1, TOPK).astype(F32)
    min_normal = float(np.finfo(np.float32).tiny)
    min_normal_bits = int(np.array(min_normal, np.float32).view(np.int32))
    hi_mask = -(1 << 16)
    one16, zero16 = jnp.ones((), BF16), jnp.zeros((), BF16)
    sub16 = QB // 16

    def block(n_tiles):
        iqm = iqm_scr[...].reshape(IDX_HEADS * QB, LANES)
        for kb in range(n_tiles):
            rel = _dot_t(ik_scr[kb * QB:(kb + 1) * QB, :], iqm)
            sc = jnp.zeros((QB, QB), F32)
            for h in range(IDX_HEADS):
                sc = sc + jnp.maximum(rel[:, h * QB:(h + 1) * QB], 0.0) * w_t[h:h + 1, :]
            sc = jnp.where(jnp.abs(sc) < min_normal, 0.0, sc)
            if kb == n_tiles - 1:
                sc = jnp.where(krow <= qcol, sc, -jnp.inf)
            bits = lax.bitcast_convert_type(sc, jnp.int32)
            key_scr[kb] = jnp.where(bits < 0, bits ^ KEY_FLIP, bits)
            hb_scr[kb] = lax.bitcast_convert_type(bits & hi_mask, F32).astype(BF16)

        def count_hi(cand16):
            acc = jnp.zeros((16, QB), BF16)
            for c in range(n_tiles):
                x = hb_scr[c].reshape(sub16, 16, QB)
                for r in range(sub16):
                    acc = acc + jnp.where(x[r] >= cand16, one16, zero16)
            return jnp.sum(acc.astype(F32), axis=0, keepdims=True)

        def count(pred):
            acc = jnp.zeros((8, QB), F32)
            for c in range(n_tiles):
                acc = acc + jnp.sum(pred(key_scr[c]).astype(F32).reshape(QB // 8, 8, QB), axis=0)
            return jnp.sum(acc, axis=0, keepdims=True)

        def hi_step(bi, prefix):
            cand_u = prefix | lax.shift_left(jnp.int32(1), 31 - bi)
            k = cand_u ^ INT_MIN
            cbits = jnp.where(k < 0, k ^ KEY_FLIP, k) & hi_mask
            cbits = jnp.where((cbits > 0) & (cbits < min_normal_bits), min_normal_bits, cbits)
            cand = lax.bitcast_convert_type(cbits, F32)
            cnt = count_hi(jnp.broadcast_to(cand, (16, QB)).astype(BF16))
            return jnp.where(cnt >= target, cand_u, prefix)

        def lo_step(bi, prefix):
            cand_u = prefix | lax.shift_left(jnp.int32(1), 15 - bi)
            cand = cand_u ^ INT_MIN
            cnt = count(lambda k: k >= cand)
            return jnp.where(cnt >= target, cand_u, prefix)

        prefix = lax.fori_loop(0, 16, hi_step, jnp.zeros((1, QB), jnp.int32))
        thr = lax.fori_loop(0, 16, lo_step, prefix) ^ INT_MIN
        need = target - count(lambda k: k > thr)

        lower = (qcol <= krow).astype(BF16)
        carry = jnp.zeros((1, QB), F32)
        for c in range(n_tiles):
            kc = key_scr[c]
            eq = kc == thr
            rank = carry + _dot(lower, eq.astype(BF16))
            sel = (kc > thr) | (eq & (rank <= need))
            o_ref[0, j, c] = jnp.where(sel, 0.0, NEG).T.astype(BF16)
            carry = rank[QB - 1:QB, :]
        for c in range(n_tiles, nk):
            o_ref[0, j, c] = jnp.full((QB, QB), NEG, BF16)

    for n_tiles in range(1 + j, nk + 1, 2):
        pl.when(i == n_tiles - 1)(functools.partial(block, n_tiles))


def _dsa_select(proj, cos_i, sin_i, ik_gains, *, batch, seq):
    nq = seq // QB

    assert nq % 2 == 0

    def qcol(t0, n):
        return pl.BlockSpec((1, 2 * QB, n * LANES), lambda b, i: (b, i, t0 // n))

    def kcol(t0):
        return pl.BlockSpec((1, seq, LANES), lambda b, i: (b, 0, t0))

    qtab = pl.BlockSpec((1, 2 * QB, LANES), lambda b, i: (b, i, 0))
    ktab = pl.BlockSpec((1, seq, LANES), lambda b, i: (b, 0, 0))
    return pl.pallas_call(
        functools.partial(_sel_kernel, seq=seq),
        grid=(batch, nq // 2),
        in_specs=[qcol(T_IQ, 4), qcol(T_IQS, 4), qcol(T_IW, 1), qtab, qtab,
                  kcol(T_IK), kcol(T_IKS), ktab, ktab,
                  pl.BlockSpec((2, LANES), lambda b, i: (0, 0))],
        out_specs=pl.BlockSpec((1, 2, nq, QB, QB), lambda b, i: (b, i, 0, 0, 0)),
        out_shape=jax.ShapeDtypeStruct((batch, nq, nq, QB, QB), BF16),
        scratch_shapes=[pltpu.VMEM((seq, LANES), BF16),
                        pltpu.VMEM((IDX_HEADS, QB, LANES), BF16),
                        pltpu.VMEM((nq, QB, QB), jnp.int32),
                        pltpu.VMEM((nq, QB, QB), BF16)],
        compiler_params=_cparams("arbitrary", "arbitrary"),
        name="dsa_select",
    )(proj, proj, proj, cos_i, sin_i, proj, proj, cos_i, sin_i, ik_gains)


def _dsa_kernel(tab_ref, bk_ref, q_ref, k_ref, v_ref, sel_ref, o_ref, kb_scr, vb_scr, bt_scr):
    b = pl.program_id(0)
    i = pl.program_id(1)

    n_pairs = (C_HEADS + 1) // 2
    heads_of = [min(2, C_HEADS - 2 * pt) for pt in range(n_pairs)]

    far = N_BUCKETS - 1

    @pl.when((b == 0) & (i == 0))
    def _():
        for pt in range(n_pairs):
            for hh in range(heads_of[pt]):
                col = A_HEADS + 2 * pt + hh
                for dd in range(2):
                    bt_scr[2 * pt + dd, hh * QB:(hh + 1) * QB, :] = (
                        _bias_from_buckets(bk_ref[dd], tab_ref, col) - tab_ref[far, col])

    @pl.when(i == 0)
    def _():
        kb_scr[...] = k_ref[0].astype(BF16)
        vb_scr[...] = v_ref[0].astype(BF16)

    lane = lax.broadcasted_iota(jnp.int32, (QB, LANES), 1)
    low = lane < HEAD_DIM

    def q_blocks(step):
        chains = [(2 * step + j, pt) for j in range(2) for pt in range(n_pairs)]

        def qk(qb, pt):
            j = qb - 2 * step
            cols = slice(pt * LANES, (pt + 1) * LANES)
            q = q_ref[0, j * QB:(j + 1) * QB, cols] * (HEAD_DIM ** -0.5)
            halves = [jnp.where(low, q, 0.0), jnp.where(low, 0.0, q)][:heads_of[pt]]
            return _dot_t(jnp.concatenate(halves, axis=0).astype(BF16), kb_scr[0:(qb + 1) * QB, cols])

        nxt = qk(*chains[0])
        for idx, (qb, pt) in enumerate(chains):
            j = qb - 2 * step
            ext = (qb + 1) * QB
            heads = heads_of[pt]
            rows = heads * QB
            cols = slice(pt * LANES, (pt + 1) * LANES)
            logit = nxt
            if idx + 1 < len(chains):
                nxt = qk(*chains[idx + 1])
            pieces = []
            for c in range(qb + 1):
                t = logit[:, c * QB:(c + 1) * QB] + jnp.concatenate([sel_ref[0, j, c].astype(F32)] * heads, axis=0)
                if qb - c < 2:
                    t = t + bt_scr[2 * pt + (qb - c), 0:rows, :]
                pieces.append(t)
            logit = pieces[0] if qb == 0 else jnp.concatenate(pieces, axis=1)
            m = jnp.max(logit, axis=-1, keepdims=True)
            e = jnp.exp(logit - m)
            s = jnp.sum(e, axis=-1, keepdims=True)
            out = _dot(e.astype(BF16), vb_scr[0:ext, cols]) / s
            upper = out[QB:] if heads == 2 else 0.0
            o_ref[0, j * QB:(j + 1) * QB, cols] = jnp.where(low, out[:QB], upper).astype(BF16)

    for step in range(sel_ref.shape[2] // 2):
        pl.when(i == step)(functools.partial(q_blocks, step))


def _dsa_attention(proj, sel, rel_bias, *, batch, seq):
    nq = seq // QB
    qi = np.arange(QB)[:, None]
    kj = np.arange(QB)[None, :]
    bk = np.stack([_np_bucket(qi - kj), _np_bucket(QB + qi - kj)]).astype(np.int32)
    assert _np_bucket(np.array(QB + 1)) == N_BUCKETS - 1
    width = ((C_HEADS + 1) // 2) * LANES
    return pl.pallas_call(
        _dsa_kernel,
        grid=(batch, nq // 2),
        in_specs=[pl.BlockSpec(memory_space=pltpu.SMEM),
                  pl.BlockSpec((2, QB, QB), lambda b, i: (0, 0, 0)),
                  pl.BlockSpec((1, 2 * QB, width), lambda b, i: (b, i, T_QC * LANES // width)),
                  pl.BlockSpec((1, seq, width), lambda b, i: (b, 0, T_KC * LANES // width)),
                  pl.BlockSpec((1, seq, width), lambda b, i: (b, 0, T_VC * LANES // width)),
                  pl.BlockSpec((1, 2, nq, QB, QB), lambda b, i: (b, i, 0, 0, 0))],
        out_specs=pl.BlockSpec((1, 2 * QB, width), lambda b, i: (b, i, 0)),
        out_shape=jax.ShapeDtypeStruct((batch, seq, width), BF16),
        scratch_shapes=[pltpu.VMEM((seq, width), BF16),
                        pltpu.VMEM((seq, width), BF16),
                        pltpu.VMEM((2 * ((C_HEADS + 1) // 2), 2 * QB, QB), F32)],
        compiler_params=_cparams("arbitrary", "arbitrary"),
        name="dsa_attn",
    )(rel_bias, jnp.asarray(bk), proj, proj, proj, sel)


def _swap_halves(w, rope):
    half = rope // 2
    return jnp.concatenate([-w[..., half:rope], w[..., :half], jnp.zeros_like(w[..., rope:])], axis=-1)


def _pad_cols(w, n):
    return jnp.concatenate([w, jnp.zeros(w.shape[:-1] + (n - w.shape[-1],), w.dtype)], axis=-1)


def _layout_w_in(w):
    D = w.shape[0]
    sizes = (A_HEADS * HEAD_DIM,) * 3 + (MLA_Q_RANK, MLA_KV_RANK, MLA_ROPE) + (C_HEADS * HEAD_DIM,) * 3 + (
        IDX_HEADS * IDX_DIM, IDX_DIM, IDX_HEADS)
    offs = np.cumsum((0,) + sizes)
    qa, ka, va, cq, ckv, kr, qc, kc, vc, iq, ik, iw = [w[:, offs[n]:offs[n + 1]] for n in range(len(sizes))]
    c_w = ((C_HEADS + 1) // 2) * LANES
    z64 = jnp.zeros((D, MLA_NOPE), w.dtype)
    iq_h = iq.reshape(D, IDX_HEADS, IDX_DIM)
    iq_sw = _swap_halves(iq_h, IDX_ROPE).reshape(D, IDX_HEADS * IDX_DIM)
    ik_sw = _swap_halves(ik, IDX_ROPE)
    cols = [qa, ka, va, _pad_cols(qc, c_w), _pad_cols(kc, c_w), _pad_cols(vc, c_w), cq, iq, iq_sw, ckv,
            _pad_cols(jnp.concatenate([z64, kr], axis=-1), LANES),
            _pad_cols(jnp.concatenate([z64, _swap_halves(kr, MLA_ROPE)], axis=-1), LANES),
            jnp.concatenate([ik, ik], axis=-1), jnp.concatenate([ik_sw, ik_sw], axis=-1), _pad_cols(iw, LANES)]
    out = jnp.concatenate(cols, axis=-1)
    assert out.shape[1] == N_TILES * LANES
    return out.astype(BF16)


def _layout_mla(w_uq, w_ukv):
    rq = w_uq.reshape(MLA_Q_RANK, B_HEADS, MLA_NOPE + MLA_ROPE)
    nope, rope = rq[..., :MLA_NOPE], rq[..., MLA_NOPE:]
    wq = _pad_cols(jnp.concatenate([nope, rope], axis=-1), LANES)
    wqs = _pad_cols(jnp.concatenate([jnp.zeros_like(nope), _swap_halves(rope, MLA_ROPE)], axis=-1), LANES)
    rkv = w_ukv.reshape(MLA_KV_RANK, B_HEADS, MLA_NOPE + HEAD_DIM)
    wk = _pad_cols(rkv[..., :MLA_NOPE], LANES)
    wv = _pad_cols(rkv[..., MLA_NOPE:], LANES)
    flat = lambda a, r: a.reshape(r, B_HEADS * LANES).astype(BF16)
    return flat(wq, MLA_Q_RANK), flat(wqs, MLA_Q_RANK), flat(wk, MLA_KV_RANK), flat(wv, MLA_KV_RANK)


def _layout_w_out(w):
    D = w.shape[1]
    a, b_, c_ = A_HEADS * HEAD_DIM, B_HEADS * HEAD_DIM, C_HEADS * HEAD_DIM
    def pad_rows(x, heads):
        return jnp.concatenate([x, jnp.zeros((((heads + 1) // 2) * LANES - x.shape[0], D), w.dtype)], axis=0)

    wa = w[:a]
    wb = pad_rows(w[a:a + b_], B_HEADS)
    wc = pad_rows(w[a + b_:a + b_ + c_], C_HEADS)
    return jnp.concatenate([wa, wc, wb], axis=0).astype(BF16)


def kernel(x, c, positions, rel_bias, ada_w, ada_b, norm_ffn1, ffn1_w_gu, ffn1_w_down, norm_mix, w_in, mla_q_norm, mla_w_uq, mla_kv_norm, mla_w_ukv, idx_k_norm, w_out, norm_ffn2, ffn2_w_gu, ffn2_w_down, final_norm):
    B, T, D = x.shape
    L = ada_w.shape[0]
    assert T % (max(DILATIONS) * BLK) == 0 and T % 512 == 0 and D % LANES == 0

    mods = _modulation(c, ada_w, ada_b).reshape(L, B, N_MOD, D)
    cos_i, sin_i = _rope_tables(positions)
    cos_m, sin_m = cos_i, sin_i
    final_gain = final_norm.reshape(1, D)
    h = x.reshape(B * T, D)
    for l in range(L):
        mod = mods[l]
        h = _ffn(h, mod, norm_ffn1[l].reshape(1, D), ffn1_w_gu[l].astype(BF16), ffn1_w_down[l].astype(BF16),
                 final_gain, ks=0, final=False, seq=T)

        proj = _in_proj(h, mod, norm_mix[l].reshape(1, D), _layout_w_in(w_in[l]), seq=T)
        proj = proj.reshape(B, T, N_TILES * LANES)
        oa = _dilated_attention(proj, rel_bias, batch=B, seq=T)
        wq, wqs, wk, wv = _layout_mla(mla_w_uq[l], mla_w_ukv[l])
        ob = _mla_attention(proj, cos_m, sin_m, mla_q_norm[l].reshape(1, -1), mla_kv_norm[l].reshape(1, -1),
                            wq, wqs, wk, wv, batch=B, seq=T)
        g = idx_k_norm[l]
        g_sw = jnp.concatenate([g[IDX_ROPE // 2:IDX_ROPE], g[:IDX_ROPE // 2], jnp.zeros_like(g[IDX_ROPE:])])
        ik_gains = jnp.stack([jnp.concatenate([g, g]), jnp.concatenate([g_sw, g_sw])])
        sel = _dsa_select(proj, cos_i, sin_i, ik_gains, batch=B, seq=T)
        oc = _dsa_attention(proj, sel, rel_bias, batch=B, seq=T)
        h = _mix_ffn(oa.reshape(B * T, -1), oc.reshape(B * T, -1), ob.reshape(B * T, -1), _layout_w_out(w_out[l]), h, mod,
                     norm_ffn2[l].reshape(1, D), ffn2_w_gu[l].astype(BF16), ffn2_w_down[l].astype(BF16),
                     final_gain, final=(l == L - 1), seq=T)
    return h.reshape(B, T, D)
```

```python
import functools
import math

import numpy as np
import jax
import jax.numpy as jnp
from jax import lax
from jax.experimental import pallas as pl
from jax.experimental.pallas import tpu as pltpu

F32 = jnp.float32
BF16 = jnp.bfloat16

HEAD_DIM = 64
A_HEADS, B_HEADS, C_HEADS = 6, 5, 5
MLA_Q_RANK, MLA_KV_RANK, MLA_NOPE, MLA_ROPE = 256, 128, 64, 32
IDX_HEADS, IDX_DIM, IDX_ROPE = 8, 64, 32
TOPK = 256
N_BUCKETS, MAX_DISTANCE = 32, 128
ROPE_THETA = 10000.0
RMS_EPS = 1e-6
N_MOD = 9
NEG = -1e30
DILATIONS = (1, 4, 16)

LANES = 128
BLK = 128
QB = 256
FFN_ROWS = 1024
FFN_CHUNK = 256
PROJ_ROWS = 512
VMEM_LIMIT = 56 * 1024 * 1024

T_QA, T_KA, T_VA = 0, 3, 6
T_QC, T_KC, T_VC = 9, 12, 15
T_CQ, T_IQ, T_IQS = 18, 20, 24
T_CKV, T_SMALL, T_IK = 28, 29, 30
N_TILES = 31
L_IKS, L_IW, L_KR, L_KRS = 0, 32, 64, 96

KEY_FLIP = 0x7FFFFFFF
INT_MIN = -(2 ** 31)


def _cparams(*sem):
    return pltpu.CompilerParams(dimension_semantics=sem, vmem_limit_bytes=VMEM_LIMIT)


def _rms(x, g):
    return x * lax.rsqrt(jnp.mean(x * x, axis=-1, keepdims=True) + RMS_EPS) * g


def _dot(a, b):
    return jnp.dot(a, b, preferred_element_type=F32)


def _dot_t(a, b):
    return lax.dot_general(a, b, (((1,), (1,)), ((), ())), preferred_element_type=F32)


def _np_bucket(dist):
    n = np.maximum(dist, 0)
    max_exact = N_BUCKETS // 2
    nf = np.maximum(n, 1).astype(np.float32)
    large = max_exact + (np.log(nf / np.float32(max_exact)) / np.float32(math.log(MAX_DISTANCE / max_exact))
                         * np.float32(N_BUCKETS - max_exact)).astype(np.int32)
    large = np.minimum(large, N_BUCKETS - 1)
    return np.where(n < max_exact, n, large).astype(np.int32)


def _bias_from_buckets(bucket, tab_ref, col):
    acc = jnp.full(bucket.shape, NEG, F32)
    for b in range(N_BUCKETS):
        acc = jnp.where(bucket == b, tab_ref[b, col], acc)
    return acc


def _mod_kernel(c_ref, w_ref, b_ref, o_ref):
    c = c_ref[...]
    cond = c / (1.0 + jnp.exp(-c))
    w = w_ref[0]
    c_hi = cond.astype(BF16)
    c_lo = (cond - c_hi.astype(F32)).astype(BF16)
    w_hi = w.astype(BF16)
    w_lo = (w - w_hi.astype(F32)).astype(BF16)
    o_ref[0] = _dot(c_hi, w_hi) + _dot(c_hi, w_lo) + _dot(c_lo, w_hi) + b_ref[0]


def _modulation(c, ada_w, ada_b):
    L, D, N = ada_w.shape
    B = c.shape[0]
    tn = 1024
    return pl.pallas_call(
        _mod_kernel,
        grid=(L, N // tn),
        in_specs=[pl.BlockSpec((B, D), lambda l, j: (0, 0)),
                  pl.BlockSpec((1, D, tn), lambda l, j: (l, 0, j)),
                  pl.BlockSpec((1, 1, tn), lambda l, j: (l, 0, j))],
        out_specs=pl.BlockSpec((1, B, tn), lambda l, j: (l, 0, j)),
        out_shape=jax.ShapeDtypeStruct((L, B, N), F32),
        compiler_params=_cparams("arbitrary", "arbitrary"),
        name="adaln_mod",
    )(c, ada_w, ada_b.reshape(L, 1, N))


def _ffn_kernel(h_ref, mod_ref, g_ref, wgu_ref, wd_ref, fg_ref, o_ref, *, ks, final, tf):
    _ffn_body(h_ref[...], mod_ref, g_ref, wgu_ref, wd_ref, fg_ref, o_ref, ks=ks, final=final, tf=tf)


def _mix_ffn_kernel(oa_ref, oc_ref, ob_ref, wo_ref, h_ref, mod_ref, g_ref, wgu_ref, wd_ref, fg_ref,
                    o_ref, *, final, tf):
    heads = jnp.concatenate([oa_ref[...], oc_ref[...], ob_ref[...]], axis=1)
    x = h_ref[...] + mod_ref[0, 5:6, :] * _dot(heads, wo_ref[...])
    _ffn_body(x, mod_ref, g_ref, wgu_ref, wd_ref, fg_ref, o_ref, ks=6, final=final, tf=tf)


def _ffn_body(x, mod_ref, g_ref, wgu_ref, wd_ref, fg_ref, o_ref, *, ks, final, tf):
    F = wd_ref.shape[0]
    y = _rms(x, g_ref[...])
    n = (y * (1.0 + mod_ref[0, ks + 1:ks + 2, :]) + mod_ref[0, ks:ks + 1, :]).astype(BF16)
    acc = jnp.zeros(x.shape, F32)
    for f0 in range(0, F, tf):
        g = _dot(n, wgu_ref[:, f0:f0 + tf])
        u = _dot(n, wgu_ref[:, F + f0:F + f0 + tf])
        a = (g / (1.0 + jnp.exp(-g))) * u
        acc = acc + _dot(a.astype(BF16), wd_ref[f0:f0 + tf, :])
    out = x + (0.5 * mod_ref[0, ks + 2:ks + 3, :]) * acc
    if final:
        out = _rms(out, fg_ref[...])
    o_ref[...] = out


def _resident(a):
    return pl.BlockSpec(a.shape, lambda *_: (0,) * a.ndim, pipeline_mode=pl.Buffered(1))


def _ffn(h, mod, gain, w_gu, w_down, final_gain, *, ks, final, seq):
    M, D = h.shape
    F = w_down.shape[0]
    tm, tf = FFN_ROWS, FFN_CHUNK
    assert F % tf == 0 and seq % tm == 0
    per_b = seq // tm
    return pl.pallas_call(
        functools.partial(_ffn_kernel, ks=ks, final=final, tf=tf),
        grid=(M // tm,),
        in_specs=[pl.BlockSpec((tm, D), lambda i: (i, 0)),
                  pl.BlockSpec((1, N_MOD, D), lambda i: (i // per_b, 0, 0)),
                  pl.BlockSpec((1, D), lambda i: (0, 0)),
                  _resident(w_gu), _resident(w_down),
                  pl.BlockSpec((1, D), lambda i: (0, 0))],
        out_specs=pl.BlockSpec((tm, D), lambda i: (i, 0)),
        out_shape=jax.ShapeDtypeStruct((M, D), F32),
        compiler_params=_cparams("arbitrary"),
        name="ffn",
    )(h, mod, gain, w_gu, w_down, final_gain)


def _mix_ffn(oa, oc, ob, w_o, h, mod, gain, w_gu, w_down, final_gain, *, final, seq):
    M, D = h.shape
    tm, tf = FFN_ROWS, FFN_CHUNK
    assert w_down.shape[0] % tf == 0 and seq % tm == 0
    per_b = seq // tm

    def rows(a):
        return pl.BlockSpec((tm, a.shape[1]), lambda i: (i, 0))

    return pl.pallas_call(
        functools.partial(_mix_ffn_kernel, final=final, tf=tf),
        grid=(M // tm,),
        in_specs=[rows(oa), rows(oc), rows(ob), _resident(w_o), rows(h),
                  pl.BlockSpec((1, N_MOD, D), lambda i: (i // per_b, 0, 0)),
                  pl.BlockSpec((1, D), lambda i: (0, 0)),
                  _resident(w_gu), _resident(w_down),
                  pl.BlockSpec((1, D), lambda i: (0, 0))],
        out_specs=pl.BlockSpec((tm, D), lambda i: (i, 0)),
        out_shape=jax.ShapeDtypeStruct((M, D), F32),
        compiler_params=_cparams("arbitrary"),
        name="mix_ffn",
    )(oa, oc, ob, w_o, h, mod, gain, w_gu, w_down, final_gain)


def _win_kernel(h_ref, mod_ref, g_ref, w_ref, o_ref, *, tn):
    y = _rms(h_ref[...], g_ref[...])
    n = (y * (1.0 + mod_ref[0, 4:5, :]) + mod_ref[0, 3:4, :]).astype(BF16)
    width = w_ref.shape[1]
    for c0 in range(0, width, tn):
        c1 = min(c0 + tn, width)
        o_ref[:, c0:c1] = _dot(n, w_ref[:, c0:c1])


def _in_proj(h, mod, gain, w, *, seq):
    M, D = h.shape
    N = w.shape[1]
    tm, tn = PROJ_ROWS, 2 * LANES
    assert N % LANES == 0
    per_b = seq // tm
    return pl.pallas_call(
        functools.partial(_win_kernel, tn=tn),
        grid=(M // tm,),
        in_specs=[pl.BlockSpec((tm, D), lambda i: (i, 0)),
                  pl.BlockSpec((1, N_MOD, D), lambda i: (i // per_b, 0, 0)),
                  pl.BlockSpec((1, D), lambda i: (0, 0)),
                  _resident(w)],
        out_specs=pl.BlockSpec((tm, N), lambda i: (i, 0)),
        out_shape=jax.ShapeDtypeStruct((M, N), F32),
        compiler_params=_cparams("arbitrary"),
        name="in_proj",
    )(h, mod, gain, w)


def _rope_kernel(pos_ref, inv_ref, c_ref, s_ref):
    ang = pos_ref[0].astype(F32) * inv_ref[...]
    c_ref[0] = jnp.cos(ang)
    s_ref[0] = jnp.sin(ang)


def _rope_tables(positions):
    B, T = positions.shape
    half = IDX_ROPE // 2
    assert MLA_ROPE == IDX_ROPE and MLA_NOPE == IDX_DIM
    inv = (ROPE_THETA ** (-np.arange(half, dtype=np.float32) / half)).astype(np.float32)
    zeros32 = np.zeros(IDX_DIM - IDX_ROPE, np.float32)
    inv_row = jnp.asarray(np.concatenate([inv, inv, zeros32, inv, inv, zeros32])[None, :])
    tab = jax.ShapeDtypeStruct((B, T, LANES), F32)
    spec = pl.BlockSpec((1, T, LANES), lambda b: (b, 0, 0))
    return pl.pallas_call(
        _rope_kernel,
        grid=(B,),
        in_specs=[pl.BlockSpec((1, T, 1), lambda b: (b, 0, 0)),
                  pl.BlockSpec((1, LANES), lambda b: (0, 0))],
        out_specs=[spec, spec],
        out_shape=[tab, tab],
        compiler_params=_cparams("arbitrary"),
        name="rope_tables",
    )(positions.reshape(B, T, 1), inv_row)


def _dil_kernel(tab_ref, bk_ref, q_ref, k_ref, v_ref, o_ref, bt_scr, o_scr, m_scr, s_scr, *, seq):
    b = pl.program_id(0)
    pair = pl.program_id(1)

    @pl.when(b == 0)
    def _():
        for p in range(3):
            for hh in range(2):
                bt_scr[pair, p, hh] = _bias_from_buckets(bk_ref[p], tab_ref, 2 * pair + hh)

    nblk = seq // BLK
    low = lax.broadcasted_iota(jnp.int32, (nblk, BLK, LANES), 2) < HEAD_DIM
    neg_half = jnp.full((BLK, BLK), NEG, F32)

    for p, d in enumerate(DILATIONS):
        nb = seq // (d * BLK)
        starts = [r + d * BLK * n for n in range(nb) for r in range(d)]

        def rows_of(s0, d=d):
            return pl.ds(s0, BLK) if d == 1 else pl.ds(s0, BLK, stride=d)

        def regroup(ref):
            return jnp.stack([ref[0, rows_of(s0), :] for s0 in starts])

        def band(z):
            if nb == 1:
                return z
            prev = jnp.concatenate([jnp.zeros((d,) + z.shape[1:], z.dtype), z[:-d]], axis=0)
            return jnp.concatenate([prev, z], axis=1)

        q = regroup(q_ref) * (HEAD_DIM ** -0.5)
        kband = band(regroup(k_ref).astype(BF16))
        vband = band(regroup(v_ref).astype(BF16))
        res = []
        for hh in range(2):
            qh = (jnp.where(low, q, 0.0) if hh == 0 else jnp.where(low, 0.0, q)).astype(BF16)
            logit = lax.dot_general(qh, kband, (((2,), (2,)), ((0,), (0,))), preferred_element_type=F32)
            bias = bt_scr[pair, p, hh]
            if nb == 1:
                logit = logit + bias[:, BLK:][None]
            else:
                first = jnp.concatenate([neg_half, bias[:, BLK:]], axis=1)
                logit = jnp.concatenate([logit[:d] + first[None], logit[d:] + bias[None]], axis=0)
            m = jnp.max(logit, axis=-1, keepdims=True)
            e = jnp.exp(logit - m)
            s = jnp.sum(e, axis=-1, keepdims=True)
            o = lax.dot_general(e.astype(BF16), vband, (((2,), (1,)), ((0,), (0,))), preferred_element_type=F32)
            res.append((m, s, o))
        o_all = jnp.where(low, res[0][2], res[1][2])
        m_all = jnp.where(low, res[0][0], res[1][0])
        s_all = jnp.where(low, res[0][1], res[1][1])
        for bi, s0 in enumerate(starts):
            o_scr[p, rows_of(s0), :] = o_all[bi]
            m_scr[p, rows_of(s0), :] = m_all[bi]
            s_scr[p, rows_of(s0), :] = s_all[bi]

    m_all = jnp.maximum(jnp.maximum(m_scr[0], m_scr[1]), m_scr[2])
    num = jnp.zeros((seq, LANES), F32)
    den = jnp.zeros((seq, LANES), F32)
    for p in range(3):
        w = jnp.exp(m_scr[p] - m_all)
        num = num + w * o_scr[p]
        den = den + w * s_scr[p]
    o_ref[0] = (num / den).astype(BF16)


def _dilated_attention(proj, rel_bias, *, batch, seq):
    qi = np.arange(BLK)[:, None]
    kj = np.arange(2 * BLK)[None, :]
    step = BLK + qi - kj
    in_band = (step >= 0) & (step <= BLK)
    bk = np.stack([np.where(in_band, _np_bucket(step * d), -1) for d in DILATIONS]).astype(np.int32)
    pairs = A_HEADS // 2

    def col(t0):
        return pl.BlockSpec((1, seq, LANES), lambda b, p: (b, 0, t0 + p))

    return pl.pallas_call(
        functools.partial(_dil_kernel, seq=seq),
        grid=(batch, pairs),
        in_specs=[pl.BlockSpec(memory_space=pltpu.SMEM),
                  pl.BlockSpec((3, BLK, 2 * BLK), lambda b, p: (0, 0, 0)),
                  col(T_QA), col(T_KA), col(T_VA)],
        out_specs=pl.BlockSpec((1, seq, LANES), lambda b, p: (b, 0, p)),
        out_shape=jax.ShapeDtypeStruct((batch, seq, pairs * LANES), BF16),
        scratch_shapes=[pltpu.VMEM((pairs, 3, 2, BLK, 2 * BLK), F32),
                        pltpu.VMEM((3, seq, LANES), F32),
                        pltpu.VMEM((3, seq, LANES), F32),
                        pltpu.VMEM((3, seq, LANES), F32)],
        compiler_params=_cparams("arbitrary", "arbitrary"),
        name="dilated_attn",
    )(rel_bias, jnp.asarray(bk), proj, proj, proj)


def _mla_kernel(cq_ref, ckv_ref, small_ref, c_ref, s_ref, qn_ref, kvn_ref,
                wq_ref, wqs_ref, wk_ref, wv_ref, o_ref, q_scr, k_scr, v_scr, *, seq):
    rc = 512
    for c0 in range(0, seq, rc):
        rows = slice(c0, c0 + rc)
        lane_r = lax.broadcasted_iota(jnp.int32, (rc, LANES), 1)
        rot = (lane_r >= MLA_NOPE) & (lane_r < MLA_NOPE + MLA_ROPE)
        cos = jnp.where(rot, c_ref[0, rows, :], 1.0)
        sin = jnp.where(rot, s_ref[0, rows, :], 0.0)
        nq = _rms(cq_ref[0, rows, :], qn_ref[...]).astype(BF16)
        nkv = _rms(ckv_ref[0, rows, :], kvn_ref[...]).astype(BF16)
        small = small_ref[0, rows, :]
        partner = pltpu.roll(small, LANES - (L_KRS - L_KR), 1)
        kr = jnp.where(rot, small * cos + partner * sin, 0.0)
        for h in range(B_HEADS):
            cols = slice(h * LANES, (h + 1) * LANES)
            q_scr[h, rows, :] = (_dot(nq, wq_ref[:, cols]) * cos + _dot(nq, wqs_ref[:, cols]) * sin).astype(BF16)
            k_scr[h, rows, :] = (_dot(nkv, wk_ref[:, cols]) + kr).astype(BF16)
            v = _dot(nkv, wv_ref[:, cols])
            lane_c = lax.broadcasted_iota(jnp.int32, v.shape, 1)
            v_scr[h, rows, :] = jnp.where(lane_c < HEAD_DIM, v, 1.0).astype(BF16)

    exp2_scale = (MLA_NOPE + MLA_ROPE) ** -0.5 * math.log2(math.e)
    causal = lax.broadcasted_iota(jnp.int32, (QB, QB), 1) <= lax.broadcasted_iota(jnp.int32, (QB, QB), 0)
    low = lax.broadcasted_iota(jnp.int32, (QB, LANES), 1) < HEAD_DIM
    blocks = [(qb, h) for qb in range(seq // QB) for h in range(B_HEADS)]

    def qk(qb, h):
        return _dot_t(q_scr[h, qb * QB:(qb + 1) * QB, :], k_scr[h, 0:(qb + 1) * QB, :])

    nxt = qk(*blocks[0])
    outs = []
    for idx, (qb, h) in enumerate(blocks):
        qs, ext = qb * QB, (qb + 1) * QB
        logit = nxt
        if idx + 1 < len(blocks):
            nxt = qk(*blocks[idx + 1])
        diag = jnp.where(causal, logit[:, qs:], NEG)
        logit = diag if qb == 0 else jnp.concatenate([logit[:, :qs], diag], axis=1)
        m = jnp.max(logit, axis=-1, keepdims=True)
        e = jnp.exp2((logit - m) * exp2_scale).astype(BF16)
        pv = _dot(e, v_scr[h, 0:ext, :])
        outs.append(jnp.where(low, pv / pltpu.roll(pv, HEAD_DIM, 1), 0.0))
        if h == B_HEADS - 1:
            for pt in range((B_HEADS + 1) // 2):
                pair = outs[2 * pt]
                if 2 * pt + 1 < B_HEADS:
                    pair = pair + pltpu.roll(outs[2 * pt + 1], HEAD_DIM, 1)
                o_ref[0, qs:qs + QB, pt * LANES:(pt + 1) * LANES] = pair.astype(BF16)
            outs = []


def _mla_attention(proj, cos_m, sin_m, q_norm, kv_norm, wq, wqs, wk, wv, *, batch, seq):
    def col(t0, n):
        return pl.BlockSpec((1, seq, n * LANES), lambda b: (b, 0, t0 // n))

    def full(a):
        return pl.BlockSpec(a.shape, lambda b: (0,) * a.ndim)

    tab = pl.BlockSpec((1, seq, LANES), lambda b: (b, 0, 0))
    width = ((B_HEADS + 1) // 2) * LANES
    return pl.pallas_call(
        functools.partial(_mla_kernel, seq=seq),
        grid=(batch,),
        in_specs=[col(T_CQ, 2), col(T_CKV, 1), col(T_SMALL, 1), tab, tab,
                  full(q_norm), full(kv_norm), full(wq), full(wqs), full(wk), full(wv)],
        out_specs=pl.BlockSpec((1, seq, width), lambda b: (b, 0, 0)),
        out_shape=jax.ShapeDtypeStruct((batch, seq, width), BF16),
        scratch_shapes=[pltpu.VMEM((B_HEADS, seq, LANES), BF16)] * 3,
        compiler_params=_cparams("arbitrary"),
        name="mla_attn",
    )(proj, proj, proj, cos_m, sin_m, q_norm, kv_norm, wq, wqs, wk, wv)


def _sel_kernel(*refs, seq):
    for j in range(2):
        _sel_block(j, *refs, seq=seq)


def _sel_block(j, iq_ref, iqs_ref, smallq_ref, cq_ref, sq_ref, ik_ref, smallk_ref, ck_ref, sk_ref, g_ref,
               o_ref, ik_scr, iqm_scr, key_scr, hb_scr, *, seq):
    i = 2 * pl.program_id(1) + j
    qrows = slice(j * QB, (j + 1) * QB)
    qs = i * QB
    nk = seq // QB

    @pl.when(i == 0)
    def _():
        x = ik_ref[0]
        r = lax.rsqrt(jnp.mean(x * x, axis=-1, keepdims=True) + RMS_EPS)
        y = x * r * g_ref[0:1, :]
        small = smallk_ref[0]
        lane_k = lax.broadcasted_iota(jnp.int32, small.shape, 1)
        partner = jnp.where(lane_k < IDX_DIM, small, pltpu.roll(small, IDX_DIM, 1))
        ys = jnp.where(lane_k % IDX_DIM < IDX_ROPE, partner, 0.0) * r * g_ref[1:2, :]
        ik_scr[...] = (y * ck_ref[0] + ys * sk_ref[0]).astype(BF16)

    lane = lax.broadcasted_iota(jnp.int32, (QB, LANES), 1)
    low = lane < IDX_DIM
    cos = cq_ref[0, qrows, :]
    sin = sq_ref[0, qrows, :]
    for t in range(IDX_HEADS // 2):
        cols = slice(t * LANES, (t + 1) * LANES)
        rot = iq_ref[0, qrows, cols] * cos + iqs_ref[0, qrows, cols] * sin
        iqm_scr[2 * t] = jnp.where(low, rot, 0.0).astype(BF16)
        iqm_scr[2 * t + 1] = jnp.where(low, 0.0, rot).astype(BF16)
    w_t = (smallq_ref[0, qrows, :] * (IDX_HEADS ** -0.5 * IDX_DIM ** -0.5)).T

    krow = lax.broadcasted_iota(jnp.int32, (QB, QB), 0)
    qcol = lax.broadcasted_iota(jnp.int32, (QB, QB), 1)
    target = jnp.minimum(qs + lax.broadcasted_iota(jnp.int32, (1, QB), 1) + 1, TOPK).astype(F32)
    min_normal = float(np.finfo(np.float32).tiny)
    min_normal_bits = int(np.array(min_normal, np.float32).view(np.int32))
    hi_mask = -(1 << 16)
    one16, zero16 = jnp.ones((), BF16), jnp.zeros((), BF16)
    sub16 = QB // 16

    def block(n_tiles):
        iqm = iqm_scr[...].reshape(IDX_HEADS * QB, LANES)
        for kb in range(n_tiles):
            rel = _dot_t(ik_scr[kb * QB:(kb + 1) * QB, :], iqm)
            sc = jnp.zeros((QB, QB), F32)
            for h in range(IDX_HEADS):
                sc = sc + jnp.maximum(rel[:, h * QB:(h + 1) * QB], 0.0) * w_t[L_IW + h:L_IW + h + 1, :]
            sc = jnp.where(jnp.abs(sc) < min_normal, 0.0, sc)
            if kb == n_tiles - 1:
                sc = jnp.where(krow <= qcol, sc, -jnp.inf)
            bits = lax.bitcast_convert_type(sc, jnp.int32)
            key_scr[kb] = jnp.where(bits < 0, bits ^ KEY_FLIP, bits)
            hb_scr[kb] = lax.bitcast_convert_type(bits & hi_mask, F32).astype(BF16)

        def count_hi(cand16):
            acc = jnp.zeros((16, QB), BF16)
            for c in range(n_tiles):
                x = hb_scr[c].reshape(sub16, 16, QB)
                for r in range(sub16):
                    acc = acc + jnp.where(x[r] >= cand16, one16, zero16)
            return jnp.sum(acc.astype(F32), axis=0, keepdims=True)

        def count(pred):
            acc = jnp.zeros((8, QB), F32)
            for c in range(n_tiles):
                acc = acc + jnp.sum(pred(key_scr[c]).astype(F32).reshape(QB // 8, 8, QB), axis=0)
            return jnp.sum(acc, axis=0, keepdims=True)

        def hi_step(bi, prefix):
            cand_u = prefix | lax.shift_left(jnp.int32(1), 31 - bi)
            k = cand_u ^ INT_MIN
            cbits = jnp.where(k < 0, k ^ KEY_FLIP, k) & hi_mask
            cbits = jnp.where((cbits > 0) & (cbits < min_normal_bits), min_normal_bits, cbits)
            cand = lax.bitcast_convert_type(cbits, F32)
            cnt = count_hi(jnp.broadcast_to(cand, (16, QB)).astype(BF16))
            return jnp.where(cnt >= target, cand_u, prefix)

        def lo_step(bi, prefix):
            cand_u = prefix | lax.shift_left(jnp.int32(1), 15 - bi)
            cand = cand_u ^ INT_MIN
            cnt = count(lambda k: k >= cand)
            return jnp.where(cnt >= target, cand_u, prefix)

        prefix = lax.fori_loop(0, 16, hi_step, jnp.zeros((1, QB), jnp.int32))
        thr = lax.fori_loop(0, 16, lo_step, prefix) ^ INT_MIN
        need = target - count(lambda k: k > thr)

        lower = (qcol <= krow).astype(BF16)
        carry = jnp.zeros((1, QB), F32)
        for c in range(n_tiles):
            kc = key_scr[c]
            eq = kc == thr
            rank = carry + _dot(lower, eq.astype(BF16))
            sel = (kc > thr) | (eq & (rank <= need))
            o_ref[0, j, c] = jnp.where(sel, 0.0, NEG).T.astype(BF16)
            carry = rank[QB - 1:QB, :]
        for c in range(n_tiles, nk):
            o_ref[0, j, c] = jnp.full((QB, QB), NEG, BF16)

    for n_tiles in range(1 + j, nk + 1, 2):
        pl.when(i == n_tiles - 1)(functools.partial(block, n_tiles))


def _dsa_select(proj, cos_i, sin_i, ik_gains, *, batch, seq):
    nq = seq // QB

    assert nq % 2 == 0

    def qcol(t0, n):
        return pl.BlockSpec((1, 2 * QB, n * LANES), lambda b, i: (b, i, t0 // n))

    def kcol(t0):
        return pl.BlockSpec((1, seq, LANES), lambda b, i: (b, 0, t0))

    qtab = pl.BlockSpec((1, 2 * QB, LANES), lambda b, i: (b, i, 0))
    ktab = pl.BlockSpec((1, seq, LANES), lambda b, i: (b, 0, 0))
    return pl.pallas_call(
        functools.partial(_sel_kernel, seq=seq),
        grid=(batch, nq // 2),
        in_specs=[qcol(T_IQ, 4), qcol(T_IQS, 4), qcol(T_SMALL, 1), qtab, qtab,
                  kcol(T_IK), kcol(T_SMALL), ktab, ktab,
                  pl.BlockSpec((2, LANES), lambda b, i: (0, 0))],
        out_specs=pl.BlockSpec((1, 2, nq, QB, QB), lambda b, i: (b, i, 0, 0, 0)),
        out_shape=jax.ShapeDtypeStruct((batch, nq, nq, QB, QB), BF16),
        scratch_shapes=[pltpu.VMEM((seq, LANES), BF16),
                        pltpu.VMEM((IDX_HEADS, QB, LANES), BF16),
                        pltpu.VMEM((nq, QB, QB), jnp.int32),
                        pltpu.VMEM((nq, QB, QB), BF16)],
        compiler_params=_cparams("arbitrary", "arbitrary"),
        name="dsa_select",
    )(proj, proj, proj, cos_i, sin_i, proj, proj, cos_i, sin_i, ik_gains)


def _dsa_kernel(tab_ref, bk_ref, q_ref, k_ref, v_ref, sel_ref, o_ref, kb_scr, vb_scr, bt_scr):
    b = pl.program_id(0)
    i = pl.program_id(1)

    n_pairs = (C_HEADS + 1) // 2
    heads_of = [min(2, C_HEADS - 2 * pt) for pt in range(n_pairs)]

    far = N_BUCKETS - 1

    @pl.when((b == 0) & (i == 0))
    def _():
        for pt in range(n_pairs):
            for hh in range(heads_of[pt]):
                col = A_HEADS + 2 * pt + hh
                for dd in range(2):
                    bt_scr[2 * pt + dd, hh * QB:(hh + 1) * QB, :] = (
                        _bias_from_buckets(bk_ref[dd], tab_ref, col) - tab_ref[far, col])

    @pl.when(i == 0)
    def _():
        kb_scr[...] = k_ref[0].astype(BF16)
        vb_scr[...] = v_ref[0].astype(BF16)

    lane = lax.broadcasted_iota(jnp.int32, (QB, LANES), 1)
    low = lane < HEAD_DIM

    def q_blocks(step):
        chains = [(2 * step + j, pt) for j in range(2) for pt in range(n_pairs)]

        def qk(qb, pt):
            j = qb - 2 * step
            cols = slice(pt * LANES, (pt + 1) * LANES)
            q = q_ref[0, j * QB:(j + 1) * QB, cols] * (HEAD_DIM ** -0.5)
            halves = [jnp.where(low, q, 0.0), jnp.where(low, 0.0, q)][:heads_of[pt]]
            return _dot_t(jnp.concatenate(halves, axis=0).astype(BF16), kb_scr[0:(qb + 1) * QB, cols])

        nxt = qk(*chains[0])
        for idx, (qb, pt) in enumerate(chains):
            j = qb - 2 * step
            ext = (qb + 1) * QB
            heads = heads_of[pt]
            rows = heads * QB
            cols = slice(pt * LANES, (pt + 1) * LANES)
            logit = nxt
            if idx + 1 < len(chains):
                nxt = qk(*chains[idx + 1])
            pieces = []
            for c in range(qb + 1):
                t = logit[:, c * QB:(c + 1) * QB] + jnp.concatenate([sel_ref[0, j, c].astype(F32)] * heads, axis=0)
                if qb - c < 2:
                    t = t + bt_scr[2 * pt + (qb - c), 0:rows, :]
                pieces.append(t)
            logit = pieces[0] if qb == 0 else jnp.concatenate(pieces, axis=1)
            m = jnp.max(logit, axis=-1, keepdims=True)
            e = jnp.exp(logit - m)
            s = jnp.sum(e, axis=-1, keepdims=True)
            out = _dot(e.astype(BF16), vb_scr[0:ext, cols]) / s
            upper = out[QB:] if heads == 2 else 0.0
            o_ref[0, j * QB:(j + 1) * QB, cols] = jnp.where(low, out[:QB], upper).astype(BF16)

    for step in range(sel_ref.shape[2] // 2):
        pl.when(i == step)(functools.partial(q_blocks, step))


def _dsa_attention(proj, sel, rel_bias, *, batch, seq):
    nq = seq // QB
    qi = np.arange(QB)[:, None]
    kj = np.arange(QB)[None, :]
    bk = np.stack([_np_bucket(qi - kj), _np_bucket(QB + qi - kj)]).astype(np.int32)
    assert _np_bucket(np.array(QB + 1)) == N_BUCKETS - 1
    width = ((C_HEADS + 1) // 2) * LANES
    return pl.pallas_call(
        _dsa_kernel,
        grid=(batch, nq // 2),
        in_specs=[pl.BlockSpec(memory_space=pltpu.SMEM),
                  pl.BlockSpec((2, QB, QB), lambda b, i: (0, 0, 0)),
                  pl.BlockSpec((1, 2 * QB, width), lambda b, i: (b, i, T_QC * LANES // width)),
                  pl.BlockSpec((1, seq, width), lambda b, i: (b, 0, T_KC * LANES // width)),
                  pl.BlockSpec((1, seq, width), lambda b, i: (b, 0, T_VC * LANES // width)),
                  pl.BlockSpec((1, 2, nq, QB, QB), lambda b, i: (b, i, 0, 0, 0))],
        out_specs=pl.BlockSpec((1, 2 * QB, width), lambda b, i: (b, i, 0)),
        out_shape=jax.ShapeDtypeStruct((batch, seq, width), BF16),
        scratch_shapes=[pltpu.VMEM((seq, width), BF16),
                        pltpu.VMEM((seq, width), BF16),
                        pltpu.VMEM((2 * ((C_HEADS + 1) // 2), 2 * QB, QB), F32)],
        compiler_params=_cparams("arbitrary", "arbitrary"),
        name="dsa_attn",
    )(rel_bias, jnp.asarray(bk), proj, proj, proj, sel)


def _swap_halves(w, rope):
    half = rope // 2
    return jnp.concatenate([-w[..., half:rope], w[..., :half], jnp.zeros_like(w[..., rope:])], axis=-1)


def _pad_cols(w, n):
    return jnp.concatenate([w, jnp.zeros(w.shape[:-1] + (n - w.shape[-1],), w.dtype)], axis=-1)


def _layout_w_in(w):
    D = w.shape[0]
    sizes = (A_HEADS * HEAD_DIM,) * 3 + (MLA_Q_RANK, MLA_KV_RANK, MLA_ROPE) + (C_HEADS * HEAD_DIM,) * 3 + (
        IDX_HEADS * IDX_DIM, IDX_DIM, IDX_HEADS)
    offs = np.cumsum((0,) + sizes)
    qa, ka, va, cq, ckv, kr, qc, kc, vc, iq, ik, iw = [w[:, offs[n]:offs[n + 1]] for n in range(len(sizes))]
    c_w = ((C_HEADS + 1) // 2) * LANES
    iq_h = iq.reshape(D, IDX_HEADS, IDX_DIM)
    iq_sw = _swap_halves(iq_h, IDX_ROPE).reshape(D, IDX_HEADS * IDX_DIM)
    ik_sw = _swap_halves(ik, IDX_ROPE)[:, :IDX_ROPE]
    assert (L_IKS, L_IW, L_KR, L_KRS) == (0, IDX_ROPE, MLA_NOPE, MLA_NOPE + MLA_ROPE)
    small = jnp.concatenate([ik_sw, _pad_cols(iw, L_KR - L_IW), kr, _swap_halves(kr, MLA_ROPE)], axis=-1)
    cols = [qa, ka, va, _pad_cols(qc, c_w), _pad_cols(kc, c_w), _pad_cols(vc, c_w), cq, iq, iq_sw, ckv,
            small, jnp.concatenate([ik, ik], axis=-1)]
    out = jnp.concatenate(cols, axis=-1)
    assert out.shape[1] == N_TILES * LANES
    return out.astype(BF16)


def _layout_mla(w_uq, w_ukv):
    rq = w_uq.reshape(MLA_Q_RANK, B_HEADS, MLA_NOPE + MLA_ROPE)
    nope, rope = rq[..., :MLA_NOPE], rq[..., MLA_NOPE:]
    wq = _pad_cols(jnp.concatenate([nope, rope], axis=-1), LANES)
    wqs = _pad_cols(jnp.concatenate([jnp.zeros_like(nope), _swap_halves(rope, MLA_ROPE)], axis=-1), LANES)
    rkv = w_ukv.reshape(MLA_KV_RANK, B_HEADS, MLA_NOPE + HEAD_DIM)
    wk = _pad_cols(rkv[..., :MLA_NOPE], LANES)
    wv = _pad_cols(rkv[..., MLA_NOPE:], LANES)
    flat = lambda a, r: a.reshape(r, B_HEADS * LANES).astype(BF16)
    return flat(wq, MLA_Q_RANK), flat(wqs, MLA_Q_RANK), flat(wk, MLA_KV_RANK), flat(wv, MLA_KV_RANK)


def _layout_w_out(w):
    D = w.shape[1]
    a, b_, c_ = A_HEADS * HEAD_DIM, B_HEADS * HEAD_DIM, C_HEADS * HEAD_DIM
    def pad_rows(x, heads):
        return jnp.concatenate([x, jnp.zeros((((heads + 1) // 2) * LANES - x.shape[0], D), w.dtype)], axis=0)

    wa = w[:a]
    wb = pad_rows(w[a:a + b_], B_HEADS)
    wc = pad_rows(w[a + b_:a + b_ + c_], C_HEADS)
    return jnp.concatenate([wa, wc, wb], axis=0).astype(BF16)


def kernel(x, c, positions, rel_bias, ada_w, ada_b, norm_ffn1, ffn1_w_gu, ffn1_w_down, norm_mix, w_in, mla_q_norm, mla_w_uq, mla_kv_norm, mla_w_ukv, idx_k_norm, w_out, norm_ffn2, ffn2_w_gu, ffn2_w_down, final_norm):
    B, T, D = x.shape
    L = ada_w.shape[0]
    assert T % (max(DILATIONS) * BLK) == 0 and T % 512 == 0 and D % LANES == 0

    mods = _modulation(c, ada_w, ada_b).reshape(L, B, N_MOD, D)
    cos_i, sin_i = _rope_tables(positions)
    cos_m, sin_m = cos_i, sin_i
    final_gain = final_norm.reshape(1, D)
    h = x.reshape(B * T, D)
    for l in range(L):
        mod = mods[l]
        h = _ffn(h, mod, norm_ffn1[l].reshape(1, D), ffn1_w_gu[l].astype(BF16), ffn1_w_down[l].astype(BF16),
                 final_gain, ks=0, final=False, seq=T)

        proj = _in_proj(h, mod, norm_mix[l].reshape(1, D), _layout_w_in(w_in[l]), seq=T)
        proj = proj.reshape(B, T, N_TILES * LANES)
        oa = _dilated_attention(proj, rel_bias, batch=B, seq=T)
        wq, wqs, wk, wv = _layout_mla(mla_w_uq[l], mla_w_ukv[l])
        ob = _mla_attention(proj, cos_m, sin_m, mla_q_norm[l].reshape(1, -1), mla_kv_norm[l].reshape(1, -1),
                            wq, wqs, wk, wv, batch=B, seq=T)
        g = idx_k_norm[l]
        g_sw = jnp.concatenate([g[IDX_ROPE // 2:IDX_ROPE], g[:IDX_ROPE // 2], jnp.zeros_like(g[IDX_ROPE:])])
        ik_gains = jnp.stack([jnp.concatenate([g, g]), jnp.concatenate([g_sw, g_sw])])
        sel = _dsa_select(proj, cos_i, sin_i, ik_gains, batch=B, seq=T)
        oc = _dsa_attention(proj, sel, rel_bias, batch=B, seq=T)
        h = _mix_ffn(oa.reshape(B * T, -1), oc.reshape(B * T, -1), ob.reshape(B * T, -1), _layout_w_out(w_out[l]), h, mod,
                     norm_ffn2[l].reshape(1, D), ffn2_w_gu[l].astype(BF16), ffn2_w_down[l].astype(BF16),
                     final_gain, final=(l == L - 1), seq=T)
    return h.reshape(B, T, D)
```

```python
import functools
import math

import numpy as np
import jax
import jax.numpy as jnp
from jax import lax
from jax.experimental import pallas as pl
from jax.experimental.pallas import tpu as pltpu

F32 = jnp.float32
BF16 = jnp.bfloat16

HEAD_DIM = 64
A_HEADS, B_HEADS, C_HEADS = 6, 5, 5
MLA_Q_RANK, MLA_KV_RANK, MLA_NOPE, MLA_ROPE = 256, 128, 64, 32
IDX_HEADS, IDX_DIM, IDX_ROPE = 8, 64, 32
TOPK = 256
N_BUCKETS, MAX_DISTANCE = 32, 128
ROPE_THETA = 10000.0
RMS_EPS = 1e-6
N_MOD = 9
NEG = -1e30
DILATIONS = (1, 4, 16)

LANES = 128
BLK = 128
QB = 256
STEP_BLOCKS = 4
FFN_ROWS = 1024
FFN_CHUNK = 256
PROJ_ROWS = 512
VMEM_LIMIT = 56 * 1024 * 1024

T_QA, T_KA, T_VA = 0, 3, 6
T_QC, T_KC, T_VC = 9, 12, 15
T_CQ, T_IQ, T_IQS = 18, 20, 24
T_CKV, T_SMALL, T_IK = 28, 29, 30
N_TILES = 31
L_IKS, L_IW, L_KR, L_KRS = 0, 32, 64, 96

KEY_FLIP = 0x7FFFFFFF
INT_MIN = -(2 ** 31)


def _cparams(*sem):
    return pltpu.CompilerParams(dimension_semantics=sem, vmem_limit_bytes=VMEM_LIMIT)


def _rms(x, g):
    return x * lax.rsqrt(jnp.mean(x * x, axis=-1, keepdims=True) + RMS_EPS) * g


def _dot(a, b):
    return jnp.dot(a, b, preferred_element_type=F32)


def _dot_t(a, b):
    return lax.dot_general(a, b, (((1,), (1,)), ((), ())), preferred_element_type=F32)


def _np_bucket(dist):
    n = np.maximum(dist, 0)
    max_exact = N_BUCKETS // 2
    nf = np.maximum(n, 1).astype(np.float32)
    large = max_exact + (np.log(nf / np.float32(max_exact)) / np.float32(math.log(MAX_DISTANCE / max_exact))
                         * np.float32(N_BUCKETS - max_exact)).astype(np.int32)
    large = np.minimum(large, N_BUCKETS - 1)
    return np.where(n < max_exact, n, large).astype(np.int32)


def _bias_from_buckets(bucket, tab_ref, col):
    acc = jnp.full(bucket.shape, NEG, F32)
    for b in range(N_BUCKETS):
        acc = jnp.where(bucket == b, tab_ref[b, col], acc)
    return acc


def _mod_kernel(c_ref, w_ref, b_ref, o_ref):
    c = c_ref[...]
    cond = c / (1.0 + jnp.exp(-c))
    w = w_ref[0]
    c_hi = cond.astype(BF16)
    c_lo = (cond - c_hi.astype(F32)).astype(BF16)
    w_hi = w.astype(BF16)
    w_lo = (w - w_hi.astype(F32)).astype(BF16)
    o_ref[0] = _dot(c_hi, w_hi) + _dot(c_hi, w_lo) + _dot(c_lo, w_hi) + b_ref[0]


def _modulation(c, ada_w, ada_b):
    L, D, N = ada_w.shape
    B = c.shape[0]
    tn = 1024
    return pl.pallas_call(
        _mod_kernel,
        grid=(L, N // tn),
        in_specs=[pl.BlockSpec((B, D), lambda l, j: (0, 0)),
                  pl.BlockSpec((1, D, tn), lambda l, j: (l, 0, j)),
                  pl.BlockSpec((1, 1, tn), lambda l, j: (l, 0, j))],
        out_specs=pl.BlockSpec((1, B, tn), lambda l, j: (l, 0, j)),
        out_shape=jax.ShapeDtypeStruct((L, B, N), F32),
        compiler_params=_cparams("arbitrary", "arbitrary"),
        name="adaln_mod",
    )(c, ada_w, ada_b.reshape(L, 1, N))


def _ffn_kernel(h_ref, mod_ref, g_ref, wgu_ref, wd_ref, fg_ref, o_ref, *, ks, final, tf):
    _ffn_body(h_ref[...], mod_ref, g_ref, wgu_ref, wd_ref, fg_ref, o_ref, ks=ks, final=final, tf=tf)


def _mix_ffn_kernel(oa_ref, oc_ref, ob_ref, wo_ref, h_ref, mod_ref, g_ref, wgu_ref, wd_ref, fg_ref,
                    o_ref, *, final, tf):
    heads = jnp.concatenate([oa_ref[...], oc_ref[...], ob_ref[...]], axis=1)
    x = h_ref[...] + mod_ref[0, 5:6, :] * _dot(heads, wo_ref[...])
    _ffn_body(x, mod_ref, g_ref, wgu_ref, wd_ref, fg_ref, o_ref, ks=6, final=final, tf=tf)


def _ffn_body(x, mod_ref, g_ref, wgu_ref, wd_ref, fg_ref, o_ref, *, ks, final, tf):
    F = wd_ref.shape[0]
    y = _rms(x, g_ref[...])
    n = (y * (1.0 + mod_ref[0, ks + 1:ks + 2, :]) + mod_ref[0, ks:ks + 1, :]).astype(BF16)
    acc = jnp.zeros(x.shape, F32)
    for f0 in range(0, F, tf):
        g = _dot(n, wgu_ref[:, f0:f0 + tf])
        u = _dot(n, wgu_ref[:, F + f0:F + f0 + tf])
        a = (g / (1.0 + jnp.exp(-g))) * u
        acc = acc + _dot(a.astype(BF16), wd_ref[f0:f0 + tf, :])
    out = x + (0.5 * mod_ref[0, ks + 2:ks + 3, :]) * acc
    if final:
        out = _rms(out, fg_ref[...])
    o_ref[...] = out


def _resident(a):
    return pl.BlockSpec(a.shape, lambda *_: (0,) * a.ndim, pipeline_mode=pl.Buffered(1))


def _ffn(h, mod, gain, w_gu, w_down, final_gain, *, ks, final, seq):
    M, D = h.shape
    F = w_down.shape[0]
    tm, tf = FFN_ROWS, FFN_CHUNK
    assert F % tf == 0 and seq % tm == 0
    per_b = seq // tm
    return pl.pallas_call(
        functools.partial(_ffn_kernel, ks=ks, final=final, tf=tf),
        grid=(M // tm,),
        in_specs=[pl.BlockSpec((tm, D), lambda i: (i, 0)),
                  pl.BlockSpec((1, N_MOD, D), lambda i: (i // per_b, 0, 0)),
                  pl.BlockSpec((1, D), lambda i: (0, 0)),
                  _resident(w_gu), _resident(w_down),
                  pl.BlockSpec((1, D), lambda i: (0, 0))],
        out_specs=pl.BlockSpec((tm, D), lambda i: (i, 0)),
        out_shape=jax.ShapeDtypeStruct((M, D), F32),
        compiler_params=_cparams("arbitrary"),
        name="ffn",
    )(h, mod, gain, w_gu, w_down, final_gain)


def _mix_ffn(oa, oc, ob, w_o, h, mod, gain, w_gu, w_down, final_gain, *, final, seq):
    M, D = h.shape
    tm, tf = FFN_ROWS, FFN_CHUNK
    assert w_down.shape[0] % tf == 0 and seq % tm == 0
    per_b = seq // tm

    def rows(a):
        return pl.BlockSpec((tm, a.shape[1]), lambda i: (i, 0))

    return pl.pallas_call(
        functools.partial(_mix_ffn_kernel, final=final, tf=tf),
        grid=(M // tm,),
        in_specs=[rows(oa), rows(oc), rows(ob), _resident(w_o), rows(h),
                  pl.BlockSpec((1, N_MOD, D), lambda i: (i // per_b, 0, 0)),
                  pl.BlockSpec((1, D), lambda i: (0, 0)),
                  _resident(w_gu), _resident(w_down),
                  pl.BlockSpec((1, D), lambda i: (0, 0))],
        out_specs=pl.BlockSpec((tm, D), lambda i: (i, 0)),
        out_shape=jax.ShapeDtypeStruct((M, D), F32),
        compiler_params=_cparams("arbitrary"),
        name="mix_ffn",
    )(oa, oc, ob, w_o, h, mod, gain, w_gu, w_down, final_gain)


def _win_kernel(h_ref, mod_ref, g_ref, w_ref, o_ref, *, tn):
    y = _rms(h_ref[...], g_ref[...])
    n = (y * (1.0 + mod_ref[0, 4:5, :]) + mod_ref[0, 3:4, :]).astype(BF16)
    width = w_ref.shape[1]
    for c0 in range(0, width, tn):
        c1 = min(c0 + tn, width)
        o_ref[:, c0:c1] = _dot(n, w_ref[:, c0:c1])


def _in_proj(h, mod, gain, w, *, seq):
    M, D = h.shape
    N = w.shape[1]
    tm, tn = PROJ_ROWS, 2 * LANES
    assert N % LANES == 0
    per_b = seq // tm
    return pl.pallas_call(
        functools.partial(_win_kernel, tn=tn),
        grid=(M // tm,),
        in_specs=[pl.BlockSpec((tm, D), lambda i: (i, 0)),
                  pl.BlockSpec((1, N_MOD, D), lambda i: (i // per_b, 0, 0)),
                  pl.BlockSpec((1, D), lambda i: (0, 0)),
                  _resident(w)],
        out_specs=pl.BlockSpec((tm, N), lambda i: (i, 0)),
        out_shape=jax.ShapeDtypeStruct((M, N), F32),
        compiler_params=_cparams("arbitrary"),
        name="in_proj",
    )(h, mod, gain, w)


def _rope_kernel(pos_ref, inv_ref, c_ref, s_ref):
    ang = pos_ref[0].astype(F32) * inv_ref[...]
    c_ref[0] = jnp.cos(ang)
    s_ref[0] = jnp.sin(ang)


def _rope_tables(positions):
    B, T = positions.shape
    half = IDX_ROPE // 2
    assert MLA_ROPE == IDX_ROPE and MLA_NOPE == IDX_DIM
    inv = (ROPE_THETA ** (-np.arange(half, dtype=np.float32) / half)).astype(np.float32)
    zeros32 = np.zeros(IDX_DIM - IDX_ROPE, np.float32)
    inv_row = jnp.asarray(np.concatenate([inv, inv, zeros32, inv, inv, zeros32])[None, :])
    tab = jax.ShapeDtypeStruct((B, T, LANES), F32)
    spec = pl.BlockSpec((1, T, LANES), lambda b: (b, 0, 0))
    return pl.pallas_call(
        _rope_kernel,
        grid=(B,),
        in_specs=[pl.BlockSpec((1, T, 1), lambda b: (b, 0, 0)),
                  pl.BlockSpec((1, LANES), lambda b: (0, 0))],
        out_specs=[spec, spec],
        out_shape=[tab, tab],
        compiler_params=_cparams("arbitrary"),
        name="rope_tables",
    )(positions.reshape(B, T, 1), inv_row)


def _dil_kernel(tab_ref, bk_ref, q_ref, k_ref, v_ref, o_ref, bt_scr, o_scr, m_scr, s_scr, *, seq):
    b = pl.program_id(0)
    pair = pl.program_id(1)

    @pl.when(b == 0)
    def _():
        for p in range(3):
            for hh in range(2):
                bt_scr[pair, p, hh] = _bias_from_buckets(bk_ref[p], tab_ref, 2 * pair + hh)

    nblk = seq // BLK
    low = lax.broadcasted_iota(jnp.int32, (nblk, BLK, LANES), 2) < HEAD_DIM
    neg_half = jnp.full((BLK, BLK), NEG, F32)

    for p, d in enumerate(DILATIONS):
        nb = seq // (d * BLK)
        starts = [r + d * BLK * n for n in range(nb) for r in range(d)]

        def rows_of(s0, d=d):
            return pl.ds(s0, BLK) if d == 1 else pl.ds(s0, BLK, stride=d)

        def regroup(ref):
            return jnp.stack([ref[0, rows_of(s0), :] for s0 in starts])

        def band(z):
            if nb == 1:
                return z
            prev = jnp.concatenate([jnp.zeros((d,) + z.shape[1:], z.dtype), z[:-d]], axis=0)
            return jnp.concatenate([prev, z], axis=1)

        q = regroup(q_ref) * (HEAD_DIM ** -0.5)
        kband = band(regroup(k_ref).astype(BF16))
        vband = band(regroup(v_ref).astype(BF16))
        res = []
        for hh in range(2):
            qh = (jnp.where(low, q, 0.0) if hh == 0 else jnp.where(low, 0.0, q)).astype(BF16)
            logit = lax.dot_general(qh, kband, (((2,), (2,)), ((0,), (0,))), preferred_element_type=F32)
            bias = bt_scr[pair, p, hh]
            if nb == 1:
                logit = logit + bias[:, BLK:][None]
            else:
                first = jnp.concatenate([neg_half, bias[:, BLK:]], axis=1)
                logit = jnp.concatenate([logit[:d] + first[None], logit[d:] + bias[None]], axis=0)
            m = jnp.max(logit, axis=-1, keepdims=True)
            e = jnp.exp(logit - m)
            s = jnp.sum(e, axis=-1, keepdims=True)
            o = lax.dot_general(e.astype(BF16), vband, (((2,), (1,)), ((0,), (0,))), preferred_element_type=F32)
            res.append((m, s, o))
        o_all = jnp.where(low, res[0][2], res[1][2])
        m_all = jnp.where(low, res[0][0], res[1][0])
        s_all = jnp.where(low, res[0][1], res[1][1])
        for bi, s0 in enumerate(starts):
            o_scr[p, rows_of(s0), :] = o_all[bi]
            m_scr[p, rows_of(s0), :] = m_all[bi]
            s_scr[p, rows_of(s0), :] = s_all[bi]

    m_all = jnp.maximum(jnp.maximum(m_scr[0], m_scr[1]), m_scr[2])
    num = jnp.zeros((seq, LANES), F32)
    den = jnp.zeros((seq, LANES), F32)
    for p in range(3):
        w = jnp.exp(m_scr[p] - m_all)
        num = num + w * o_scr[p]
        den = den + w * s_scr[p]
    o_ref[0] = (num / den).astype(BF16)


def _dilated_attention(proj, rel_bias, *, batch, seq):
    qi = np.arange(BLK)[:, None]
    kj = np.arange(2 * BLK)[None, :]
    step = BLK + qi - kj
    in_band = (step >= 0) & (step <= BLK)
    bk = np.stack([np.where(in_band, _np_bucket(step * d), -1) for d in DILATIONS]).astype(np.int32)
    pairs = A_HEADS // 2

    def col(t0):
        return pl.BlockSpec((1, seq, LANES), lambda b, p: (b, 0, t0 + p))

    return pl.pallas_call(
        functools.partial(_dil_kernel, seq=seq),
        grid=(batch, pairs),
        in_specs=[pl.BlockSpec(memory_space=pltpu.SMEM),
                  pl.BlockSpec((3, BLK, 2 * BLK), lambda b, p: (0, 0, 0)),
                  col(T_QA), col(T_KA), col(T_VA)],
        out_specs=pl.BlockSpec((1, seq, LANES), lambda b, p: (b, 0, p)),
        out_shape=jax.ShapeDtypeStruct((batch, seq, pairs * LANES), BF16),
        scratch_shapes=[pltpu.VMEM((pairs, 3, 2, BLK, 2 * BLK), F32),
                        pltpu.VMEM((3, seq, LANES), F32),
                        pltpu.VMEM((3, seq, LANES), F32),
                        pltpu.VMEM((3, seq, LANES), F32)],
        compiler_params=_cparams("arbitrary", "arbitrary"),
        name="dilated_attn",
    )(rel_bias, jnp.asarray(bk), proj, proj, proj)


def _mla_kernel(cq_ref, ckv_ref, small_ref, c_ref, s_ref, qn_ref, kvn_ref,
                wq_ref, wqs_ref, wk_ref, wv_ref, o_ref, q_scr, k_scr, v_scr, *, seq):
    rc = 512
    for c0 in range(0, seq, rc):
        rows = slice(c0, c0 + rc)
        lane_r = lax.broadcasted_iota(jnp.int32, (rc, LANES), 1)
        rot = (lane_r >= MLA_NOPE) & (lane_r < MLA_NOPE + MLA_ROPE)
        cos = jnp.where(rot, c_ref[0, rows, :], 1.0)
        sin = jnp.where(rot, s_ref[0, rows, :], 0.0)
        nq = _rms(cq_ref[0, rows, :], qn_ref[...]).astype(BF16)
        nkv = _rms(ckv_ref[0, rows, :], kvn_ref[...]).astype(BF16)
        small = small_ref[0, rows, :]
        partner = pltpu.roll(small, LANES - (L_KRS - L_KR), 1)
        kr = jnp.where(rot, small * cos + partner * sin, 0.0)
        for h in range(B_HEADS):
            cols = slice(h * LANES, (h + 1) * LANES)
            q_scr[h, rows, :] = (_dot(nq, wq_ref[:, cols]) * cos + _dot(nq, wqs_ref[:, cols]) * sin).astype(BF16)
            k_scr[h, rows, :] = (_dot(nkv, wk_ref[:, cols]) + kr).astype(BF16)
            v = _dot(nkv, wv_ref[:, cols])
            lane_c = lax.broadcasted_iota(jnp.int32, v.shape, 1)
            v_scr[h, rows, :] = jnp.where(lane_c < HEAD_DIM, v, 1.0).astype(BF16)

    exp2_scale = (MLA_NOPE + MLA_ROPE) ** -0.5 * math.log2(math.e)
    causal = lax.broadcasted_iota(jnp.int32, (QB, QB), 1) <= lax.broadcasted_iota(jnp.int32, (QB, QB), 0)
    low = lax.broadcasted_iota(jnp.int32, (QB, LANES), 1) < HEAD_DIM
    blocks = [(qb, h) for qb in range(seq // QB) for h in range(B_HEADS)]

    def qk(qb, h):
        return _dot_t(q_scr[h, qb * QB:(qb + 1) * QB, :], k_scr[h, 0:(qb + 1) * QB, :])

    nxt = qk(*blocks[0])
    outs = []
    for idx, (qb, h) in enumerate(blocks):
        qs, ext = qb * QB, (qb + 1) * QB
        logit = nxt
        if idx + 1 < len(blocks):
            nxt = qk(*blocks[idx + 1])
        diag = jnp.where(causal, logit[:, qs:], NEG)
        logit = diag if qb == 0 else jnp.concatenate([logit[:, :qs], diag], axis=1)
        m = jnp.max(logit, axis=-1, keepdims=True)
        e = jnp.exp2((logit - m) * exp2_scale).astype(BF16)
        pv = _dot(e, v_scr[h, 0:ext, :])
        outs.append(jnp.where(low, pv / pltpu.roll(pv, HEAD_DIM, 1), 0.0))
        if h == B_HEADS - 1:
            for pt in range((B_HEADS + 1) // 2):
                pair = outs[2 * pt]
                if 2 * pt + 1 < B_HEADS:
                    pair = pair + pltpu.roll(outs[2 * pt + 1], HEAD_DIM, 1)
                o_ref[0, qs:qs + QB, pt * LANES:(pt + 1) * LANES] = pair.astype(BF16)
            outs = []


def _mla_attention(proj, cos_m, sin_m, q_norm, kv_norm, wq, wqs, wk, wv, *, batch, seq):
    def col(t0, n):
        return pl.BlockSpec((1, seq, n * LANES), lambda b: (b, 0, t0 // n))

    def full(a):
        return pl.BlockSpec(a.shape, lambda b: (0,) * a.ndim)

    tab = pl.BlockSpec((1, seq, LANES), lambda b: (b, 0, 0))
    width = ((B_HEADS + 1) // 2) * LANES
    return pl.pallas_call(
        functools.partial(_mla_kernel, seq=seq),
        grid=(batch,),
        in_specs=[col(T_CQ, 2), col(T_CKV, 1), col(T_SMALL, 1), tab, tab,
                  full(q_norm), full(kv_norm), full(wq), full(wqs), full(wk), full(wv)],
        out_specs=pl.BlockSpec((1, seq, width), lambda b: (b, 0, 0)),
        out_shape=jax.ShapeDtypeStruct((batch, seq, width), BF16),
        scratch_shapes=[pltpu.VMEM((B_HEADS, seq, LANES), BF16)] * 3,
        compiler_params=_cparams("arbitrary"),
        name="mla_attn",
    )(proj, proj, proj, cos_m, sin_m, q_norm, kv_norm, wq, wqs, wk, wv)


def _sel_kernel(*refs, seq):
    for j in range(STEP_BLOCKS):
        _sel_block(j, *refs, seq=seq)


def _sel_block(j, iq_ref, iqs_ref, smallq_ref, cq_ref, sq_ref, ik_ref, smallk_ref, ck_ref, sk_ref, g_ref,
               o_ref, ik_scr, iqm_scr, key_scr, hb_scr, *, seq):
    i = STEP_BLOCKS * pl.program_id(1) + j
    qrows = slice(j * QB, (j + 1) * QB)
    qs = i * QB
    nk = seq // QB

    @pl.when(i == 0)
    def _():
        x = ik_ref[0]
        r = lax.rsqrt(jnp.mean(x * x, axis=-1, keepdims=True) + RMS_EPS)
        y = x * r * g_ref[0:1, :]
        small = smallk_ref[0]
        lane_k = lax.broadcasted_iota(jnp.int32, small.shape, 1)
        partner = jnp.where(lane_k < IDX_DIM, small, pltpu.roll(small, IDX_DIM, 1))
        ys = jnp.where(lane_k % IDX_DIM < IDX_ROPE, partner, 0.0) * r * g_ref[1:2, :]
        ik_scr[...] = (y * ck_ref[0] + ys * sk_ref[0]).astype(BF16)

    lane = lax.broadcasted_iota(jnp.int32, (QB, LANES), 1)
    low = lane < IDX_DIM
    cos = cq_ref[0, qrows, :]
    sin = sq_ref[0, qrows, :]
    for t in range(IDX_HEADS // 2):
        cols = slice(t * LANES, (t + 1) * LANES)
        rot = iq_ref[0, qrows, cols] * cos + iqs_ref[0, qrows, cols] * sin
        iqm_scr[2 * t] = jnp.where(low, rot, 0.0).astype(BF16)
        iqm_scr[2 * t + 1] = jnp.where(low, 0.0, rot).astype(BF16)
    w_t = (smallq_ref[0, qrows, :] * (IDX_HEADS ** -0.5 * IDX_DIM ** -0.5)).T

    krow = lax.broadcasted_iota(jnp.int32, (QB, QB), 0)
    qcol = lax.broadcasted_iota(jnp.int32, (QB, QB), 1)
    target = jnp.minimum(qs + lax.broadcasted_iota(jnp.int32, (1, QB), 1) + 1, TOPK).astype(F32)
    min_normal = float(np.finfo(np.float32).tiny)
    min_normal_bits = int(np.array(min_normal, np.float32).view(np.int32))
    hi_mask = -(1 << 16)
    one16, zero16 = jnp.ones((), BF16), jnp.zeros((), BF16)
    sub16 = QB // 16

    def block(n_tiles):
        iqm = iqm_scr[...].reshape(IDX_HEADS * QB, LANES)
        for kb in range(n_tiles):
            rel = _dot_t(ik_scr[kb * QB:(kb + 1) * QB, :], iqm)
            sc = jnp.zeros((QB, QB), F32)
            for h in range(IDX_HEADS):
                sc = sc + jnp.maximum(rel[:, h * QB:(h + 1) * QB], 0.0) * w_t[L_IW + h:L_IW + h + 1, :]
            sc = jnp.where(jnp.abs(sc) < min_normal, 0.0, sc)
            if kb == n_tiles - 1:
                sc = jnp.where(krow <= qcol, sc, -jnp.inf)
            bits = lax.bitcast_convert_type(sc, jnp.int32)
            key_scr[kb] = jnp.where(bits < 0, bits ^ KEY_FLIP, bits)
            hb_scr[kb] = lax.bitcast_convert_type(bits & hi_mask, F32).astype(BF16)

        def count_hi(cand16):
            acc = jnp.zeros((16, QB), BF16)
            for c in range(n_tiles):
                x = hb_scr[c].reshape(sub16, 16, QB)
                for r in range(sub16):
                    acc = acc + jnp.where(x[r] >= cand16, one16, zero16)
            return jnp.sum(acc.astype(F32), axis=0, keepdims=True)

        def count(pred):
            acc = jnp.zeros((8, QB), F32)
            for c in range(n_tiles):
                acc = acc + jnp.sum(pred(key_scr[c]).astype(F32).reshape(QB // 8, 8, QB), axis=0)
            return jnp.sum(acc, axis=0, keepdims=True)

        def hi_step(bi, prefix):
            cand_u = prefix | lax.shift_left(jnp.int32(1), 31 - bi)
            k = cand_u ^ INT_MIN
            cbits = jnp.where(k < 0, k ^ KEY_FLIP, k) & hi_mask
            cbits = jnp.where((cbits > 0) & (cbits < min_normal_bits), min_normal_bits, cbits)
            cand = lax.bitcast_convert_type(cbits, F32)
            cnt = count_hi(jnp.broadcast_to(cand, (16, QB)).astype(BF16))
            return jnp.where(cnt >= target, cand_u, prefix)

        def lo_step(bi, prefix):
            cand_u = prefix | lax.shift_left(jnp.int32(1), 15 - bi)
            cand = cand_u ^ INT_MIN
            cnt = count(lambda k: k >= cand)
            return jnp.where(cnt >= target, cand_u, prefix)

        prefix = lax.fori_loop(0, 16, hi_step, jnp.zeros((1, QB), jnp.int32))
        thr = lax.fori_loop(0, 16, lo_step, prefix) ^ INT_MIN
        need = target - count(lambda k: k > thr)

        lower = (qcol <= krow).astype(BF16)
        carry = jnp.zeros((1, QB), F32)
        for c in range(n_tiles):
            kc = key_scr[c]
            eq = kc == thr
            rank = carry + _dot(lower, eq.astype(BF16))
            sel = (kc > thr) | (eq & (rank <= need))
            o_ref[0, j, c] = jnp.where(sel, 0.0, NEG).T.astype(BF16)
            carry = rank[QB - 1:QB, :]
        for c in range(n_tiles, nk):
            o_ref[0, j, c] = jnp.full((QB, QB), NEG, BF16)

    for n_tiles in range(1 + j, nk + 1, STEP_BLOCKS):
        pl.when(i == n_tiles - 1)(functools.partial(block, n_tiles))


def _dsa_select(proj, cos_i, sin_i, ik_gains, *, batch, seq):
    nq = seq // QB

    assert nq % STEP_BLOCKS == 0
    step_rows = STEP_BLOCKS * QB

    def qcol(t0, n):
        return pl.BlockSpec((1, step_rows, n * LANES), lambda b, i: (b, i, t0 // n))

    def kcol(t0):
        return pl.BlockSpec((1, seq, LANES), lambda b, i: (b, 0, t0))

    qtab = pl.BlockSpec((1, step_rows, LANES), lambda b, i: (b, i, 0))
    ktab = pl.BlockSpec((1, seq, LANES), lambda b, i: (b, 0, 0))
    return pl.pallas_call(
        functools.partial(_sel_kernel, seq=seq),
        grid=(batch, nq // STEP_BLOCKS),
        in_specs=[qcol(T_IQ, 4), qcol(T_IQS, 4), qcol(T_SMALL, 1), qtab, qtab,
                  kcol(T_IK), kcol(T_SMALL), ktab, ktab,
                  pl.BlockSpec((2, LANES), lambda b, i: (0, 0))],
        out_specs=pl.BlockSpec((1, STEP_BLOCKS, nq, QB, QB), lambda b, i: (b, i, 0, 0, 0)),
        out_shape=jax.ShapeDtypeStruct((batch, nq, nq, QB, QB), BF16),
        scratch_shapes=[pltpu.VMEM((seq, LANES), BF16),
                        pltpu.VMEM((IDX_HEADS, QB, LANES), BF16),
                        pltpu.VMEM((nq, QB, QB), jnp.int32),
                        pltpu.VMEM((nq, QB, QB), BF16)],
        compiler_params=_cparams("arbitrary", "arbitrary"),
        name="dsa_select",
    )(proj, proj, proj, cos_i, sin_i, proj, proj, cos_i, sin_i, ik_gains)


def _dsa_kernel(tab_ref, bk_ref, q_ref, k_ref, v_ref, sel_ref, o_ref, kb_scr, vb_scr, bt_scr):
    b = pl.program_id(0)
    i = pl.program_id(1)

    n_pairs = (C_HEADS + 1) // 2
    heads_of = [min(2, C_HEADS - 2 * pt) for pt in range(n_pairs)]

    far = N_BUCKETS - 1

    @pl.when((b == 0) & (i == 0))
    def _():
        for pt in range(n_pairs):
            for hh in range(heads_of[pt]):
                col = A_HEADS + 2 * pt + hh
                for dd in range(2):
                    bt_scr[2 * pt + dd, hh * QB:(hh + 1) * QB, :] = (
                        _bias_from_buckets(bk_ref[dd], tab_ref, col) - tab_ref[far, col])

    @pl.when(i == 0)
    def _():
        kb_scr[...] = k_ref[0].astype(BF16)
        vb_scr[...] = v_ref[0].astype(BF16)

    lane = lax.broadcasted_iota(jnp.int32, (QB, LANES), 1)
    low = lane < HEAD_DIM

    def q_blocks(step):
        chains = [(STEP_BLOCKS * step + j, pt) for j in range(STEP_BLOCKS) for pt in range(n_pairs)]

        def qk(qb, pt):
            j = qb - STEP_BLOCKS * step
            cols = slice(pt * LANES, (pt + 1) * LANES)
            q = q_ref[0, j * QB:(j + 1) * QB, cols] * (HEAD_DIM ** -0.5)
            halves = [jnp.where(low, q, 0.0), jnp.where(low, 0.0, q)][:heads_of[pt]]
            return _dot_t(jnp.concatenate(halves, axis=0).astype(BF16), kb_scr[0:(qb + 1) * QB, cols])

        nxt = qk(*chains[0])
        for idx, (qb, pt) in enumerate(chains):
            j = qb - STEP_BLOCKS * step
            ext = (qb + 1) * QB
            heads = heads_of[pt]
            rows = heads * QB
            cols = slice(pt * LANES, (pt + 1) * LANES)
            logit = nxt
            if idx + 1 < len(chains):
                nxt = qk(*chains[idx + 1])
            pieces = []
            for c in range(qb + 1):
                t = logit[:, c * QB:(c + 1) * QB] + jnp.concatenate([sel_ref[0, j, c].astype(F32)] * heads, axis=0)
                if qb - c < 2:
                    t = t + bt_scr[2 * pt + (qb - c), 0:rows, :]
                pieces.append(t)
            logit = pieces[0] if qb == 0 else jnp.concatenate(pieces, axis=1)
            m = jnp.max(logit, axis=-1, keepdims=True)
            e = jnp.exp(logit - m)
            s = jnp.sum(e, axis=-1, keepdims=True)
            out = _dot(e.astype(BF16), vb_scr[0:ext, cols]) / s
            upper = out[QB:] if heads == 2 else 0.0
            o_ref[0, j * QB:(j + 1) * QB, cols] = jnp.where(low, out[:QB], upper).astype(BF16)

    for step in range(sel_ref.shape[2] // STEP_BLOCKS):
        pl.when(i == step)(functools.partial(q_blocks, step))


def _dsa_attention(proj, sel, rel_bias, *, batch, seq):
    nq = seq // QB
    qi = np.arange(QB)[:, None]
    kj = np.arange(QB)[None, :]
    bk = np.stack([_np_bucket(qi - kj), _np_bucket(QB + qi - kj)]).astype(np.int32)
    assert _np_bucket(np.array(QB + 1)) == N_BUCKETS - 1
    width = ((C_HEADS + 1) // 2) * LANES
    return pl.pallas_call(
        _dsa_kernel,
        grid=(batch, nq // STEP_BLOCKS),
        in_specs=[pl.BlockSpec(memory_space=pltpu.SMEM),
                  pl.BlockSpec((2, QB, QB), lambda b, i: (0, 0, 0)),
                  pl.BlockSpec((1, STEP_BLOCKS * QB, width), lambda b, i: (b, i, T_QC * LANES // width)),
                  pl.BlockSpec((1, seq, width), lambda b, i: (b, 0, T_KC * LANES // width)),
                  pl.BlockSpec((1, seq, width), lambda b, i: (b, 0, T_VC * LANES // width)),
                  pl.BlockSpec((1, STEP_BLOCKS, nq, QB, QB), lambda b, i: (b, i, 0, 0, 0))],
        out_specs=pl.BlockSpec((1, STEP_BLOCKS * QB, width), lambda b, i: (b, i, 0)),
        out_shape=jax.ShapeDtypeStruct((batch, seq, width), BF16),
        scratch_shapes=[pltpu.VMEM((seq, width), BF16),
                        pltpu.VMEM((seq, width), BF16),
                        pltpu.VMEM((2 * ((C_HEADS + 1) // 2), 2 * QB, QB), F32)],
        compiler_params=_cparams("arbitrary", "arbitrary"),
        name="dsa_attn",
    )(rel_bias, jnp.asarray(bk), proj, proj, proj, sel)


def _swap_halves(w, rope):
    half = rope // 2
    return jnp.concatenate([-w[..., half:rope], w[..., :half], jnp.zeros_like(w[..., rope:])], axis=-1)


def _pad_cols(w, n):
    return jnp.concatenate([w, jnp.zeros(w.shape[:-1] + (n - w.shape[-1],), w.dtype)], axis=-1)


def _layout_w_in(w):
    D = w.shape[0]
    sizes = (A_HEADS * HEAD_DIM,) * 3 + (MLA_Q_RANK, MLA_KV_RANK, MLA_ROPE) + (C_HEADS * HEAD_DIM,) * 3 + (
        IDX_HEADS * IDX_DIM, IDX_DIM, IDX_HEADS)
    offs = np.cumsum((0,) + sizes)
    qa, ka, va, cq, ckv, kr, qc, kc, vc, iq, ik, iw = [w[:, offs[n]:offs[n + 1]] for n in range(len(sizes))]
    c_w = ((C_HEADS + 1) // 2) * LANES
    iq_h = iq.reshape(D, IDX_HEADS, IDX_DIM)
    iq_sw = _swap_halves(iq_h, IDX_ROPE).reshape(D, IDX_HEADS * IDX_DIM)
    ik_sw = _swap_halves(ik, IDX_ROPE)[:, :IDX_ROPE]
    assert (L_IKS, L_IW, L_KR, L_KRS) == (0, IDX_ROPE, MLA_NOPE, MLA_NOPE + MLA_ROPE)
    small = jnp.concatenate([ik_sw, _pad_cols(iw, L_KR - L_IW), kr, _swap_halves(kr, MLA_ROPE)], axis=-1)
    cols = [qa, ka, va, _pad_cols(qc, c_w), _pad_cols(kc, c_w), _pad_cols(vc, c_w), cq, iq, iq_sw, ckv,
            small, jnp.concatenate([ik, ik], axis=-1)]
    out = jnp.concatenate(cols, axis=-1)
    assert out.shape[1] == N_TILES * LANES
    return out.astype(BF16)


def _layout_mla(w_uq, w_ukv):
    rq = w_uq.reshape(MLA_Q_RANK, B_HEADS, MLA_NOPE + MLA_ROPE)
    nope, rope = rq[..., :MLA_NOPE], rq[..., MLA_NOPE:]
    wq = _pad_cols(jnp.concatenate([nope, rope], axis=-1), LANES)
    wqs = _pad_cols(jnp.concatenate([jnp.zeros_like(nope), _swap_halves(rope, MLA_ROPE)], axis=-1), LANES)
    rkv = w_ukv.reshape(MLA_KV_RANK, B_HEADS, MLA_NOPE + HEAD_DIM)
    wk = _pad_cols(rkv[..., :MLA_NOPE], LANES)
    wv = _pad_cols(rkv[..., MLA_NOPE:], LANES)
    flat = lambda a, r: a.reshape(r, B_HEADS * LANES).astype(BF16)
    return flat(wq, MLA_Q_RANK), flat(wqs, MLA_Q_RANK), flat(wk, MLA_KV_RANK), flat(wv, MLA_KV_RANK)


def _layout_w_out(w):
    D = w.shape[1]
    a, b_, c_ = A_HEADS * HEAD_DIM, B_HEADS * HEAD_DIM, C_HEADS * HEAD_DIM
    def pad_rows(x, heads):
        return jnp.concatenate([x, jnp.zeros((((heads + 1) // 2) * LANES - x.shape[0], D), w.dtype)], axis=0)

    wa = w[:a]
    wb = pad_rows(w[a:a + b_], B_HEADS)
    wc = pad_rows(w[a + b_:a + b_ + c_], C_HEADS)
    return jnp.concatenate([wa, wc, wb], axis=0).astype(BF16)


def kernel(x, c, positions, rel_bias, ada_w, ada_b, norm_ffn1, ffn1_w_gu, ffn1_w_down, norm_mix, w_in, mla_q_norm, mla_w_uq, mla_kv_norm, mla_w_ukv, idx_k_norm, w_out, norm_ffn2, ffn2_w_gu, ffn2_w_down, final_norm):
    B, T, D = x.shape
    L = ada_w.shape[0]
    assert T % (max(DILATIONS) * BLK) == 0 and T % 512 == 0 and D % LANES == 0

    mods = _modulation(c, ada_w, ada_b).reshape(L, B, N_MOD, D)
    cos_i, sin_i = _rope_tables(positions)
    cos_m, sin_m = cos_i, sin_i
    final_gain = final_norm.reshape(1, D)
    h = x.reshape(B * T, D)
    for l in range(L):
        mod = mods[l]
        h = _ffn(h, mod, norm_ffn1[l].reshape(1, D), ffn1_w_gu[l].astype(BF16), ffn1_w_down[l].astype(BF16),
                 final_gain, ks=0, final=False, seq=T)

        proj = _in_proj(h, mod, norm_mix[l].reshape(1, D), _layout_w_in(w_in[l]), seq=T)
        proj = proj.reshape(B, T, N_TILES * LANES)
        oa = _dilated_attention(proj, rel_bias, batch=B, seq=T)
        wq, wqs, wk, wv = _layout_mla(mla_w_uq[l], mla_w_ukv[l])
        ob = _mla_attention(proj, cos_m, sin_m, mla_q_norm[l].reshape(1, -1), mla_kv_norm[l].reshape(1, -1),
                            wq, wqs, wk, wv, batch=B, seq=T)
        g = idx_k_norm[l]
        g_sw = jnp.concatenate([g[IDX_ROPE // 2:IDX_ROPE], g[:IDX_ROPE // 2], jnp.zeros_like(g[IDX_ROPE:])])
        ik_gains = jnp.stack([jnp.concatenate([g, g]), jnp.concatenate([g_sw, g_sw])])
        sel = _dsa_select(proj, cos_i, sin_i, ik_gains, batch=B, seq=T)
        oc = _dsa_attention(proj, sel, rel_bias, batch=B, seq=T)
        h = _mix_ffn(oa.reshape(B * T, -1), oc.reshape(B * T, -1), ob.reshape(B * T, -1), _layout_w_out(w_out[l]), h, mod,
                     norm_ffn2[l].reshape(1, D), ffn2_w_gu[l].astype(BF16), ffn2_w_down[l].astype(BF16),
                     final_gain, final=(l == L - 1), seq=T)
    return h.reshape(B, T, D)
```

```python
import functools
import math

import numpy as np
import jax
import jax.numpy as jnp
from jax import lax
from jax.experimental import pallas as pl
from jax.experimental.pallas import tpu as pltpu

F32 = jnp.float32
BF16 = jnp.bfloat16

HEAD_DIM = 64
A_HEADS, B_HEADS, C_HEADS = 6, 5, 5
MLA_Q_RANK, MLA_KV_RANK, MLA_NOPE, MLA_ROPE = 256, 128, 64, 32
IDX_HEADS, IDX_DIM, IDX_ROPE = 8, 64, 32
TOPK = 256
N_BUCKETS, MAX_DISTANCE = 32, 128
ROPE_THETA = 10000.0
RMS_EPS = 1e-6
N_MOD = 9
NEG = -1e30
DILATIONS = (1, 4, 16)

LANES = 128
BLK = 128
QB = 256
STEP_BLOCKS = 4
FFN_ROWS = 1024
FFN_CHUNK = 256
PROJ_ROWS = 1024
VMEM_LIMIT = 56 * 1024 * 1024

T_QA, T_KA, T_VA = 0, 3, 6
T_QC, T_KC, T_VC = 9, 12, 15
T_CQ, T_IQ, T_IQS = 18, 20, 24
T_CKV, T_SMALL, T_IK = 28, 29, 30
N_TILES = 31
L_IKS, L_IW, L_KR, L_KRS = 0, 32, 64, 96

KEY_FLIP = 0x7FFFFFFF
INT_MIN = -(2 ** 31)


def _cparams(*sem):
    return pltpu.CompilerParams(dimension_semantics=sem, vmem_limit_bytes=VMEM_LIMIT)


def _rms(x, g):
    return x * lax.rsqrt(jnp.mean(x * x, axis=-1, keepdims=True) + RMS_EPS) * g


def _dot(a, b):
    return jnp.dot(a, b, preferred_element_type=F32)


def _dot_t(a, b):
    return lax.dot_general(a, b, (((1,), (1,)), ((), ())), preferred_element_type=F32)


def _np_bucket(dist):
    n = np.maximum(dist, 0)
    max_exact = N_BUCKETS // 2
    nf = np.maximum(n, 1).astype(np.float32)
    large = max_exact + (np.log(nf / np.float32(max_exact)) / np.float32(math.log(MAX_DISTANCE / max_exact))
                         * np.float32(N_BUCKETS - max_exact)).astype(np.int32)
    large = np.minimum(large, N_BUCKETS - 1)
    return np.where(n < max_exact, n, large).astype(np.int32)


def _bias_from_buckets(bucket, tab_ref, col):
    acc = jnp.full(bucket.shape, NEG, F32)
    for b in range(N_BUCKETS):
        acc = jnp.where(bucket == b, tab_ref[b, col], acc)
    return acc


def _mod_kernel(c_ref, w_ref, b_ref, o_ref):
    c = c_ref[...]
    cond = c / (1.0 + jnp.exp(-c))
    w = w_ref[0]
    c_hi = cond.astype(BF16)
    c_lo = (cond - c_hi.astype(F32)).astype(BF16)
    w_hi = w.astype(BF16)
    w_lo = (w - w_hi.astype(F32)).astype(BF16)
    o_ref[0] = _dot(c_hi, w_hi) + _dot(c_hi, w_lo) + _dot(c_lo, w_hi) + b_ref[0]


def _modulation(c, ada_w, ada_b):
    L, D, N = ada_w.shape
    B = c.shape[0]
    tn = 1024
    return pl.pallas_call(
        _mod_kernel,
        grid=(L, N // tn),
        in_specs=[pl.BlockSpec((B, D), lambda l, j: (0, 0)),
                  pl.BlockSpec((1, D, tn), lambda l, j: (l, 0, j)),
                  pl.BlockSpec((1, 1, tn), lambda l, j: (l, 0, j))],
        out_specs=pl.BlockSpec((1, B, tn), lambda l, j: (l, 0, j)),
        out_shape=jax.ShapeDtypeStruct((L, B, N), F32),
        compiler_params=_cparams("arbitrary", "arbitrary"),
        name="adaln_mod",
    )(c, ada_w, ada_b.reshape(L, 1, N))


def _ffn_kernel(h_ref, mod_ref, g_ref, wgu_ref, wd_ref, fg_ref, o_ref, *, ks, final, tf):
    _ffn_body(h_ref[...], mod_ref, g_ref, wgu_ref, wd_ref, fg_ref, o_ref, ks=ks, final=final, tf=tf)


def _mix_ffn_kernel(oa_ref, oc_ref, ob_ref, wo_ref, h_ref, mod_ref, g_ref, wgu_ref, wd_ref, fg_ref,
                    o_ref, *, final, tf):
    heads = jnp.concatenate([oa_ref[...], oc_ref[...], ob_ref[...]], axis=1)
    x = h_ref[...] + mod_ref[0, 5:6, :] * _dot(heads, wo_ref[...])
    _ffn_body(x, mod_ref, g_ref, wgu_ref, wd_ref, fg_ref, o_ref, ks=6, final=final, tf=tf)


def _ffn_body(x, mod_ref, g_ref, wgu_ref, wd_ref, fg_ref, o_ref, *, ks, final, tf):
    F = wd_ref.shape[0]
    y = _rms(x, g_ref[...])
    n = (y * (1.0 + mod_ref[0, ks + 1:ks + 2, :]) + mod_ref[0, ks:ks + 1, :]).astype(BF16)
    acc = jnp.zeros(x.shape, F32)
    for f0 in range(0, F, tf):
        g = _dot(n, wgu_ref[:, f0:f0 + tf])
        u = _dot(n, wgu_ref[:, F + f0:F + f0 + tf])
        a = (g / (1.0 + jnp.exp(-g))) * u
        acc = acc + _dot(a.astype(BF16), wd_ref[f0:f0 + tf, :])
    out = x + (0.5 * mod_ref[0, ks + 2:ks + 3, :]) * acc
    if final:
        out = _rms(out, fg_ref[...])
    o_ref[...] = out


def _resident(a):
    return pl.BlockSpec(a.shape, lambda *_: (0,) * a.ndim, pipeline_mode=pl.Buffered(1))


def _ffn(h, mod, gain, w_gu, w_down, final_gain, *, ks, final, seq):
    M, D = h.shape
    F = w_down.shape[0]
    tm, tf = FFN_ROWS, FFN_CHUNK
    assert F % tf == 0 and seq % tm == 0
    per_b = seq // tm
    return pl.pallas_call(
        functools.partial(_ffn_kernel, ks=ks, final=final, tf=tf),
        grid=(M // tm,),
        in_specs=[pl.BlockSpec((tm, D), lambda i: (i, 0)),
                  pl.BlockSpec((1, N_MOD, D), lambda i: (i // per_b, 0, 0)),
                  pl.BlockSpec((1, D), lambda i: (0, 0)),
                  _resident(w_gu), _resident(w_down),
                  pl.BlockSpec((1, D), lambda i: (0, 0))],
        out_specs=pl.BlockSpec((tm, D), lambda i: (i, 0)),
        out_shape=jax.ShapeDtypeStruct((M, D), F32),
        compiler_params=_cparams("arbitrary"),
        name="ffn",
    )(h, mod, gain, w_gu, w_down, final_gain)


def _mix_ffn(oa, oc, ob, w_o, h, mod, gain, w_gu, w_down, final_gain, *, final, seq):
    M, D = h.shape
    tm, tf = FFN_ROWS, FFN_CHUNK
    assert w_down.shape[0] % tf == 0 and seq % tm == 0
    per_b = seq // tm

    def rows(a):
        return pl.BlockSpec((tm, a.shape[1]), lambda i: (i, 0))

    return pl.pallas_call(
        functools.partial(_mix_ffn_kernel, final=final, tf=tf),
        grid=(M // tm,),
        in_specs=[rows(oa), rows(oc), rows(ob), _resident(w_o), rows(h),
                  pl.BlockSpec((1, N_MOD, D), lambda i: (i // per_b, 0, 0)),
                  pl.BlockSpec((1, D), lambda i: (0, 0)),
                  _resident(w_gu), _resident(w_down),
                  pl.BlockSpec((1, D), lambda i: (0, 0))],
        out_specs=pl.BlockSpec((tm, D), lambda i: (i, 0)),
        out_shape=jax.ShapeDtypeStruct((M, D), F32),
        compiler_params=_cparams("arbitrary"),
        name="mix_ffn",
    )(oa, oc, ob, w_o, h, mod, gain, w_gu, w_down, final_gain)


def _win_kernel(h_ref, mod_ref, g_ref, w_ref, o_ref, *, tn):
    y = _rms(h_ref[...], g_ref[...])
    n = (y * (1.0 + mod_ref[0, 4:5, :]) + mod_ref[0, 3:4, :]).astype(BF16)
    width = w_ref.shape[1]
    for c0 in range(0, width, tn):
        c1 = min(c0 + tn, width)
        o_ref[:, c0:c1] = _dot(n, w_ref[:, c0:c1])


def _in_proj(h, mod, gain, w, *, seq):
    M, D = h.shape
    N = w.shape[1]
    tm, tn = PROJ_ROWS, 2 * LANES
    assert N % LANES == 0
    per_b = seq // tm
    return pl.pallas_call(
        functools.partial(_win_kernel, tn=tn),
        grid=(M // tm,),
        in_specs=[pl.BlockSpec((tm, D), lambda i: (i, 0)),
                  pl.BlockSpec((1, N_MOD, D), lambda i: (i // per_b, 0, 0)),
                  pl.BlockSpec((1, D), lambda i: (0, 0)),
                  _resident(w)],
        out_specs=pl.BlockSpec((tm, N), lambda i: (i, 0)),
        out_shape=jax.ShapeDtypeStruct((M, N), F32),
        compiler_params=_cparams("arbitrary"),
        name="in_proj",
    )(h, mod, gain, w)


def _rope_kernel(pos_ref, inv_ref, c_ref, s_ref):
    ang = pos_ref[0].astype(F32) * inv_ref[...]
    c_ref[0] = jnp.cos(ang)
    s_ref[0] = jnp.sin(ang)


def _rope_tables(positions):
    B, T = positions.shape
    half = IDX_ROPE // 2
    assert MLA_ROPE == IDX_ROPE and MLA_NOPE == IDX_DIM
    inv = (ROPE_THETA ** (-np.arange(half, dtype=np.float32) / half)).astype(np.float32)
    zeros32 = np.zeros(IDX_DIM - IDX_ROPE, np.float32)
    inv_row = jnp.asarray(np.concatenate([inv, inv, zeros32, inv, inv, zeros32])[None, :])
    tab = jax.ShapeDtypeStruct((B, T, LANES), F32)
    spec = pl.BlockSpec((1, T, LANES), lambda b: (b, 0, 0))
    return pl.pallas_call(
        _rope_kernel,
        grid=(B,),
        in_specs=[pl.BlockSpec((1, T, 1), lambda b: (b, 0, 0)),
                  pl.BlockSpec((1, LANES), lambda b: (0, 0))],
        out_specs=[spec, spec],
        out_shape=[tab, tab],
        compiler_params=_cparams("arbitrary"),
        name="rope_tables",
    )(positions.reshape(B, T, 1), inv_row)


def _dil_kernel(tab_ref, bk_ref, q_ref, k_ref, v_ref, o_ref, bt_scr, o_scr, m_scr, s_scr, *, seq):
    b = pl.program_id(0)
    pair = pl.program_id(1)

    @pl.when(b == 0)
    def _():
        for p in range(3):
            for hh in range(2):
                bt_scr[pair, p, hh] = _bias_from_buckets(bk_ref[p], tab_ref, 2 * pair + hh)

    nblk = seq // BLK
    low = lax.broadcasted_iota(jnp.int32, (nblk, BLK, LANES), 2) < HEAD_DIM
    neg_half = jnp.full((BLK, BLK), NEG, F32)

    for p, d in enumerate(DILATIONS):
        nb = seq // (d * BLK)
        starts = [r + d * BLK * n for n in range(nb) for r in range(d)]

        def rows_of(s0, d=d):
            return pl.ds(s0, BLK) if d == 1 else pl.ds(s0, BLK, stride=d)

        def regroup(ref):
            return jnp.stack([ref[0, rows_of(s0), :] for s0 in starts])

        def band(z):
            if nb == 1:
                return z
            prev = jnp.concatenate([jnp.zeros((d,) + z.shape[1:], z.dtype), z[:-d]], axis=0)
            return jnp.concatenate([prev, z], axis=1)

        q = regroup(q_ref) * (HEAD_DIM ** -0.5)
        kband = band(regroup(k_ref).astype(BF16))
        vband = band(regroup(v_ref).astype(BF16))
        res = []
        for hh in range(2):
            qh = (jnp.where(low, q, 0.0) if hh == 0 else jnp.where(low, 0.0, q)).astype(BF16)
            logit = lax.dot_general(qh, kband, (((2,), (2,)), ((0,), (0,))), preferred_element_type=F32)
            bias = bt_scr[pair, p, hh]
            if nb == 1:
                logit = logit + bias[:, BLK:][None]
            else:
                first = jnp.concatenate([neg_half, bias[:, BLK:]], axis=1)
                logit = jnp.concatenate([logit[:d] + first[None], logit[d:] + bias[None]], axis=0)
            m = jnp.max(logit, axis=-1, keepdims=True)
            e = jnp.exp(logit - m)
            s = jnp.sum(e, axis=-1, keepdims=True)
            o = lax.dot_general(e.astype(BF16), vband, (((2,), (1,)), ((0,), (0,))), preferred_element_type=F32)
            res.append((m, s, o))
        o_all = jnp.where(low, res[0][2], res[1][2])
        m_all = jnp.where(low, res[0][0], res[1][0])
        s_all = jnp.where(low, res[0][1], res[1][1])
        for bi, s0 in enumerate(starts):
            o_scr[p, rows_of(s0), :] = o_all[bi]
            m_scr[p, rows_of(s0), :] = m_all[bi]
            s_scr[p, rows_of(s0), :] = s_all[bi]

    m_all = jnp.maximum(jnp.maximum(m_scr[0], m_scr[1]), m_scr[2])
    num = jnp.zeros((seq, LANES), F32)
    den = jnp.zeros((seq, LANES), F32)
    for p in range(3):
        w = jnp.exp(m_scr[p] - m_all)
        num = num + w * o_scr[p]
        den = den + w * s_scr[p]
    o_ref[0] = (num / den).astype(BF16)


def _dilated_attention(proj, rel_bias, *, batch, seq):
    qi = np.arange(BLK)[:, None]
    kj = np.arange(2 * BLK)[None, :]
    step = BLK + qi - kj
    in_band = (step >= 0) & (step <= BLK)
    bk = np.stack([np.where(in_band, _np_bucket(step * d), -1) for d in DILATIONS]).astype(np.int32)
    pairs = A_HEADS // 2

    def col(t0):
        return pl.BlockSpec((1, seq, LANES), lambda b, p: (b, 0, t0 + p))

    return pl.pallas_call(
        functools.partial(_dil_kernel, seq=seq),
        grid=(batch, pairs),
        in_specs=[pl.BlockSpec(memory_space=pltpu.SMEM),
                  pl.BlockSpec((3, BLK, 2 * BLK), lambda b, p: (0, 0, 0)),
                  col(T_QA), col(T_KA), col(T_VA)],
        out_specs=pl.BlockSpec((1, seq, LANES), lambda b, p: (b, 0, p)),
        out_shape=jax.ShapeDtypeStruct((batch, seq, pairs * LANES), BF16),
        scratch_shapes=[pltpu.VMEM((pairs, 3, 2, BLK, 2 * BLK), F32),
                        pltpu.VMEM((3, seq, LANES), F32),
                        pltpu.VMEM((3, seq, LANES), F32),
                        pltpu.VMEM((3, seq, LANES), F32)],
        compiler_params=_cparams("arbitrary", "arbitrary"),
        name="dilated_attn",
    )(rel_bias, jnp.asarray(bk), proj, proj, proj)


def _mla_kernel(cq_ref, ckv_ref, small_ref, c_ref, s_ref, qn_ref, kvn_ref,
                wq_ref, wqs_ref, wk_ref, wv_ref, o_ref, q_scr, k_scr, v_scr, *, seq):
    rc = 512
    for c0 in range(0, seq, rc):
        rows = slice(c0, c0 + rc)
        lane_r = lax.broadcasted_iota(jnp.int32, (rc, LANES), 1)
        rot = (lane_r >= MLA_NOPE) & (lane_r < MLA_NOPE + MLA_ROPE)
        cos = jnp.where(rot, c_ref[0, rows, :], 1.0)
        sin = jnp.where(rot, s_ref[0, rows, :], 0.0)
        nq = _rms(cq_ref[0, rows, :], qn_ref[...]).astype(BF16)
        nkv = _rms(ckv_ref[0, rows, :], kvn_ref[...]).astype(BF16)
        small = small_ref[0, rows, :]
        partner = pltpu.roll(small, LANES - (L_KRS - L_KR), 1)
        kr = jnp.where(rot, small * cos + partner * sin, 0.0)
        for h in range(B_HEADS):
            cols = slice(h * LANES, (h + 1) * LANES)
            q_scr[h, rows, :] = (_dot(nq, wq_ref[:, cols]) * cos + _dot(nq, wqs_ref[:, cols]) * sin).astype(BF16)
            k_scr[h, rows, :] = (_dot(nkv, wk_ref[:, cols]) + kr).astype(BF16)
            v = _dot(nkv, wv_ref[:, cols])
            lane_c = lax.broadcasted_iota(jnp.int32, v.shape, 1)
            v_scr[h, rows, :] = jnp.where(lane_c < HEAD_DIM, v, 1.0).astype(BF16)

    exp2_scale = (MLA_NOPE + MLA_ROPE) ** -0.5 * math.log2(math.e)
    causal = lax.broadcasted_iota(jnp.int32, (QB, QB), 1) <= lax.broadcasted_iota(jnp.int32, (QB, QB), 0)
    low = lax.broadcasted_iota(jnp.int32, (QB, LANES), 1) < HEAD_DIM
    blocks = [(qb, h) for qb in range(seq // QB) for h in range(B_HEADS)]

    def qk(qb, h):
        return _dot_t(q_scr[h, qb * QB:(qb + 1) * QB, :], k_scr[h, 0:(qb + 1) * QB, :])

    nxt = qk(*blocks[0])
    outs = []
    for idx, (qb, h) in enumerate(blocks):
        qs, ext = qb * QB, (qb + 1) * QB
        logit = nxt
        if idx + 1 < len(blocks):
            nxt = qk(*blocks[idx + 1])
        diag = jnp.where(causal, logit[:, qs:], NEG)
        logit = diag if qb == 0 else jnp.concatenate([logit[:, :qs], diag], axis=1)
        m = jnp.max(logit, axis=-1, keepdims=True)
        e = jnp.exp2((logit - m) * exp2_scale).astype(BF16)
        pv = _dot(e, v_scr[h, 0:ext, :])
        outs.append(jnp.where(low, pv / pltpu.roll(pv, HEAD_DIM, 1), 0.0))
        if h == B_HEADS - 1:
            for pt in range((B_HEADS + 1) // 2):
                pair = outs[2 * pt]
                if 2 * pt + 1 < B_HEADS:
                    pair = pair + pltpu.roll(outs[2 * pt + 1], HEAD_DIM, 1)
                o_ref[0, qs:qs + QB, pt * LANES:(pt + 1) * LANES] = pair.astype(BF16)
            outs = []


def _mla_attention(proj, cos_m, sin_m, q_norm, kv_norm, wq, wqs, wk, wv, *, batch, seq):
    def col(t0, n):
        return pl.BlockSpec((1, seq, n * LANES), lambda b: (b, 0, t0 // n))

    def full(a):
        return pl.BlockSpec(a.shape, lambda b: (0,) * a.ndim)

    tab = pl.BlockSpec((1, seq, LANES), lambda b: (b, 0, 0))
    width = ((B_HEADS + 1) // 2) * LANES
    return pl.pallas_call(
        functools.partial(_mla_kernel, seq=seq),
        grid=(batch,),
        in_specs=[col(T_CQ, 2), col(T_CKV, 1), col(T_SMALL, 1), tab, tab,
                  full(q_norm), full(kv_norm), full(wq), full(wqs), full(wk), full(wv)],
        out_specs=pl.BlockSpec((1, seq, width), lambda b: (b, 0, 0)),
        out_shape=jax.ShapeDtypeStruct((batch, seq, width), BF16),
        scratch_shapes=[pltpu.VMEM((B_HEADS, seq, LANES), BF16)] * 3,
        compiler_params=_cparams("arbitrary"),
        name="mla_attn",
    )(proj, proj, proj, cos_m, sin_m, q_norm, kv_norm, wq, wqs, wk, wv)


def _sel_kernel(*refs, seq):
    for j in range(STEP_BLOCKS):
        _sel_block(j, *refs, seq=seq)


def _sel_block(j, iq_ref, iqs_ref, smallq_ref, cq_ref, sq_ref, ik_ref, smallk_ref, ck_ref, sk_ref, g_ref,
               o_ref, ik_scr, iqm_scr, key_scr, hb_scr, *, seq):
    i = STEP_BLOCKS * pl.program_id(1) + j
    qrows = slice(j * QB, (j + 1) * QB)
    qs = i * QB
    nk = seq // QB

    @pl.when(i == 0)
    def _():
        x = ik_ref[0]
        r = lax.rsqrt(jnp.mean(x * x, axis=-1, keepdims=True) + RMS_EPS)
        y = x * r * g_ref[0:1, :]
        small = smallk_ref[0]
        lane_k = lax.broadcasted_iota(jnp.int32, small.shape, 1)
        partner = jnp.where(lane_k < IDX_DIM, small, pltpu.roll(small, IDX_DIM, 1))
        ys = jnp.where(lane_k % IDX_DIM < IDX_ROPE, partner, 0.0) * r * g_ref[1:2, :]
        ik_scr[...] = (y * ck_ref[0] + ys * sk_ref[0]).astype(BF16)

    lane = lax.broadcasted_iota(jnp.int32, (QB, LANES), 1)
    low = lane < IDX_DIM
    cos = cq_ref[0, qrows, :]
    sin = sq_ref[0, qrows, :]
    for t in range(IDX_HEADS // 2):
        cols = slice(t * LANES, (t + 1) * LANES)
        rot = iq_ref[0, qrows, cols] * cos + iqs_ref[0, qrows, cols] * sin
        iqm_scr[2 * t] = jnp.where(low, rot, 0.0).astype(BF16)
        iqm_scr[2 * t + 1] = jnp.where(low, 0.0, rot).astype(BF16)
    w_t = (smallq_ref[0, qrows, :] * (IDX_HEADS ** -0.5 * IDX_DIM ** -0.5)).T

    krow = lax.broadcasted_iota(jnp.int32, (QB, QB), 0)
    qcol = lax.broadcasted_iota(jnp.int32, (QB, QB), 1)
    target = jnp.minimum(qs + lax.broadcasted_iota(jnp.int32, (1, QB), 1) + 1, TOPK).astype(F32)
    min_normal = float(np.finfo(np.float32).tiny)
    min_normal_bits = int(np.array(min_normal, np.float32).view(np.int32))
    hi_mask = -(1 << 16)
    one16, zero16 = jnp.ones((), BF16), jnp.zeros((), BF16)
    sub16 = QB // 16

    def block(n_tiles):
        iqm = iqm_scr[...].reshape(IDX_HEADS * QB, LANES)
        for kb in range(n_tiles):
            rel = _dot_t(ik_scr[kb * QB:(kb + 1) * QB, :], iqm)
            sc = jnp.zeros((QB, QB), F32)
            for h in range(IDX_HEADS):
                sc = sc + jnp.maximum(rel[:, h * QB:(h + 1) * QB], 0.0) * w_t[L_IW + h:L_IW + h + 1, :]
            sc = jnp.where(jnp.abs(sc) < min_normal, 0.0, sc)
            if kb == n_tiles - 1:
                sc = jnp.where(krow <= qcol, sc, -jnp.inf)
            bits = lax.bitcast_convert_type(sc, jnp.int32)
            key_scr[kb] = jnp.where(bits < 0, bits ^ KEY_FLIP, bits)
            hb_scr[kb] = lax.bitcast_convert_type(bits & hi_mask, F32).astype(BF16)

        def count_hi(cand16):
            acc = jnp.zeros((16, QB), BF16)
            for c in range(n_tiles):
                x = hb_scr[c].reshape(sub16, 16, QB)
                for r in range(sub16):
                    acc = acc + jnp.where(x[r] >= cand16, one16, zero16)
            return jnp.sum(acc.astype(F32), axis=0, keepdims=True)

        def count(pred):
            acc = jnp.zeros((8, QB), F32)
            for c in range(n_tiles):
                acc = acc + jnp.sum(pred(key_scr[c]).astype(F32).reshape(QB // 8, 8, QB), axis=0)
            return jnp.sum(acc, axis=0, keepdims=True)

        def hi_step(bi, prefix):
            cand_u = prefix | lax.shift_left(jnp.int32(1), 31 - bi)
            k = cand_u ^ INT_MIN
            cbits = jnp.where(k < 0, k ^ KEY_FLIP, k) & hi_mask
            cbits = jnp.where((cbits > 0) & (cbits < min_normal_bits), min_normal_bits, cbits)
            cand = lax.bitcast_convert_type(cbits, F32)
            cnt = count_hi(jnp.broadcast_to(cand, (16, QB)).astype(BF16))
            return jnp.where(cnt >= target, cand_u, prefix)

        def lo_step(bi, prefix):
            cand_u = prefix | lax.shift_left(jnp.int32(1), 15 - bi)
            cand = cand_u ^ INT_MIN
            cnt = count(lambda k: k >= cand)
            return jnp.where(cnt >= target, cand_u, prefix)

        prefix = lax.fori_loop(0, 16, hi_step, jnp.zeros((1, QB), jnp.int32))
        thr = lax.fori_loop(0, 16, lo_step, prefix) ^ INT_MIN
        need = target - count(lambda k: k > thr)

        lower = (qcol <= krow).astype(BF16)
        carry = jnp.zeros((1, QB), F32)
        for c in range(n_tiles):
            kc = key_scr[c]
            eq = kc == thr
            rank = carry + _dot(lower, eq.astype(BF16))
            sel = (kc > thr) | (eq & (rank <= need))
            o_ref[0, j, c] = jnp.where(sel, 0.0, NEG).T.astype(BF16)
            carry = rank[QB - 1:QB, :]
        for c in range(n_tiles, nk):
            o_ref[0, j, c] = jnp.full((QB, QB), NEG, BF16)

    for n_tiles in range(1 + j, nk + 1, STEP_BLOCKS):
        pl.when(i == n_tiles - 1)(functools.partial(block, n_tiles))


def _dsa_select(proj, cos_i, sin_i, ik_gains, *, batch, seq):
    nq = seq // QB

    assert nq % STEP_BLOCKS == 0
    step_rows = STEP_BLOCKS * QB

    def qcol(t0, n):
        return pl.BlockSpec((1, step_rows, n * LANES), lambda b, i: (b, i, t0 // n))

    def kcol(t0):
        return pl.BlockSpec((1, seq, LANES), lambda b, i: (b, 0, t0))

    qtab = pl.BlockSpec((1, step_rows, LANES), lambda b, i: (b, i, 0))
    ktab = pl.BlockSpec((1, seq, LANES), lambda b, i: (b, 0, 0))
    return pl.pallas_call(
        functools.partial(_sel_kernel, seq=seq),
        grid=(batch, nq // STEP_BLOCKS),
        in_specs=[qcol(T_IQ, 4), qcol(T_IQS, 4), qcol(T_SMALL, 1), qtab, qtab,
                  kcol(T_IK), kcol(T_SMALL), ktab, ktab,
                  pl.BlockSpec((2, LANES), lambda b, i: (0, 0))],
        out_specs=pl.BlockSpec((1, STEP_BLOCKS, nq, QB, QB), lambda b, i: (b, i, 0, 0, 0)),
        out_shape=jax.ShapeDtypeStruct((batch, nq, nq, QB, QB), BF16),
        scratch_shapes=[pltpu.VMEM((seq, LANES), BF16),
                        pltpu.VMEM((IDX_HEADS, QB, LANES), BF16),
                        pltpu.VMEM((nq, QB, QB), jnp.int32),
                        pltpu.VMEM((nq, QB, QB), BF16)],
        compiler_params=_cparams("arbitrary", "arbitrary"),
        name="dsa_select",
    )(proj, proj, proj, cos_i, sin_i, proj, proj, cos_i, sin_i, ik_gains)


def _dsa_kernel(tab_ref, bk_ref, q_ref, k_ref, v_ref, sel_ref, o_ref, kb_scr, vb_scr, bt_scr):
    b = pl.program_id(0)
    i = pl.program_id(1)

    n_pairs = (C_HEADS + 1) // 2
    heads_of = [min(2, C_HEADS - 2 * pt) for pt in range(n_pairs)]

    far = N_BUCKETS - 1

    @pl.when((b == 0) & (i == 0))
    def _():
        for pt in range(n_pairs):
            for hh in range(heads_of[pt]):
                col = A_HEADS + 2 * pt + hh
                for dd in range(2):
                    bt_scr[2 * pt + dd, hh * QB:(hh + 1) * QB, :] = (
                        _bias_from_buckets(bk_ref[dd], tab_ref, col) - tab_ref[far, col])

    @pl.when(i == 0)
    def _():
        kb_scr[...] = k_ref[0].astype(BF16)
        vb_scr[...] = v_ref[0].astype(BF16)

    lane = lax.broadcasted_iota(jnp.int32, (QB, LANES), 1)
    low = lane < HEAD_DIM

    def q_blocks(step):
        chains = [(STEP_BLOCKS * step + j, pt) for j in range(STEP_BLOCKS) for pt in range(n_pairs)]

        def qk(qb, pt):
            j = qb - STEP_BLOCKS * step
            cols = slice(pt * LANES, (pt + 1) * LANES)
            q = q_ref[0, j * QB:(j + 1) * QB, cols] * (HEAD_DIM ** -0.5)
            halves = [jnp.where(low, q, 0.0), jnp.where(low, 0.0, q)][:heads_of[pt]]
            return _dot_t(jnp.concatenate(halves, axis=0).astype(BF16), kb_scr[0:(qb + 1) * QB, cols])

        nxt = qk(*chains[0])
        for idx, (qb, pt) in enumerate(chains):
            j = qb - STEP_BLOCKS * step
            ext = (qb + 1) * QB
            heads = heads_of[pt]
            rows = heads * QB
            cols = slice(pt * LANES, (pt + 1) * LANES)
            logit = nxt
            if idx + 1 < len(chains):
                nxt = qk(*chains[idx + 1])
            pieces = []
            for c in range(qb + 1):
                t = logit[:, c * QB:(c + 1) * QB] + jnp.concatenate([sel_ref[0, j, c].astype(F32)] * heads, axis=0)
                if qb - c < 2:
                    t = t + bt_scr[2 * pt + (qb - c), 0:rows, :]
                pieces.append(t)
            logit = pieces[0] if qb == 0 else jnp.concatenate(pieces, axis=1)
            m = jnp.max(logit, axis=-1, keepdims=True)
            e = jnp.exp(logit - m)
            s = jnp.sum(e, axis=-1, keepdims=True)
            out = _dot(e.astype(BF16), vb_scr[0:ext, cols]) / s
            upper = out[QB:] if heads == 2 else 0.0
            o_ref[0, j * QB:(j + 1) * QB, cols] = jnp.where(low, out[:QB], upper).astype(BF16)

    for step in range(sel_ref.shape[2] // STEP_BLOCKS):
        pl.when(i == step)(functools.partial(q_blocks, step))


def _dsa_attention(proj, sel, rel_bias, *, batch, seq):
    nq = seq // QB
    qi = np.arange(QB)[:, None]
    kj = np.arange(QB)[None, :]
    bk = np.stack([_np_bucket(qi - kj), _np_bucket(QB + qi - kj)]).astype(np.int32)
    assert _np_bucket(np.array(QB + 1)) == N_BUCKETS - 1
    width = ((C_HEADS + 1) // 2) * LANES
    return pl.pallas_call(
        _dsa_kernel,
        grid=(batch, nq // STEP_BLOCKS),
        in_specs=[pl.BlockSpec(memory_space=pltpu.SMEM),
                  pl.BlockSpec((2, QB, QB), lambda b, i: (0, 0, 0)),
                  pl.BlockSpec((1, STEP_BLOCKS * QB, width), lambda b, i: (b, i, T_QC * LANES // width)),
                  pl.BlockSpec((1, seq, width), lambda b, i: (b, 0, T_KC * LANES // width)),
                  pl.BlockSpec((1, seq, width), lambda b, i: (b, 0, T_VC * LANES // width)),
                  pl.BlockSpec((1, STEP_BLOCKS, nq, QB, QB), lambda b, i: (b, i, 0, 0, 0))],
        out_specs=pl.BlockSpec((1, STEP_BLOCKS * QB, width), lambda b, i: (b, i, 0)),
        out_shape=jax.ShapeDtypeStruct((batch, seq, width), BF16),
        scratch_shapes=[pltpu.VMEM((seq, width), BF16),
                        pltpu.VMEM((seq, width), BF16),
                        pltpu.VMEM((2 * ((C_HEADS + 1) // 2), 2 * QB, QB), F32)],
        compiler_params=_cparams("arbitrary", "arbitrary"),
        name="dsa_attn",
    )(rel_bias, jnp.asarray(bk), proj, proj, proj, sel)


def _swap_halves(w, rope):
    half = rope // 2
    return jnp.concatenate([-w[..., half:rope], w[..., :half], jnp.zeros_like(w[..., rope:])], axis=-1)


def _pad_cols(w, n):
    return jnp.concatenate([w, jnp.zeros(w.shape[:-1] + (n - w.shape[-1],), w.dtype)], axis=-1)


def _layout_w_in(w):
    D = w.shape[0]
    sizes = (A_HEADS * HEAD_DIM,) * 3 + (MLA_Q_RANK, MLA_KV_RANK, MLA_ROPE) + (C_HEADS * HEAD_DIM,) * 3 + (
        IDX_HEADS * IDX_DIM, IDX_DIM, IDX_HEADS)
    offs = np.cumsum((0,) + sizes)
    qa, ka, va, cq, ckv, kr, qc, kc, vc, iq, ik, iw = [w[:, offs[n]:offs[n + 1]] for n in range(len(sizes))]
    c_w = ((C_HEADS + 1) // 2) * LANES
    iq_h = iq.reshape(D, IDX_HEADS, IDX_DIM)
    iq_sw = _swap_halves(iq_h, IDX_ROPE).reshape(D, IDX_HEADS * IDX_DIM)
    ik_sw = _swap_halves(ik, IDX_ROPE)[:, :IDX_ROPE]
    assert (L_IKS, L_IW, L_KR, L_KRS) == (0, IDX_ROPE, MLA_NOPE, MLA_NOPE + MLA_ROPE)
    small = jnp.concatenate([ik_sw, _pad_cols(iw, L_KR - L_IW), kr, _swap_halves(kr, MLA_ROPE)], axis=-1)
    cols = [qa, ka, va, _pad_cols(qc, c_w), _pad_cols(kc, c_w), _pad_cols(vc, c_w), cq, iq, iq_sw, ckv,
            small, jnp.concatenate([ik, ik], axis=-1)]
    out = jnp.concatenate(cols, axis=-1)
    assert out.shape[1] == N_TILES * LANES
    return out.astype(BF16)


def _layout_mla(w_uq, w_ukv):
    rq = w_uq.reshape(MLA_Q_RANK, B_HEADS, MLA_NOPE + MLA_ROPE)
    nope, rope = rq[..., :MLA_NOPE], rq[..., MLA_NOPE:]
    wq = _pad_cols(jnp.concatenate([nope, rope], axis=-1), LANES)
    wqs = _pad_cols(jnp.concatenate([jnp.zeros_like(nope), _swap_halves(rope, MLA_ROPE)], axis=-1), LANES)
    rkv = w_ukv.reshape(MLA_KV_RANK, B_HEADS, MLA_NOPE + HEAD_DIM)
    wk = _pad_cols(rkv[..., :MLA_NOPE], LANES)
    wv = _pad_cols(rkv[..., MLA_NOPE:], LANES)
    flat = lambda a, r: a.reshape(r, B_HEADS * LANES).astype(BF16)
    return flat(wq, MLA_Q_RANK), flat(wqs, MLA_Q_RANK), flat(wk, MLA_KV_RANK), flat(wv, MLA_KV_RANK)


def _layout_w_out(w):
    D = w.shape[1]
    a, b_, c_ = A_HEADS * HEAD_DIM, B_HEADS * HEAD_DIM, C_HEADS * HEAD_DIM
    def pad_rows(x, heads):
        return jnp.concatenate([x, jnp.zeros((((heads + 1) // 2) * LANES - x.shape[0], D), w.dtype)], axis=0)

    wa = w[:a]
    wb = pad_rows(w[a:a + b_], B_HEADS)
    wc = pad_rows(w[a + b_:a + b_ + c_], C_HEADS)
    return jnp.concatenate([wa, wc, wb], axis=0).astype(BF16)


def kernel(x, c, positions, rel_bias, ada_w, ada_b, norm_ffn1, ffn1_w_gu, ffn1_w_down, norm_mix, w_in, mla_q_norm, mla_w_uq, mla_kv_norm, mla_w_ukv, idx_k_norm, w_out, norm_ffn2, ffn2_w_gu, ffn2_w_down, final_norm):
    B, T, D = x.shape
    L = ada_w.shape[0]
    assert T % (max(DILATIONS) * BLK) == 0 and T % 512 == 0 and D % LANES == 0

    mods = _modulation(c, ada_w, ada_b).reshape(L, B, N_MOD, D)
    cos_i, sin_i = _rope_tables(positions)
    cos_m, sin_m = cos_i, sin_i
    final_gain = final_norm.reshape(1, D)
    h = x.reshape(B * T, D)
    for l in range(L):
        mod = mods[l]
        h = _ffn(h, mod, norm_ffn1[l].reshape(1, D), ffn1_w_gu[l].astype(BF16), ffn1_w_down[l].astype(BF16),
                 final_gain, ks=0, final=False, seq=T)

        proj = _in_proj(h, mod, norm_mix[l].reshape(1, D), _layout_w_in(w_in[l]), seq=T)
        proj = proj.reshape(B, T, N_TILES * LANES)
        oa = _dilated_attention(proj, rel_bias, batch=B, seq=T)
        wq, wqs, wk, wv = _layout_mla(mla_w_uq[l], mla_w_ukv[l])
        ob = _mla_attention(proj, cos_m, sin_m, mla_q_norm[l].reshape(1, -1), mla_kv_norm[l].reshape(1, -1),
                            wq, wqs, wk, wv, batch=B, seq=T)
        g = idx_k_norm[l]
        g_sw = jnp.concatenate([g[IDX_ROPE // 2:IDX_ROPE], g[:IDX_ROPE // 2], jnp.zeros_like(g[IDX_ROPE:])])
        ik_gains = jnp.stack([jnp.concatenate([g, g]), jnp.concatenate([g_sw, g_sw])])
        sel = _dsa_select(proj, cos_i, sin_i, ik_gains, batch=B, seq=T)
        oc = _dsa_attention(proj, sel, rel_bias, batch=B, seq=T)
        h = _mix_ffn(oa.reshape(B * T, -1), oc.reshape(B * T, -1), ob.reshape(B * T, -1), _layout_w_out(w_out[l]), h, mod,
                     norm_ffn2[l].reshape(1, D), ffn2_w_gu[l].astype(BF16), ffn2_w_down[l].astype(BF16),
                     final_gain, final=(l == L - 1), seq=T)
    return h.reshape(B, T, D)
```
